```python
import math
import jax, jax.numpy as jnp
from jax import lax
import numpy as np

D_MODEL = 2048
BATCH = 2
SEQ = 4096
DEPTH = 1

N_HEADS = 8
HEAD_DIM = 128
N_KV_HEADS = 2
Q_PER_KV = N_HEADS // N_KV_HEADS
ATTN_WIDTH = N_HEADS * HEAD_DIM
KV_WIDTH = N_KV_HEADS * HEAD_DIM
IDX_HEADS = 16
IDX_DIM = 64
IDX_TOPK_MAX = 256
Q_BLOCK = 128
POOL_WINDOWS = (2, 4, 8, 16)
POOL_GROUPS = 4
POOL_WIDTH = 1024
POOL_GROUP_DIM = POOL_WIDTH // POOL_GROUPS
N_BRANCHES = 2
IN_SPLITS = (ATTN_WIDTH, KV_WIDTH, KV_WIDTH, IDX_HEADS * IDX_DIM, IDX_DIM, IDX_HEADS,
             POOL_WIDTH, N_BRANCHES * D_MODEL)
IN_WIDTH = 7760
N_EXPERTS = 64
N_EXPERT_GROUPS = 8
EXPERTS_PER_GROUP = N_EXPERTS // N_EXPERT_GROUPS
TOPK_GROUPS = 4
EXPERT_TOPK = 8
EXPERT_HIDDEN = 512
SHARED_HIDDEN = 512
ROUTED_SCALE = 2.5
MOE_BLOCK = 128
NORM_EPS = 1e-6
N_MOD = 6

kernel_name = "hybrid_dsa_pool_moe_adaln_block"


def rms_norm(x, g):
    xf = x.astype(jnp.float32)
    y = xf * lax.rsqrt(jnp.mean(xf * xf, axis=-1, keepdims=True) + NORM_EPS)
    return (y * g.astype(jnp.float32)).astype(x.dtype)


def alibi_slopes():
    s = np.array([2.0 ** (-8.0 * (i + 1) / N_HEADS) for i in range(N_HEADS)], dtype=np.float32)
    return jnp.asarray(s)


def sparse_indexed_attention(q, k, v, qi, ki, wi, q_norm, k_norm):
    B, S = q.shape[0], q.shape[1]
    topk = min(IDX_TOPK_MAX, S // 4)
    q = rms_norm(q.reshape(B, S, N_HEADS, HEAD_DIM), q_norm).reshape(B, S, N_KV_HEADS, Q_PER_KV, HEAD_DIM)
    k = rms_norm(k.reshape(B, S, N_KV_HEADS, HEAD_DIM), k_norm)
    v = v.reshape(B, S, N_KV_HEADS, HEAD_DIM)
    qi = qi.reshape(B, S, IDX_HEADS, IDX_DIM)
    wi = wi.astype(jnp.float32) * (IDX_HEADS ** -0.5 * IDX_DIM ** -0.5)
    slopes = alibi_slopes().reshape(N_KV_HEADS, Q_PER_KV)
    key_pos = jnp.arange(S)
    attn_scale = HEAD_DIM ** -0.5

    def block(i):
        start = i * Q_BLOCK
        qb = lax.dynamic_slice_in_dim(q, start, Q_BLOCK, axis=1)
        qib = lax.dynamic_slice_in_dim(qi, start, Q_BLOCK, axis=1)
        wib = lax.dynamic_slice_in_dim(wi, start, Q_BLOCK, axis=1)
        qpos = start + jnp.arange(Q_BLOCK)
        dots = jnp.einsum('bthd,bsd->bths', qib, ki).astype(jnp.float32)
        isc = jnp.einsum('bths,bth->bts', jax.nn.relu(dots), wib)
        causal = key_pos[None, :] <= qpos[:, None]
        isc = jnp.where(causal[None], isc, -jnp.inf)
        _, idx = lax.top_k(isc, topk)
        valid = idx <= qpos[None, :, None]
        k_sel = jax.vmap(lambda kb, ib: kb[ib])(k, idx)
        v_sel = jax.vmap(lambda vb, ib: vb[ib])(v, idx)
        logits = jnp.einsum('btngd,btjnd->btngj', qb, k_sel).astype(jnp.float32) * attn_scale
        dist = (qpos[None, :, None] - idx).astype(jnp.float32)
        logits = logits - slopes[None, None, :, :, None] * dist[:, :, None, None, :]
        logits = jnp.where(valid[:, :, None, None, :], logits, -jnp.inf)
        p = jax.nn.softmax(logits, axis=-1).astype(v.dtype)
        o = jnp.einsum('btngj,btjnd->btngd', p, v_sel)
        return o.reshape(B, Q_BLOCK, ATTN_WIDTH)

    out = lax.map(block, jnp.arange(S // Q_BLOCK))
    return out.transpose(1, 0, 2, 3).reshape(B, S, ATTN_WIDTH)


def causal_pool_minus_identity(u, window):
    S = u.shape[1]
    uf = u.astype(jnp.float32)
    cs = jnp.pad(jnp.cumsum(uf, axis=1), ((0, 0), (1, 0), (0, 0)))
    t = jnp.arange(S)
    lo = jnp.maximum(t + 1 - window, 0)
    win_sum = cs[:, t + 1] - cs[:, lo]
    count = (t + 1 - lo).astype(jnp.float32)
    return (win_sum / count[None, :, None] - uf).astype(u.dtype)


def multiscale_pool(u, pool_lin, pool_scale):
    B, S = u.shape[0], u.shape[1]
    ug = u.reshape(B, S, POOL_GROUPS, POOL_GROUP_DIM)
    pooled = jnp.stack([causal_pool_minus_identity(ug[:, :, g], POOL_WINDOWS[g])
                        for g in range(POOL_GROUPS)], axis=2)
    mixed = jnp.einsum('bsgc,gcd->bsgd', pooled, pool_lin).reshape(B, S, POOL_WIDTH)
    return mixed * pool_scale


def moe_ffn(h, w_router, router_bias, w1, w3, w2, ws1, ws3, ws2):
    B, S, D = h.shape
    N = B * S
    t = h.reshape(N, D)
    scores = jax.nn.sigmoid(jnp.dot(t, w_router).astype(jnp.float32))
    sel = scores + router_bias.astype(jnp.float32)
    grp_score = lax.top_k(sel.reshape(N, N_EXPERT_GROUPS, EXPERTS_PER_GROUP), 2)[0].sum(-1)
    _, top_g = lax.top_k(grp_score, TOPK_GROUPS)
    gmask = jnp.any(top_g[:, :, None] == jnp.arange(N_EXPERT_GROUPS)[None, None, :], axis=1)
    sel = jnp.where(jnp.repeat(gmask, EXPERTS_PER_GROUP, axis=1), sel, -jnp.inf)
    _, top_e = lax.top_k(sel, EXPERT_TOPK)
    gate = jnp.take_along_axis(scores, top_e, axis=1)
    gate = gate / jnp.sum(gate, axis=-1, keepdims=True) * ROUTED_SCALE
    NK = N * EXPERT_TOPK
    flat_e = top_e.reshape(-1)
    flat_tok = jnp.repeat(jnp.arange(N, dtype=jnp.int32), EXPERT_TOPK)
    flat_g = gate.reshape(-1)
    order = jnp.argsort(flat_e)
    sorted_e = flat_e[order]
    counts = jnp.bincount(flat_e, length=N_EXPERTS)
    padded = (counts + MOE_BLOCK - 1) // MOE_BLOCK * MOE_BLOCK
    pend = jnp.cumsum(padded)
    pstart = pend - padded
    ustart = jnp.cumsum(counts) - counts
    dest = pstart[sorted_e] + (jnp.arange(NK) - ustart[sorted_e])
    n_blocks = -(-(NK + N_EXPERTS * (MOE_BLOCK - 1)) // MOE_BLOCK)
    P = n_blocks * MOE_BLOCK
    buf_tok = jnp.zeros((P,), jnp.int32).at[dest].set(flat_tok[order])
    buf_gate = jnp.zeros((P,), jnp.float32).at[dest].set(flat_g[order])
    block_e = jnp.minimum(jnp.searchsorted(pend, jnp.arange(n_blocks) * MOE_BLOCK, side='right'),
                          N_EXPERTS - 1)

    def expert_block(args):
        e, tok, g = args
        xb = t[tok]
        y = jnp.dot(jax.nn.silu(jnp.dot(xb, w1[e])) * jnp.dot(xb, w3[e]), w2[e])
        return y * g[:, None].astype(y.dtype)

    ys = lax.map(expert_block, (block_e, buf_tok.reshape(n_blocks, MOE_BLOCK),
                                buf_gate.reshape(n_blocks, MOE_BLOCK)))
    routed = jax.ops.segment_sum(ys.reshape(P, D), buf_tok, num_segments=N)
    shared = jnp.dot(jax.nn.silu(jnp.dot(t, ws1)) * jnp.dot(t, ws3), ws2)
    return (routed + shared).reshape(B, S, D)


def hybrid_layer(x, c, w_ada, b_ada, g_mix, w_in, q_norm, k_norm, w_attn_up, pool_lin,
                 pool_scale, w_pool_up, w_out, g_ffn, w_router, router_bias, w1, w3, w2,
                 ws1, ws3, ws2):
    mod = jnp.dot(jax.nn.silu(c), w_ada) + b_ada
    shift_m, scale_m, gate_m, shift_f, scale_f, gate_f = jnp.split(mod[:, None, :], N_MOD, axis=-1)
    h = rms_norm(x, g_mix) * (1 + scale_m) + shift_m
    proj = jnp.dot(h, w_in)
    offs = np.cumsum(IN_SPLITS)[:-1].tolist()
    q, k, v, qi, ki, wi, u, gates = jnp.split(proj, offs, axis=-1)
    attn = sparse_indexed_attention(q, k, v, qi, ki, wi, q_norm, k_norm)
    pool = multiscale_pool(u, pool_lin, pool_scale)
    g_attn, g_pool = jnp.split(jax.nn.sigmoid(gates), N_BRANCHES, axis=-1)
    merged = g_attn * jnp.dot(attn, w_attn_up) + g_pool * jnp.dot(pool, w_pool_up)
    x = x + gate_m * jnp.dot(merged, w_out)
    h2 = rms_norm(x, g_ffn) * (1 + scale_f) + shift_f
    x = x + gate_f * moe_ffn(h2, w_router, router_bias, w1, w3, w2, ws1, ws3, ws2)
    return x


def setup_inputs(seed: int = 0) -> dict:
    key = jax.random.key(seed)
    ks = jax.random.split(key, 24)

    def nrm(k, shape, scale):
        return jax.random.normal(k, shape, jnp.float32) * scale

    L = DEPTH
    D = D_MODEL
    return {
        "x": nrm(ks[0], (BATCH, SEQ, D), 1.0),
        "c": nrm(ks[1], (BATCH, D), 1.0),
        "w_ada": nrm(ks[2], (L, D, N_MOD * D), 0.5 * D ** -0.5),
        "b_ada": nrm(ks[3], (L, N_MOD * D), 0.02),
        "g_mix": 1.0 + nrm(ks[4], (L, D), 0.02),
        "w_in": nrm(ks[5], (L, D, IN_WIDTH), D ** -0.5),
        "q_norm": 1.0 + nrm(ks[6], (L, HEAD_DIM), 0.02),
        "k_norm": 1.0 + nrm(ks[7], (L, HEAD_DIM), 0.02),
        "w_attn_up": nrm(ks[8], (L, ATTN_WIDTH, D), ATTN_WIDTH ** -0.5),
        "pool_lin": nrm(ks[9], (L, POOL_GROUPS, POOL_GROUP_DIM, POOL_GROUP_DIM), POOL_GROUP_DIM ** -0.5),
        "pool_scale": 1.0 + nrm(ks[10], (L, POOL_WIDTH), 0.02),
        "w_pool_up": nrm(ks[11], (L, POOL_WIDTH, D), POOL_WIDTH ** -0.5),
        "w_out": nrm(ks[12], (L, D, D), D ** -0.5),
        "g_ffn": 1.0 + nrm(ks[13], (L, D), 0.02),
        "w_router": nrm(ks[14], (L, D, N_EXPERTS), D ** -0.5),
        "router_bias": nrm(ks[15], (L, N_EXPERTS), 0.01),
        "w1": nrm(ks[16], (L, N_EXPERTS, D, EXPERT_HIDDEN), D ** -0.5),
        "w3": nrm(ks[17], (L, N_EXPERTS, D, EXPERT_HIDDEN), D ** -0.5),
        "w2": nrm(ks[18], (L, N_EXPERTS, EXPERT_HIDDEN, D), EXPERT_HIDDEN ** -0.5),
        "ws1": nrm(ks[19], (L, D, SHARED_HIDDEN), D ** -0.5),
        "ws3": nrm(ks[20], (L, D, SHARED_HIDDEN), D ** -0.5),
        "ws2": nrm(ks[21], (L, SHARED_HIDDEN, D), SHARED_HIDDEN ** -0.5),
    }


def reference(x, c, w_ada, b_ada, g_mix, w_in, q_norm, k_norm, w_attn_up, pool_lin, pool_scale,
              w_pool_up, w_out, g_ffn, w_router, router_bias, w1, w3, w2, ws1, ws3, ws2):
    for l in range(DEPTH):
        x = hybrid_layer(x, c, w_ada[l], b_ada[l], g_mix[l], w_in[l], q_norm[l], k_norm[l],
                         w_attn_up[l], pool_lin[l], pool_scale[l], w_pool_up[l], w_out[l],
                         g_ffn[l], w_router[l], router_bias[l], w1[l], w3[l], w2[l],
                         ws1[l], ws3[l], ws2[l])
    return x
```

```python
import functools

import jax
import jax.numpy as jnp
from jax import lax
from jax.experimental import pallas as pl
from jax.experimental.pallas import tpu as pltpu

f32 = jnp.float32
bf16 = jnp.bfloat16
i32 = jnp.int32

N_HEADS = 8
HEAD_DIM = 128
N_KV_HEADS = 2
Q_PER_KV = N_HEADS // N_KV_HEADS
ATTN_WIDTH = N_HEADS * HEAD_DIM
KV_WIDTH = N_KV_HEADS * HEAD_DIM
IDX_HEADS = 16
IDX_DIM = 64
IDX_TOPK_MAX = 256
Q_BLOCK = 128
POOL_WINDOWS = (2, 4, 8, 16)
POOL_GROUP_DIM = 256
POOL_WIDTH = 1024
POOL_HALO = 16
N_EXPERTS = 64
N_EXPERT_GROUPS = 8
EXPERTS_PER_GROUP = 8
TOPK_GROUPS = 4
EXPERT_TOPK = 8
EXPERT_HIDDEN = 512
ROUTED_SCALE = 2.5
NORM_EPS = 1e-6
N_MOD = 6

COL_GATE_A = 0
COL_GATE_P = 2048
COL_Q = 4096
COL_QI = 5120
COL_U = 6144
COL_K = 7168
COL_V = 7424
MAIN_WIDTH = 7680
KIWI_WIDTH = 128

KEY_CHUNK = 512
EXPERT_TILE = 256
VMEM_LIMIT = 56 * 1024 * 1024
INT_MIN = -2147483648
NEG_BIG = -1e30


def _cparams(n_axes, vmem=VMEM_LIMIT):
    return pltpu.CompilerParams(dimension_semantics=("arbitrary",) * n_axes, vmem_limit_bytes=vmem)


def _sigmoid(x):
    return 1.0 / (1.0 + jnp.exp(-x))


def _silu(x):
    return x * _sigmoid(x)


def _pack_pair(lo, hi):
    lo_b = lax.bitcast_convert_type(lo.astype(bf16).astype(f32), i32)
    hi_b = lax.bitcast_convert_type(hi.astype(bf16).astype(f32), i32)
    return lax.shift_right_logical(lo_b, 16) | (hi_b & jnp.int32(-65536))


def _unpack_pair(p):
    lo = lax.bitcast_convert_type(lax.shift_left(p, 16), f32)
    hi = lax.bitcast_convert_type(p & jnp.int32(-65536), f32)
    return lo, hi


def _ada_kernel(c_ref, w_ref, b_ref, o_ref):
    sc = _silu(c_ref[...]).astype(bf16)
    o_ref[...] = jnp.dot(sc, w_ref[...].astype(bf16), preferred_element_type=f32) + b_ref[...]


def _ada(c8, w_ada, b_ada):
    d, n = w_ada.shape
    tn = 1024
    return pl.pallas_call(
        _ada_kernel,
        grid=(n // tn,),
        in_specs=[pl.BlockSpec((8, d), lambda j: (0, 0)),
                  pl.BlockSpec((d, tn), lambda j: (0, j)),
                  pl.BlockSpec((1, tn), lambda j: (0, j))],
        out_specs=pl.BlockSpec((8, tn), lambda j: (0, j)),
        out_shape=jax.ShapeDtypeStruct((8, n), f32),
        compiler_params=_cparams(1),
        name="adaln",
    )(c8, w_ada, b_ada)


def _inproj_kernel(x_ref, mod_ref, g_ref, w_ref, wk_ref, o_ref, kiwi_ref, h_scr):
    @pl.when(pl.program_id(1) == 0)
    def _():
        x = x_ref[...]
        y = x * lax.rsqrt(jnp.mean(x * x, axis=-1, keepdims=True) + NORM_EPS) * g_ref[...]
        h = y * (1.0 + mod_ref[0, 1:2, :]) + mod_ref[0, 0:1, :]
        hb = h.astype(bf16)
        h_scr[...] = hb
        kiwi_ref[...] = jnp.dot(hb, wk_ref[...], preferred_element_type=f32)

    o_ref[...] = jnp.dot(h_scr[...], w_ref[...], preferred_element_type=f32).astype(o_ref.dtype)


def _inproj(x2, mod8, g_mix, w_main, w_kiwi, seq):
    n, d = x2.shape
    tm, tn = 1024, 512
    tm = min(tm, seq)
    per_seq = seq // tm
    return pl.pallas_call(
        _inproj_kernel,
        grid=(n // tm, MAIN_WIDTH // tn),
        in_specs=[pl.BlockSpec((tm, d), lambda i, j: (i, 0)),
                  pl.BlockSpec((1, 8, d), lambda i, j: (i // per_seq, 0, 0)),
                  pl.BlockSpec((1, d), lambda i, j: (0, 0)),
                  pl.BlockSpec((d, tn), lambda i, j: (0, j)),
                  pl.BlockSpec((d, KIWI_WIDTH), lambda i, j: (0, 0))],
        out_specs=[pl.BlockSpec((tm, tn), lambda i, j: (i, j)),
                   pl.BlockSpec((tm, KIWI_WIDTH), lambda i, j: (i, 0))],
        out_shape=[jax.ShapeDtypeStruct((n, MAIN_WIDTH), bf16),
                   jax.ShapeDtypeStruct((n, KIWI_WIDTH), f32)],
        scratch_shapes=[pltpu.VMEM((tm, d), bf16)],
        compiler_params=_cparams(2),
        name="inproj",
    )(x2, mod8, g_mix, w_main, w_kiwi)


def _attn_kernel(q_ref, qi_ref, k_ref, v_ref, kiwi_all_ref, kiwi_blk_ref, qn_ref, kn_ref, o_ref,
                 kn_scr, ki_scr, keys_scr, qg_scr, m_scr, l_scr, acc_scr, *, seq, topk):
    i = pl.program_id(1)
    ck = min(KEY_CHUNK, seq)
    blocks_per_chunk = ck // Q_BLOCK
    n_chunks = i // blocks_per_chunk + 1
    nt = (((1,), (1,)), ((), ()))

    @pl.when(i == 0)
    def _prep_keys():
        for n in range(N_KV_HEADS):
            kf = k_ref[:, n * HEAD_DIM:(n + 1) * HEAD_DIM].astype(f32)
            r = lax.rsqrt(jnp.mean(kf * kf, axis=-1, keepdims=True) + NORM_EPS)
            kn_scr[:, n * HEAD_DIM:(n + 1) * HEAD_DIM] = (kf * r * kn_ref[...]).astype(bf16)
        ki_scr[...] = kiwi_all_ref[:, 0:IDX_DIM].astype(bf16)

    for h in range(N_HEADS):
        qf = q_ref[:, h * HEAD_DIM:(h + 1) * HEAD_DIM].astype(f32)
        r = lax.rsqrt(jnp.mean(qf * qf, axis=-1, keepdims=True) + NORM_EPS)
        n, g = divmod(h, Q_PER_KV)
        qg_scr[n, g * Q_BLOCK:(g + 1) * Q_BLOCK, :] = (qf * r * qn_ref[...]).astype(bf16)

    qpos = i * Q_BLOCK + lax.broadcasted_iota(i32, (Q_BLOCK, 1), 0)
    wi = kiwi_blk_ref[:, IDX_DIM:IDX_DIM + IDX_HEADS] * (IDX_HEADS ** -0.5 * IDX_DIM ** -0.5)

    def index_chunk(c, carry):
        start = pl.multiple_of(c * ck, ck)
        kc = ki_scr[pl.ds(start, ck), :]
        acc = jnp.zeros((Q_BLOCK, ck), f32)
        for h in range(IDX_HEADS):
            qh = qi_ref[:, h * IDX_DIM:(h + 1) * IDX_DIM]
            d = lax.dot_general(qh, kc, nt, preferred_element_type=f32)
            acc = acc + jnp.maximum(d, 0.0) * wi[:, h:h + 1]
        bits = lax.bitcast_convert_type(acc, i32)
        key = bits ^ (lax.shift_right_arithmetic(bits, 31) & jnp.int32(0x7FFFFFFF))
        kpos = start + lax.broadcasted_iota(i32, (1, ck), 1)
        keys_scr[c] = jnp.where(kpos <= qpos, key, jnp.int32(INT_MIN))
        return carry

    lax.fori_loop(0, n_chunks, index_chunk, 0)

    def bit_step(b, t_u):
        bit = lax.shift_left(jnp.int32(1), 31 - b)
        cand_u = t_u | bit
        cand_s = cand_u ^ jnp.int32(INT_MIN)

        def count_chunk(c, cnt):
            ge = jnp.where(keys_scr[c] >= cand_s, 1.0, 0.0)
            part = ge[:, 0:128]
            for s in range(1, ck // 128):
                part = part + ge[:, s * 128:(s + 1) * 128]
            return cnt + part

        cnt = lax.fori_loop(0, n_chunks, count_chunk, jnp.zeros((Q_BLOCK, 128), f32))
        total = jnp.sum(cnt, axis=1, keepdims=True)
        return jnp.where(total >= float(topk), cand_u, t_u)

    t_u = lax.fori_loop(0, 32, bit_step, jnp.zeros((Q_BLOCK, 1), i32))
    thr = jnp.maximum(t_u ^ jnp.int32(INT_MIN), jnp.int32(INT_MIN + 1))

    scale = HEAD_DIM ** -0.5
    for n in range(N_KV_HEADS):
        m_scr[...] = jnp.full(m_scr.shape, NEG_BIG, f32)
        l_scr[...] = jnp.zeros(l_scr.shape, f32)
        acc_scr[...] = jnp.zeros(acc_scr.shape, f32)
        qg = qg_scr[n]

        def attn_chunk(c, carry, n=n, qg=qg):
            start = pl.multiple_of(c * ck, ck)
            kc = kn_scr[pl.ds(start, ck), n * HEAD_DIM:(n + 1) * HEAD_DIM]
            vc = v_ref[pl.ds(start, ck), n * HEAD_DIM:(n + 1) * HEAD_DIM]
            s_all = lax.dot_general(qg, kc, nt, preferred_element_type=f32)
            sel = keys_scr[c] >= thr
            kpos = start + lax.broadcasted_iota(i32, (1, ck), 1)
            dist = (qpos - kpos).astype(f32)
            parts = []
            for g in range(Q_PER_KV):
                slope = 2.0 ** (-8.0 * (n * Q_PER_KV + g + 1) / N_HEADS)
                sg = s_all[g * Q_BLOCK:(g + 1) * Q_BLOCK, :] * scale - slope * dist
                parts.append(jnp.where(sel, sg, NEG_BIG))
            s = jnp.concatenate(parts, axis=0)
            m_old = m_scr[...]
            m_new = jnp.maximum(m_old, jnp.max(s, axis=-1, keepdims=True))
            alpha = jnp.exp(m_old - m_new)
            p = jnp.exp(s - m_new)
            l_scr[...] = alpha * l_scr[...] + jnp.sum(p, axis=-1, keepdims=True)
            acc_scr[...] = alpha * acc_scr[...] + jnp.dot(p.astype(bf16), vc, preferred_element_type=f32)
            m_scr[...] = m_new
            return carry

        lax.fori_loop(0, n_chunks, attn_chunk, 0)
        out = acc_scr[...] / l_scr[...]
        for g in range(Q_PER_KV):
            h = n * Q_PER_KV + g
            o_ref[:, h * HEAD_DIM:(h + 1) * HEAD_DIM] = out[g * Q_BLOCK:(g + 1) * Q_BLOCK, :].astype(o_ref.dtype)


def _attention(proj, kiwi, q_norm, k_norm, batch, seq):
    n = batch * seq
    nq = seq // Q_BLOCK
    ck = min(KEY_CHUNK, seq)
    topk = min(IDX_TOPK_MAX, seq // 4)
    kern = functools.partial(_attn_kernel, seq=seq, topk=topk)
    return pl.pallas_call(
        kern,
        grid=(batch, nq),
        in_specs=[pl.BlockSpec((Q_BLOCK, ATTN_WIDTH), lambda b, i: (b * nq + i, COL_Q // ATTN_WIDTH)),
                  pl.BlockSpec((Q_BLOCK, IDX_HEADS * IDX_DIM), lambda b, i: (b * nq + i, COL_QI // (IDX_HEADS * IDX_DIM))),
                  pl.BlockSpec((seq, KV_WIDTH), lambda b, i: (b, COL_K // KV_WIDTH)),
                  pl.BlockSpec((seq, KV_WIDTH), lambda b, i: (b, COL_V // KV_WIDTH)),
                  pl.BlockSpec((seq, KIWI_WIDTH), lambda b, i: (b, 0)),
                  pl.BlockSpec((Q_BLOCK, KIWI_WIDTH), lambda b, i: (b * nq + i, 0)),
                  pl.BlockSpec((1, HEAD_DIM), lambda b, i: (0, 0)),
                  pl.BlockSpec((1, HEAD_DIM), lambda b, i: (0, 0))],
        out_specs=pl.BlockSpec((Q_BLOCK, ATTN_WIDTH), lambda b, i: (b * nq + i, 0)),
        out_shape=jax.ShapeDtypeStruct((n, ATTN_WIDTH), bf16),
        scratch_shapes=[pltpu.VMEM((seq, KV_WIDTH), bf16),
                        pltpu.VMEM((seq, IDX_DIM), bf16),
                        pltpu.VMEM((seq // ck, Q_BLOCK, ck), i32),
                        pltpu.VMEM((N_KV_HEADS, Q_PER_KV * Q_BLOCK, HEAD_DIM), bf16),
                        pltpu.VMEM((Q_PER_KV * Q_BLOCK, 1), f32),
                        pltpu.VMEM((Q_PER_KV * Q_BLOCK, 1), f32),
                        pltpu.VMEM((Q_PER_KV * Q_BLOCK, HEAD_DIM), f32)],
        compiler_params=_cparams(2),
        name="sparse_attn",
    )(proj, proj, proj, proj, kiwi, kiwi, q_norm, k_norm)


def _pool_kernel(u_ref, halo_ref, lin_ref, ps_ref, o_ref, scr, *, tm, per_seq):
    i = pl.program_id(0)
    first = (i % per_seq) == 0
    scr[0:POOL_HALO, :] = jnp.where(first, 0.0, halo_ref[...].astype(f32))
    scr[POOL_HALO:POOL_HALO + tm, :] = u_ref[...].astype(f32)
    t_in_seq = (i % per_seq) * tm + lax.broadcasted_iota(i32, (tm, 1), 0)
    for g, w in enumerate(POOL_WINDOWS):
        c0, c1 = g * POOL_GROUP_DIM, (g + 1) * POOL_GROUP_DIM
        cur = scr[POOL_HALO:POOL_HALO + tm, c0:c1]
        s = cur
        for j in range(1, w):
            s = s + scr[POOL_HALO - j:POOL_HALO - j + tm, c0:c1]
        count = jnp.minimum(t_in_seq + 1, w).astype(f32)
        pooled = s / count - cur
        mixed = jnp.dot(pooled.astype(bf16), lin_ref[g], preferred_element_type=f32)
        o_ref[:, c0:c1] = (mixed * ps_ref[:, c0:c1]).astype(o_ref.dtype)


def _pool(proj, pool_lin_b, pool_scale, seq):
    n = proj.shape[0]
    tm = min(512, seq)
    per_seq = seq // tm
    hb = tm // POOL_HALO
    kern = functools.partial(_pool_kernel, tm=tm, per_seq=per_seq)
    return pl.pallas_call(
        kern,
        grid=(n // tm,),
        in_specs=[pl.BlockSpec((tm, POOL_WIDTH), lambda i: (i, COL_U // POOL_WIDTH)),
                  pl.BlockSpec((POOL_HALO, POOL_WIDTH), lambda i: (jnp.maximum(i * hb - 1, 0), COL_U // POOL_WIDTH)),
                  pl.BlockSpec((len(POOL_WINDOWS), POOL_GROUP_DIM, POOL_GROUP_DIM), lambda i: (0, 0, 0)),
                  pl.BlockSpec((1, POOL_WIDTH), lambda i: (0, 0))],
        out_specs=pl.BlockSpec((tm, POOL_WIDTH), lambda i: (i, 0)),
        out_shape=jax.ShapeDtypeStruct((n, POOL_WIDTH), bf16),
        scratch_shapes=[pltpu.VMEM((POOL_HALO + tm, POOL_WIDTH), f32)],
        compiler_params=_cparams(1),
        name="pool",
    )(proj, proj, pool_lin_b, pool_scale)


def _merge_kernel(a_ref, p_ref, wa_ref, wp_ref, ga_ref, gp_ref, o_ref):
    ya = jnp.dot(a_ref[...], wa_ref[...], preferred_element_type=f32)
    yp = jnp.dot(p_ref[...], wp_ref[...], preferred_element_type=f32)
    o = _sigmoid(ga_ref[...].astype(f32)) * ya + _sigmoid(gp_ref[...].astype(f32)) * yp
    o_ref[...] = o.astype(o_ref.dtype)


def _merge(attn, pool, w_au, w_pu, proj):
    n = attn.shape[0]
    d = w_au.shape[1]
    tm, tn = min(512, n), 512
    ga0, gp0 = COL_GATE_A // tn, COL_GATE_P // tn
    return pl.pallas_call(
        _merge_kernel,
        grid=(n // tm, d // tn),
        in_specs=[pl.BlockSpec((tm, ATTN_WIDTH), lambda i, j: (i, 0)),
                  pl.BlockSpec((tm, POOL_WIDTH), lambda i, j: (i, 0)),
                  pl.BlockSpec((ATTN_WIDTH, tn), lambda i, j: (0, j)),
                  pl.BlockSpec((POOL_WIDTH, tn), lambda i, j: (0, j)),
                  pl.BlockSpec((tm, tn), lambda i, j: (i, ga0 + j)),
                  pl.BlockSpec((tm, tn), lambda i, j: (i, gp0 + j))],
        out_specs=pl.BlockSpec((tm, tn), lambda i, j: (i, j)),
        out_shape=jax.ShapeDtypeStruct((n, d), bf16),
        compiler_params=_cparams(2),
        name="merge",
    )(attn, pool, w_au, w_pu, proj, proj)


def _outproj_kernel(m_ref, x_ref, mod_ref, g_ref, wo_ref, wr_ref, x1_ref, hp_ref, lg_ref):
    d = x_ref.shape[1]
    y = jnp.dot(m_ref[...], wo_ref[...], preferred_element_type=f32)
    x1 = x_ref[...] + mod_ref[0, 2:3, :] * y
    x1_ref[...] = x1
    hn = x1 * lax.rsqrt(jnp.mean(x1 * x1, axis=-1, keepdims=True) + NORM_EPS) * g_ref[...]
    h2 = hn * (1.0 + mod_ref[0, 4:5, :]) + mod_ref[0, 3:4, :]
    hp_ref[...] = _pack_pair(h2[:, 0:d // 2], h2[:, d // 2:d])
    lg_ref[...] = jnp.dot(h2, wr_ref[...], preferred_element_type=f32, precision=lax.Precision.HIGHEST)


def _outproj(merged, x2, mod8, g_ffn, w_out_b, w_router, seq):
    n, d = x2.shape
    tm = min(256, seq)
    per_seq = seq // tm
    return pl.pallas_call(
        _outproj_kernel,
        grid=(n // tm,),
        in_specs=[pl.BlockSpec((tm, d), lambda i: (i, 0)),
                  pl.BlockSpec((tm, d), lambda i: (i, 0)),
                  pl.BlockSpec((1, 8, d), lambda i: (i // per_seq, 0, 0)),
                  pl.BlockSpec((1, d), lambda i: (0, 0)),
                  pl.BlockSpec((d, d), lambda i: (0, 0)),
                  pl.BlockSpec((d, N_EXPERTS), lambda i: (0, 0))],
        out_specs=[pl.BlockSpec((tm, d), lambda i: (i, 0)),
                   pl.BlockSpec((tm, d // 2), lambda i: (i, 0)),
                   pl.BlockSpec((tm, N_EXPERTS), lambda i: (i, 0))],
        out_shape=[jax.ShapeDtypeStruct((n, d), f32),
                   jax.ShapeDtypeStruct((n, d // 2), i32),
                   jax.ShapeDtypeStruct((n, N_EXPERTS), f32)],
        compiler_params=_cparams(1),
        name="outproj",
    )(merged, x2, mod8, g_ffn, w_out_b, w_router)


def _route_select(lg_ref, bias_ref):
    s = _sigmoid(lg_ref[...])
    sel = s + bias_ref[...]
    rows = [sel[EXPERTS_PER_GROUP * g:EXPERTS_PER_GROUP * (g + 1), :] for g in range(N_EXPERT_GROUPS)]
    grp = []
    for r in rows:
        m1 = jnp.max(r, axis=0, keepdims=True)
        eq = r == m1
        n_eq = jnp.sum(jnp.where(eq, 1.0, 0.0), axis=0, keepdims=True)
        m2 = jnp.max(jnp.where(eq, -jnp.inf, r), axis=0, keepdims=True)
        grp.append(m1 + jnp.where(n_eq >= 2.0, m1, m2))
    masked = []
    for g in range(N_EXPERT_GROUPS):
        rank = jnp.zeros_like(grp[g])
        for g2 in range(N_EXPERT_GROUPS):
            if g2 == g:
                continue
            beats = (grp[g2] >= grp[g]) if g2 < g else (grp[g2] > grp[g])
            rank = rank + jnp.where(beats, 1.0, 0.0)
        masked.append(jnp.where(rank < float(TOPK_GROUPS), rows[g], -jnp.inf))
    masked = jnp.concatenate(masked, axis=0)
    eidx = lax.broadcasted_iota(i32, (N_EXPERTS, 1), 0)
    rank = jnp.zeros_like(masked)
    for e2 in range(N_EXPERTS):
        row = masked[e2:e2 + 1, :]
        beats = (row > masked) | ((row == masked) & (eidx > e2))
        rank = rank + jnp.where(beats, 1.0, 0.0)
    return s, rank, rank < float(EXPERT_TOPK)


def _route_count_kernel(lg_ref, bias_ref, cnt_ref):
    @pl.when(pl.program_id(0) == 0)
    def _():
        cnt_ref[...] = jnp.zeros(cnt_ref.shape, f32)

    _, _, selected = _route_select(lg_ref, bias_ref)
    cnt_ref[...] += jnp.sum(jnp.where(selected, 1.0, 0.0), axis=1, keepdims=True)


def _route_assign_kernel(lg_ref, bias_ref, pstart_ref, gate_ref, pos_ref, run_scr):
    @pl.when(pl.program_id(0) == 0)
    def _():
        run_scr[...] = jnp.zeros(run_scr.shape, f32)

    tn = lg_ref.shape[1]
    s, rank, selected = _route_select(lg_ref, bias_ref)
    sel_f = jnp.where(selected, 1.0, 0.0)
    earlier = lax.broadcasted_iota(i32, (tn, tn), 0) < lax.broadcasted_iota(i32, (tn, tn), 1)
    prefix = jnp.dot(sel_f.astype(bf16), jnp.where(earlier, 1.0, 0.0).astype(bf16), preferred_element_type=f32)
    pos = pstart_ref[...] + run_scr[...] + prefix
    run_scr[...] += jnp.sum(sel_f, axis=1, keepdims=True)
    gate = jnp.where(selected, s, 0.0)
    gate = gate / jnp.sum(gate, axis=0, keepdims=True) * ROUTED_SCALE
    for j in range(EXPERT_TOPK):
        slot = rank == float(j)
        gate_ref[j:j + 1, :] = jnp.sum(jnp.where(slot, gate, 0.0), axis=0, keepdims=True)
        pos_ref[j:j + 1, :] = jnp.sum(jnp.where(slot, pos, 0.0), axis=0, keepdims=True).astype(i32)


def _route_counts(logits_t, bias_col):
    e, n = logits_t.shape
    tn = min(512, n)
    return pl.pallas_call(
        _route_count_kernel,
        grid=(n // tn,),
        in_specs=[pl.BlockSpec((e, tn), lambda t: (0, t)),
                  pl.BlockSpec((e, 1), lambda t: (0, 0))],
        out_specs=pl.BlockSpec((e, 1), lambda t: (0, 0)),
        out_shape=jax.ShapeDtypeStruct((e, 1), f32),
        compiler_params=_cparams(1),
        name="route_count",
    )(logits_t, bias_col)


def _route_assign(logits_t, bias_col, pstart_col):
    e, n = logits_t.shape
    tn = min(512, n)
    return pl.pallas_call(
        _route_assign_kernel,
        grid=(n // tn,),
        in_specs=[pl.BlockSpec((e, tn), lambda t: (0, t)),
                  pl.BlockSpec((e, 1), lambda t: (0, 0)),
                  pl.BlockSpec((e, 1), lambda t: (0, 0))],
        out_specs=[pl.BlockSpec((EXPERT_TOPK, tn), lambda t: (0, t)),
                   pl.BlockSpec((EXPERT_TOPK, tn), lambda t: (0, t))],
        out_shape=[jax.ShapeDtypeStruct((EXPERT_TOPK, n), f32),
                   jax.ShapeDtypeStruct((EXPERT_TOPK, n), i32)],
        scratch_shapes=[pltpu.VMEM((e, 1), f32)],
        compiler_params=_cparams(1),
        name="route_assign",
    )(logits_t, bias_col, pstart_col)


def _dispatch_kernel(pos_ref, h_ref, xs_in_ref, xs_ref, sem, *, n_tokens, tn):
    del xs_in_ref
    base = pl.program_id(0) * tn

    def row_copy(t, p):
        return pltpu.make_async_copy(h_ref.at[pl.ds(t, 1), :], xs_ref.at[pl.ds(p, 1), :], sem)

    def issue(t, carry):
        for j in range(EXPERT_TOPK):
            row_copy(t, pos_ref[j * n_tokens + base + t]).start()
        return carry

    lax.fori_loop(0, tn, issue, 0)

    def drain(t, carry):
        for j in range(EXPERT_TOPK):
            row_copy(t, pos_ref[j * n_tokens + base + t]).wait()
        return carry

    lax.fori_loop(0, tn, drain, 0)


def _dispatch(pos_flat, h2p, xs_init):
    n, w = h2p.shape
    tn = min(256, n)
    kern = functools.partial(_dispatch_kernel, n_tokens=n, tn=tn)
    return pl.pallas_call(
        kern,
        grid=(n // tn,),
        in_specs=[pl.BlockSpec(memory_space=pltpu.SMEM),
                  pl.BlockSpec((tn, w), lambda i: (i, 0)),
                  pl.BlockSpec(memory_space=pl.ANY)],
        out_specs=pl.BlockSpec(memory_space=pl.ANY),
        out_shape=jax.ShapeDtypeStruct(xs_init.shape, xs_init.dtype),
        scratch_shapes=[pltpu.SemaphoreType.DMA(())],
        input_output_aliases={2: 0},
        compiler_params=_cparams(1),
        name="dispatch",
    )(pos_flat, h2p, xs_init)


def _expert_kernel(be_ref, nu_ref, xs_ref, w1_ref, w3_ref, w2_ref, ys_ref, w1b, w3b, w2b):
    i = pl.program_id(0)
    e = be_ref[i]
    prev = be_ref[jnp.maximum(i - 1, 0)]
    half = w1b.shape[0] // 2

    @pl.when((i == 0) | (e != prev))
    def _cast_weights():
        w1b[...] = w1_ref[0].astype(bf16)
        w3b[...] = w3_ref[0].astype(bf16)
        w2b[...] = w2_ref[0].astype(bf16)

    @pl.when(i < nu_ref[0])
    def _compute():
        lo, hi = _unpack_pair(xs_ref[...])
        lo = lo.astype(bf16)
        hi = hi.astype(bf16)
        h1 = (jnp.dot(lo, w1b[0:half, :], preferred_element_type=f32)
              + jnp.dot(hi, w1b[half:2 * half, :], preferred_element_type=f32))
        h3 = (jnp.dot(lo, w3b[0:half, :], preferred_element_type=f32)
              + jnp.dot(hi, w3b[half:2 * half, :], preferred_element_type=f32))
        a = (_silu(h1) * h3).astype(bf16)
        y = jnp.dot(a, w2b[...], preferred_element_type=f32)
        ys_ref[...] = _pack_pair(y[:, 0:half], y[:, half:2 * half])

    @pl.when(i >= nu_ref[0])
    def _unused():
        ys_ref[...] = jnp.zeros(ys_ref.shape, i32)


def _experts(block_e, n_used, xs, w1, w3, w2):
    p, w = xs.shape
    _, d, hdim = w1.shape
    t = EXPERT_TILE
    grid_spec = pltpu.PrefetchScalarGridSpec(
        num_scalar_prefetch=2,
        grid=(p // t,),
        in_specs=[pl.BlockSpec((t, w), lambda i, be, nu: (i, 0)),
                  pl.BlockSpec((1, d, hdim), lambda i, be, nu: (be[i], 0, 0)),
                  pl.BlockSpec((1, d, hdim), lambda i, be, nu: (be[i], 0, 0)),
                  pl.BlockSpec((1, hdim, d), lambda i, be, nu: (be[i], 0, 0))],
        out_specs=pl.BlockSpec((t, w), lambda i, be, nu: (i, 0)),
        scratch_shapes=[pltpu.VMEM((d, hdim), bf16), pltpu.VMEM((d, hdim), bf16), pltpu.VMEM((hdim, d), bf16)],
    )
    return pl.pallas_call(
        _expert_kernel,
        grid_spec=grid_spec,
        out_shape=jax.ShapeDtypeStruct((p, w), i32),
        compiler_params=_cparams(1),
        name="experts",
    )(block_e, n_used, xs, w1, w3, w2)


def _combine_kernel(pos_ref, gate_ref, hp_ref, x1_ref, mod_ref, ws1_ref, ws3_ref, ws2_ref, ys_ref, o_ref,
                    gbuf, sem, *, n_tokens, tn):
    base = pl.program_id(0) * tn
    half = hp_ref.shape[1]

    def row_copy(j, t, p):
        return pltpu.make_async_copy(ys_ref.at[pl.ds(p, 1), :], gbuf.at[j, pl.ds(t, 1), :], sem)

    def issue(t, carry):
        for j in range(EXPERT_TOPK):
            row_copy(j, t, pos_ref[j * n_tokens + base + t]).start()
        return carry

    lax.fori_loop(0, tn, issue, 0)

    lo, hi = _unpack_pair(hp_ref[...])
    lo = lo.astype(bf16)
    hi = hi.astype(bf16)
    h1 = (jnp.dot(lo, ws1_ref[0:half, :], preferred_element_type=f32)
          + jnp.dot(hi, ws1_ref[half:2 * half, :], preferred_element_type=f32))
    h3 = (jnp.dot(lo, ws3_ref[0:half, :], preferred_element_type=f32)
          + jnp.dot(hi, ws3_ref[half:2 * half, :], preferred_element_type=f32))
    a = (_silu(h1) * h3).astype(bf16)
    shared = jnp.dot(a, ws2_ref[...], preferred_element_type=f32)

    def drain(t, carry):
        for j in range(EXPERT_TOPK):
            row_copy(j, t, pos_ref[j * n_tokens + base + t]).wait()
        return carry

    lax.fori_loop(0, tn, drain, 0)

    r_lo = jnp.zeros((tn, half), f32)
    r_hi = jnp.zeros((tn, half), f32)
    for j in range(EXPERT_TOPK):
        y_lo, y_hi = _unpack_pair(gbuf[j])
        g = gate_ref[:, j:j + 1]
        r_lo = r_lo + g * y_lo
        r_hi = r_hi + g * y_hi
    gate_f = mod_ref[0, 5:6, :]
    o_ref[:, 0:half] = x1_ref[:, 0:half] + gate_f[:, 0:half] * (r_lo + shared[:, 0:half])
    o_ref[:, half:2 * half] = x1_ref[:, half:2 * half] + gate_f[:, half:2 * half] * (r_hi + shared[:, half:2 * half])


def _combine(pos_flat, gate_t, h2p, x1, mod8, ws1b, ws3b, ws2b, ys, seq):
    n, d = x1.shape
    tn = min(256, seq)
    per_seq = seq // tn
    hdim = ws1b.shape[1]
    kern = functools.partial(_combine_kernel, n_tokens=n, tn=tn)
    return pl.pallas_call(
        kern,
        grid=(n // tn,),
        in_specs=[pl.BlockSpec(memory_space=pltpu.SMEM),
                  pl.BlockSpec((tn, EXPERT_TOPK), lambda i: (i, 0)),
                  pl.BlockSpec((tn, d // 2), lambda i: (i, 0)),
                  pl.BlockSpec((tn, d), lambda i: (i, 0)),
                  pl.BlockSpec((1, 8, d), lambda i: (i // per_seq, 0, 0)),
                  pl.BlockSpec((d, hdim), lambda i: (0, 0)),
                  pl.BlockSpec((d, hdim), lambda i: (0, 0)),
                  pl.BlockSpec((hdim, d), lambda i: (0, 0)),
                  pl.BlockSpec(memory_space=pl.ANY)],
        out_specs=pl.BlockSpec((tn, d), lambda i: (i, 0)),
        out_shape=jax.ShapeDtypeStruct((n, d), f32),
        scratch_shapes=[pltpu.VMEM((EXPERT_TOPK, tn, d // 2), i32), pltpu.SemaphoreType.DMA(())],
        compiler_params=_cparams(1),
        name="combine",
    )(pos_flat, gate_t, h2p, x1, mod8, ws1b, ws3b, ws2b, ys)


def _layer(x, c, w_ada, b_ada, g_mix, w_in, q_norm, k_norm, w_attn_up, pool_lin, pool_scale, w_pool_up,
           w_out, g_ffn, w_router, router_bias, w1, w3, w2, ws1, ws3, ws2):
    batch, seq, d = x.shape
    n = batch * seq
    x2 = x.reshape(n, d)

    offs = [0, ATTN_WIDTH, ATTN_WIDTH + KV_WIDTH, ATTN_WIDTH + 2 * KV_WIDTH]
    w_q = w_in[:, offs[0]:offs[1]]
    w_k = w_in[:, offs[1]:offs[2]]
    w_v = w_in[:, offs[2]:offs[3]]
    o_qi = offs[3]
    w_qi = w_in[:, o_qi:o_qi + IDX_HEADS * IDX_DIM]
    o_ki = o_qi + IDX_HEADS * IDX_DIM
    w_ki = w_in[:, o_ki:o_ki + IDX_DIM]
    o_wi = o_ki + IDX_DIM
    w_wi = w_in[:, o_wi:o_wi + IDX_HEADS]
    o_u = o_wi + IDX_HEADS
    w_u = w_in[:, o_u:o_u + POOL_WIDTH]
    o_g = o_u + POOL_WIDTH
    w_g = w_in[:, o_g:o_g + 2 * d]
    w_main = jnp.concatenate([w_g, w_q, w_qi, w_u, w_k, w_v], axis=1).astype(bf16)
    w_kiwi = jnp.concatenate(
        [w_ki, w_wi, jnp.zeros((d, KIWI_WIDTH - IDX_DIM - IDX_HEADS), w_in.dtype)], axis=1).astype(bf16)

    c8 = jnp.zeros((8, d), f32).at[:batch].set(c)
    mod = _ada(c8, w_ada, b_ada.reshape(1, -1))[:batch]
    mod8 = jnp.zeros((batch, 8, d), f32).at[:, :N_MOD].set(mod.reshape(batch, N_MOD, d))

    proj, kiwi = _inproj(x2, mod8, g_mix.reshape(1, d), w_main, w_kiwi, seq)
    attn = _attention(proj, kiwi, q_norm.reshape(1, -1), k_norm.reshape(1, -1), batch, seq)
    pool = _pool(proj, pool_lin.astype(bf16), pool_scale.reshape(1, -1), seq)
    merged = _merge(attn, pool, w_attn_up.astype(bf16), w_pool_up.astype(bf16), proj)
    x1, h2p, logits = _outproj(merged, x2, mod8, g_ffn.reshape(1, d), w_out.astype(bf16), w_router, seq)

    logits_t = logits.T
    bias_col = router_bias.reshape(N_EXPERTS, 1)
    counts = _route_counts(logits_t, bias_col)[:, 0].astype(i32)
    t = EXPERT_TILE
    tiles_e = (counts + t - 1) // t
    tile_end = jnp.cumsum(tiles_e)
    pstart = ((tile_end - tiles_e) * t).astype(f32).reshape(N_EXPERTS, 1)
    n_tiles = n * EXPERT_TOPK // t + N_EXPERTS
    n_used = tile_end[-1]
    tile_ids = jnp.minimum(jnp.arange(n_tiles, dtype=i32), n_used - 1)
    block_e = jnp.minimum(jnp.searchsorted(tile_end, tile_ids, side='right'), N_EXPERTS - 1).astype(i32)
    gate8, pos8 = _route_assign(logits_t, bias_col, pstart)
    pos_flat = pos8.reshape(-1)

    xs = _dispatch(pos_flat, h2p, jnp.zeros((n_tiles * t, d // 2), i32))
    ys = _experts(block_e, n_used.reshape(1).astype(i32), xs, w1, w3, w2)
    out = _combine(pos_flat, gate8.T, h2p, x1, mod8, ws1.astype(bf16), ws3.astype(bf16), ws2.astype(bf16), ys, seq)
    return out.reshape(batch, seq, d)


def kernel(x, c, w_ada, b_ada, g_mix, w_in, q_norm, k_norm, w_attn_up, pool_lin, pool_scale, w_pool_up, w_out, g_ffn, w_router, router_bias, w1, w3, w2, ws1, ws3, ws2):
    for l in range(w_ada.shape[0]):
        x = _layer(x, c, w_ada[l], b_ada[l], g_mix[l], w_in[l], q_norm[l], k_norm[l], w_attn_up[l], pool_lin[l],
                   pool_scale[l], w_pool_up[l], w_out[l], g_ffn[l], w_router[l], router_bias[l], w1[l], w3[l],
                   w2[l], ws1[l], ws3[l], ws2[l])
    return x
```

```python
import functools

import jax
import jax.numpy as jnp
from jax import lax
from jax.experimental import pallas as pl
from jax.experimental.pallas import tpu as pltpu

f32 = jnp.float32
bf16 = jnp.bfloat16
i32 = jnp.int32

N_HEADS = 8
HEAD_DIM = 128
N_KV_HEADS = 2
Q_PER_KV = N_HEADS // N_KV_HEADS
ATTN_WIDTH = N_HEADS * HEAD_DIM
KV_WIDTH = N_KV_HEADS * HEAD_DIM
IDX_HEADS = 16
IDX_DIM = 64
IDX_TOPK_MAX = 256
Q_BLOCK = 128
POOL_WINDOWS = (2, 4, 8, 16)
POOL_GROUP_DIM = 256
POOL_WIDTH = 1024
POOL_HALO = 16
N_EXPERTS = 64
N_EXPERT_GROUPS = 8
EXPERTS_PER_GROUP = 8
TOPK_GROUPS = 4
EXPERT_TOPK = 8
EXPERT_HIDDEN = 512
ROUTED_SCALE = 2.5
NORM_EPS = 1e-6
N_MOD = 6

COL_GATE_A = 0
COL_GATE_P = 2048
COL_Q = 4096
COL_QI = 5120
COL_U = 6144
COL_K = 7168
COL_V = 7424
MAIN_WIDTH = 7680
KIWI_WIDTH = 128

KEY_CHUNK = 512
EXPERT_TILE = 256
VMEM_LIMIT = 56 * 1024 * 1024
INT_MIN = -2147483648
KEY_OF_LOWEST_FINITE = -2139095040
NEG_BIG = -1e30
LOG2E = 1.4426950408889634


def _alibi_slope(h):
    return 2.0 ** (-8.0 * (h + 1) / N_HEADS)


def _cparams(n_axes, vmem=VMEM_LIMIT):
    return pltpu.CompilerParams(dimension_semantics=("arbitrary",) * n_axes, vmem_limit_bytes=vmem)


def _sigmoid(x):
    return 1.0 / (1.0 + jnp.exp(-x))


def _silu(x):
    return x * _sigmoid(x)


def _ada_kernel(c_ref, w_ref, b_ref, o_ref):
    sc = _silu(c_ref[...]).astype(bf16)
    o_ref[...] = jnp.dot(sc, w_ref[...].astype(bf16), preferred_element_type=f32) + b_ref[...]


def _ada(c8, w_ada, b_ada):
    d, n = w_ada.shape
    tn = 1024
    return pl.pallas_call(
        _ada_kernel,
        grid=(n // tn,),
        in_specs=[pl.BlockSpec((8, d), lambda j: (0, 0)),
                  pl.BlockSpec((d, tn), lambda j: (0, j)),
                  pl.BlockSpec((1, tn), lambda j: (0, j))],
        out_specs=pl.BlockSpec((8, tn), lambda j: (0, j)),
        out_shape=jax.ShapeDtypeStruct((8, n), f32),
        compiler_params=_cparams(1),
        name="adaln",
    )(c8, w_ada, b_ada)


def _inproj_kernel(x_ref, mod_ref, g_ref, w_ref, wk_ref, o_ref, kiwi_ref, h_scr):
    @pl.when(pl.program_id(1) == 0)
    def _():
        x = x_ref[...]
        y = x * lax.rsqrt(jnp.mean(x * x, axis=-1, keepdims=True) + NORM_EPS) * g_ref[...]
        h = y * (1.0 + mod_ref[0, 1:2, :]) + mod_ref[0, 0:1, :]
        hb = h.astype(bf16)
        h_scr[...] = hb
        kiwi_ref[...] = jnp.dot(hb, wk_ref[...], preferred_element_type=f32)

    o_ref[...] = jnp.dot(h_scr[...], w_ref[...], preferred_element_type=f32).astype(o_ref.dtype)


def _inproj(x2, mod8, g_mix, w_main, w_kiwi, seq):
    n, d = x2.shape
    tm, tn = 1024, 512
    tm = min(tm, seq)
    per_seq = seq // tm
    return pl.pallas_call(
        _inproj_kernel,
        grid=(n // tm, MAIN_WIDTH // tn),
        in_specs=[pl.BlockSpec((tm, d), lambda i, j: (i, 0)),
                  pl.BlockSpec((1, 8, d), lambda i, j: (i // per_seq, 0, 0)),
                  pl.BlockSpec((1, d), lambda i, j: (0, 0)),
                  pl.BlockSpec((d, tn), lambda i, j: (0, j)),
                  pl.BlockSpec((d, KIWI_WIDTH), lambda i, j: (0, 0))],
        out_specs=[pl.BlockSpec((tm, tn), lambda i, j: (i, j)),
                   pl.BlockSpec((tm, KIWI_WIDTH), lambda i, j: (i, 0))],
        out_shape=[jax.ShapeDtypeStruct((n, MAIN_WIDTH), bf16),
                   jax.ShapeDtypeStruct((n, KIWI_WIDTH), f32)],
        scratch_shapes=[pltpu.VMEM((tm, d), bf16)],
        compiler_params=_cparams(2),
        name="inproj",
    )(x2, mod8, g_mix, w_main, w_kiwi)


def _tree_sum(parts):
    while len(parts) > 1:
        parts = [parts[a] + parts[a + 1] for a in range(0, len(parts) - 1, 2)] + ([parts[-1]] if len(parts) % 2 else [])
    return parts[0]


def _attn_t_kernel(q_ref, qi_ref, k_ref, v_ref, kiwi_all_ref, kiwi_blk_ref, qn_ref, kn_ref, o_ref,
                   kn_scr, ki_scr, vt_scr, score_scr, qt_scr, qit_scr, bias_scr, m_scr, l_scr, acc_scr, *, seq, topk):
    i = pl.program_id(1)
    ck = min(KEY_CHUNK, seq)
    n_chunks = i // (ck // Q_BLOCK) + 1
    heads_per_dot = 4

    @pl.when(i == 0)
    def _prep_keys():
        for n in range(N_KV_HEADS):
            kf = k_ref[:, n * HEAD_DIM:(n + 1) * HEAD_DIM].astype(f32)
            r = lax.rsqrt(jnp.mean(kf * kf, axis=-1, keepdims=True) + NORM_EPS)
            kn_scr[:, n * HEAD_DIM:(n + 1) * HEAD_DIM] = (kf * r * kn_ref[...]).astype(bf16)
        ki_scr[...] = kiwi_all_ref[:, 0:IDX_DIM].astype(bf16)
        for c in range(seq // ck):
            vt_scr[c] = v_ref[c * ck:(c + 1) * ck, :].astype(f32).T.astype(bf16)
        rel = (lax.broadcasted_iota(i32, (ck, Q_BLOCK), 1) - lax.broadcasted_iota(i32, (ck, Q_BLOCK), 0)).astype(f32)
        for h in range(N_HEADS):
            bias_scr[h] = rel * (-_alibi_slope(h) * LOG2E)

    q_t = []
    for h in range(N_HEADS):
        qf = q_ref[:, h * HEAD_DIM:(h + 1) * HEAD_DIM].astype(f32)
        r = lax.rsqrt(jnp.mean(qf * qf, axis=-1, keepdims=True) + NORM_EPS)
        q_t.append((qf * r * qn_ref[...] * (HEAD_DIM ** -0.5 * LOG2E)).T)
    for n in range(N_KV_HEADS):
        qt_scr[n] = jnp.concatenate(q_t[n * Q_PER_KV:(n + 1) * Q_PER_KV], axis=1).astype(bf16)
    qi_t = qi_ref[...].astype(f32).T
    for a in range(IDX_HEADS // heads_per_dot):
        qit_scr[a] = jnp.concatenate(
            [qi_t[(a * heads_per_dot + b) * IDX_DIM:(a * heads_per_dot + b + 1) * IDX_DIM, :]
             for b in range(heads_per_dot)], axis=1).astype(bf16)
    wi_t = kiwi_blk_ref[...].T[IDX_DIM:IDX_DIM + IDX_HEADS, :] * (IDX_HEADS ** -0.5 * IDX_DIM ** -0.5)
    qpos = i * Q_BLOCK + lax.broadcasted_iota(i32, (1, Q_BLOCK), 1)

    def index_chunk(c, carry):
        start = pl.multiple_of(c * ck, ck)
        kc = ki_scr[pl.ds(start, ck), :]
        acc = jnp.zeros((ck, Q_BLOCK), f32)
        for a in range(IDX_HEADS // heads_per_dot):
            d = jnp.dot(kc, qit_scr[a], preferred_element_type=f32)
            for b in range(heads_per_dot):
                h = a * heads_per_dot + b
                acc = acc + jnp.maximum(d[:, b * Q_BLOCK:(b + 1) * Q_BLOCK], 0.0) * wi_t[h:h + 1, :]
        kpos = start + lax.broadcasted_iota(i32, (ck, 1), 0)
        score_scr[c] = jnp.where(kpos <= qpos, acc, -jnp.inf)
        return carry

    lax.fori_loop(0, n_chunks, index_chunk, 0)

    def key_to_float(key):
        bits = key ^ (lax.shift_right_arithmetic(key, 31) & jnp.int32(0x7FFFFFFF))
        return lax.bitcast_convert_type(bits, f32)

    def bit_step(b, t_u):
        bit = lax.shift_left(jnp.int32(1), 31 - b)
        cand_u = t_u | bit
        cand = key_to_float(cand_u ^ jnp.int32(INT_MIN))

        def count_chunk(c, cnt):
            ge = jnp.where(score_scr[c] >= cand, 1.0, 0.0)
            return cnt + _tree_sum([ge[s * 8:(s + 1) * 8, :] for s in range(ck // 8)])

        cnt = lax.fori_loop(0, n_chunks, count_chunk, jnp.zeros((8, Q_BLOCK), f32))
        total = jnp.sum(cnt, axis=0, keepdims=True)
        return jnp.where(total >= float(topk), cand_u, t_u)

    t_u = lax.fori_loop(0, 32, bit_step, jnp.zeros((1, Q_BLOCK), i32))
    thr = key_to_float(jnp.maximum(t_u ^ jnp.int32(INT_MIN), jnp.int32(KEY_OF_LOWEST_FINITE)))

    m_scr[...] = jnp.full(m_scr.shape, NEG_BIG, f32)
    l_scr[...] = jnp.zeros(l_scr.shape, f32)
    acc_scr[...] = jnp.zeros(acc_scr.shape, f32)
    lane_head = lax.broadcasted_iota(i32, (1, Q_PER_KV * Q_BLOCK), 1) // Q_BLOCK

    def attn_chunk(c, carry):
        start = pl.multiple_of(c * ck, ck)
        sel = score_scr[c] >= thr
        tile_dist = (i * Q_BLOCK - start).astype(f32)
        for n in range(N_KV_HEADS):
            kc = kn_scr[pl.ds(start, ck), n * HEAD_DIM:(n + 1) * HEAD_DIM]
            vt = vt_scr[c, n * HEAD_DIM:(n + 1) * HEAD_DIM, :]
            z_all = jnp.dot(kc, qt_scr[n], preferred_element_type=f32)
            parts = []
            slope_row = jnp.zeros((1, Q_PER_KV * Q_BLOCK), f32)
            for g in range(Q_PER_KV):
                h = n * Q_PER_KV + g
                parts.append(jnp.where(sel, z_all[:, g * Q_BLOCK:(g + 1) * Q_BLOCK] + bias_scr[h], NEG_BIG))
                slope_row = jnp.where(lane_head == g, _alibi_slope(h) * LOG2E, slope_row)
            z = jnp.concatenate(parts, axis=1)
            off = slope_row * (-tile_dist)
            m_old = m_scr[n]
            m_new = jnp.maximum(m_old, jnp.max(z, axis=0, keepdims=True) + off)
            alpha = jnp.exp2(m_old - m_new)
            p = jnp.exp2(z + (off - m_new))
            l_scr[n] = alpha * l_scr[n] + jnp.sum(p, axis=0, keepdims=True)
            acc_scr[n] = alpha * acc_scr[n] + jnp.dot(vt, p.astype(bf16), preferred_element_type=f32)
            m_scr[n] = m_new
        return carry

    lax.fori_loop(0, n_chunks, attn_chunk, 0)
    for n in range(N_KV_HEADS):
        out_t = acc_scr[n] / l_scr[n]
        for g in range(Q_PER_KV):
            h = n * Q_PER_KV + g
            o_ref[:, h * HEAD_DIM:(h + 1) * HEAD_DIM] = out_t[:, g * Q_BLOCK:(g + 1) * Q_BLOCK].T.astype(o_ref.dtype)


def _attention(proj, kiwi, q_norm, k_norm, batch, seq):
    n = batch * seq
    nq = seq // Q_BLOCK
    ck = min(KEY_CHUNK, seq)
    topk = min(IDX_TOPK_MAX, seq // 4)
    kern = functools.partial(_attn_t_kernel, seq=seq, topk=topk)
    return pl.pallas_call(
        kern,
        grid=(batch, nq),
        in_specs=[pl.BlockSpec((Q_BLOCK, ATTN_WIDTH), lambda b, i: (b * nq + i, COL_Q // ATTN_WIDTH)),
                  pl.BlockSpec((Q_BLOCK, IDX_HEADS * IDX_DIM), lambda b, i: (b * nq + i, COL_QI // (IDX_HEADS * IDX_DIM))),
                  pl.BlockSpec((seq, KV_WIDTH), lambda b, i: (b, COL_K // KV_WIDTH)),
                  pl.BlockSpec((seq, KV_WIDTH), lambda b, i: (b, COL_V // KV_WIDTH)),
                  pl.BlockSpec((seq, KIWI_WIDTH), lambda b, i: (b, 0)),
                  pl.BlockSpec((Q_BLOCK, KIWI_WIDTH), lambda b, i: (b * nq + i, 0)),
                  pl.BlockSpec((1, HEAD_DIM), lambda b, i: (0, 0)),
                  pl.BlockSpec((1, HEAD_DIM), lambda b, i: (0, 0))],
        out_specs=pl.BlockSpec((Q_BLOCK, ATTN_WIDTH), lambda b, i: (b * nq + i, 0)),
        out_shape=jax.ShapeDtypeStruct((n, ATTN_WIDTH), bf16),
        scratch_shapes=[pltpu.VMEM((seq, KV_WIDTH), bf16),
                        pltpu.VMEM((seq, IDX_DIM), bf16),
                        pltpu.VMEM((seq // ck, KV_WIDTH, ck), bf16),
                        pltpu.VMEM((seq // ck, ck, Q_BLOCK), f32),
                        pltpu.VMEM((N_KV_HEADS, HEAD_DIM, Q_PER_KV * Q_BLOCK), bf16),
                        pltpu.VMEM((IDX_HEADS // 4, IDX_DIM, 4 * Q_BLOCK), bf16),
                        pltpu.VMEM((N_HEADS, ck, Q_BLOCK), f32),
                        pltpu.VMEM((N_KV_HEADS, 1, Q_PER_KV * Q_BLOCK), f32),
                        pltpu.VMEM((N_KV_HEADS, 1, Q_PER_KV * Q_BLOCK), f32),
                        pltpu.VMEM((N_KV_HEADS, HEAD_DIM, Q_PER_KV * Q_BLOCK), f32)],
        compiler_params=_cparams(2),
        name="sparse_attn",
    )(proj, proj, proj, proj, kiwi, kiwi, q_norm, k_norm)


def _pool_kernel(u_ref, halo_ref, lin_ref, ps_ref, o_ref, scr, *, tm, per_seq):
    i = pl.program_id(0)
    first = (i % per_seq) == 0
    scr[0:POOL_HALO, :] = jnp.where(first, 0.0, halo_ref[...].astype(f32))
    scr[POOL_HALO:POOL_HALO + tm, :] = u_ref[...].astype(f32)
    t_in_seq = (i % per_seq) * tm + lax.broadcasted_iota(i32, (tm, 1), 0)
    for g, w in enumerate(POOL_WINDOWS):
        c0, c1 = g * POOL_GROUP_DIM, (g + 1) * POOL_GROUP_DIM
        cur = scr[POOL_HALO:POOL_HALO + tm, c0:c1]
        s = cur
        for j in range(1, w):
            s = s + scr[POOL_HALO - j:POOL_HALO - j + tm, c0:c1]
        count = jnp.minimum(t_in_seq + 1, w).astype(f32)
        pooled = s / count - cur
        mixed = jnp.dot(pooled.astype(bf16), lin_ref[g], preferred_element_type=f32)
        o_ref[:, c0:c1] = (mixed * ps_ref[:, c0:c1]).astype(o_ref.dtype)


def _pool(proj, pool_lin_b, pool_scale, seq):
    n = proj.shape[0]
    tm = min(512, seq)
    per_seq = seq // tm
    hb = tm // POOL_HALO
    kern = functools.partial(_pool_kernel, tm=tm, per_seq=per_seq)
    return pl.pallas_call(
        kern,
        grid=(n // tm,),
        in_specs=[pl.BlockSpec((tm, POOL_WIDTH), lambda i: (i, COL_U // POOL_WIDTH)),
                  pl.BlockSpec((POOL_HALO, POOL_WIDTH), lambda i: (jnp.maximum(i * hb - 1, 0), COL_U // POOL_WIDTH)),
                  pl.BlockSpec((len(POOL_WINDOWS), POOL_GROUP_DIM, POOL_GROUP_DIM), lambda i: (0, 0, 0)),
                  pl.BlockSpec((1, POOL_WIDTH), lambda i: (0, 0))],
        out_specs=pl.BlockSpec((tm, POOL_WIDTH), lambda i: (i, 0)),
        out_shape=jax.ShapeDtypeStruct((n, POOL_WIDTH), bf16),
        scratch_shapes=[pltpu.VMEM((POOL_HALO + tm, POOL_WIDTH), f32)],
        compiler_params=_cparams(1),
        name="pool",
    )(proj, proj, pool_lin_b, pool_scale)


def _merge_kernel(a_ref, p_ref, wa_ref, wp_ref, ga_ref, gp_ref, o_ref):
    ya = jnp.dot(a_ref[...], wa_ref[...], preferred_element_type=f32)
    yp = jnp.dot(p_ref[...], wp_ref[...], preferred_element_type=f32)
    o = _sigmoid(ga_ref[...].astype(f32)) * ya + _sigmoid(gp_ref[...].astype(f32)) * yp
    o_ref[...] = o.astype(o_ref.dtype)


def _merge(attn, pool, w_au, w_pu, proj):
    n = attn.shape[0]
    d = w_au.shape[1]
    tm, tn = min(512, n), 512
    ga0, gp0 = COL_GATE_A // tn, COL_GATE_P // tn
    return pl.pallas_call(
        _merge_kernel,
        grid=(n // tm, d // tn),
        in_specs=[pl.BlockSpec((tm, ATTN_WIDTH), lambda i, j: (i, 0)),
                  pl.BlockSpec((tm, POOL_WIDTH), lambda i, j: (i, 0)),
                  pl.BlockSpec((ATTN_WIDTH, tn), lambda i, j: (0, j)),
                  pl.BlockSpec((POOL_WIDTH, tn), lambda i, j: (0, j)),
                  pl.BlockSpec((tm, tn), lambda i, j: (i, ga0 + j)),
                  pl.BlockSpec((tm, tn), lambda i, j: (i, gp0 + j))],
        out_specs=pl.BlockSpec((tm, tn), lambda i, j: (i, j)),
        out_shape=jax.ShapeDtypeStruct((n, d), bf16),
        compiler_params=_cparams(2),
        name="merge",
    )(attn, pool, w_au, w_pu, proj, proj)


def _outproj_kernel(m_ref, x_ref, mod_ref, g_ref, wo_ref, wr_ref, ws1_ref, ws3_ref, ws2_ref,
                    base_ref, h2_ref, lg_ref):
    y = jnp.dot(m_ref[...], wo_ref[...], preferred_element_type=f32)
    x1 = x_ref[...] + mod_ref[0, 2:3, :] * y
    hn = x1 * lax.rsqrt(jnp.mean(x1 * x1, axis=-1, keepdims=True) + NORM_EPS) * g_ref[...]
    h2 = hn * (1.0 + mod_ref[0, 4:5, :]) + mod_ref[0, 3:4, :]
    h2_ref[...] = h2
    h_hi = h2.astype(bf16)
    act = (_silu(jnp.dot(h_hi, ws1_ref[...], preferred_element_type=f32))
           * jnp.dot(h_hi, ws3_ref[...], preferred_element_type=f32)).astype(bf16)
    shared = jnp.dot(act, ws2_ref[...], preferred_element_type=f32)
    base_ref[...] = x1 + mod_ref[0, 5:6, :] * shared
    h_lo = (h2 - h_hi.astype(f32)).astype(bf16)
    a = jnp.dot(h_hi, wr_ref[...], preferred_element_type=f32)
    b = jnp.dot(h_lo, wr_ref[:, 0:N_EXPERTS], preferred_element_type=f32)
    lg_ref[...] = a[:, 0:N_EXPERTS] + (a[:, N_EXPERTS:2 * N_EXPERTS] + b)


def _outproj(merged, x2, mod8, g_ffn, w_out_b, w_router, ws1b, ws3b, ws2b, seq):
    n, d = x2.shape
    hdim = ws1b.shape[1]
    tm = min(256, seq)
    per_seq = seq // tm
    return pl.pallas_call(
        _outproj_kernel,
        grid=(n // tm,),
        in_specs=[pl.BlockSpec((tm, d), lambda i: (i, 0)),
                  pl.BlockSpec((tm, d), lambda i: (i, 0)),
                  pl.BlockSpec((1, 8, d), lambda i: (i // per_seq, 0, 0)),
                  pl.BlockSpec((1, d), lambda i: (0, 0)),
                  pl.BlockSpec((d, d), lambda i: (0, 0)),
                  pl.BlockSpec((d, 2 * N_EXPERTS), lambda i: (0, 0)),
                  pl.BlockSpec((d, hdim), lambda i: (0, 0)),
                  pl.BlockSpec((d, hdim), lambda i: (0, 0)),
                  pl.BlockSpec((hdim, d), lambda i: (0, 0))],
        out_specs=[pl.BlockSpec((tm, d), lambda i: (i, 0)),
                   pl.BlockSpec((tm, d), lambda i: (i, 0)),
                   pl.BlockSpec((tm, N_EXPERTS), lambda i: (i, 0))],
        out_shape=[jax.ShapeDtypeStruct((n, d), f32),
                   jax.ShapeDtypeStruct((n, d), f32),
                   jax.ShapeDtypeStruct((n, N_EXPERTS), f32)],
        compiler_params=_cparams(1),
        name="outproj",
    )(merged, x2, mod8, g_ffn, w_out_b, w_router, ws1b, ws3b, ws2b)


def _route_select(lg_ref, bias_ref):
    s = _sigmoid(lg_ref[...])
    sel = s + bias_ref[...]
    rows = [sel[EXPERTS_PER_GROUP * g:EXPERTS_PER_GROUP * (g + 1), :] for g in range(N_EXPERT_GROUPS)]
    grp = []
    for r in rows:
        m1 = jnp.max(r, axis=0, keepdims=True)
        eq = r == m1
        n_eq = jnp.sum(jnp.where(eq, 1.0, 0.0), axis=0, keepdims=True)
        m2 = jnp.max(jnp.where(eq, -jnp.inf, r), axis=0, keepdims=True)
        grp.append(m1 + jnp.where(n_eq >= 2.0, m1, m2))
    masked = []
    for g in range(N_EXPERT_GROUPS):
        rank = jnp.zeros_like(grp[g])
        for g2 in range(N_EXPERT_GROUPS):
            if g2 == g:
                continue
            beats = (grp[g2] >= grp[g]) if g2 < g else (grp[g2] > grp[g])
            rank = rank + jnp.where(beats, 1.0, 0.0)
        masked.append(jnp.where(rank < float(TOPK_GROUPS), rows[g], -jnp.inf))
    masked = jnp.concatenate(masked, axis=0)
    eidx = lax.broadcasted_iota(i32, (N_EXPERTS, 1), 0)
    rank = jnp.zeros_like(masked)
    for e2 in range(N_EXPERTS):
        row = masked[e2:e2 + 1, :]
        beats = (row > masked) | ((row == masked) & (eidx > e2))
        rank = rank + jnp.where(beats, 1.0, 0.0)
    return s, rank, rank < float(EXPERT_TOPK)


def _route_count_kernel(lg_ref, bias_ref, cnt_ref):
    @pl.when(pl.program_id(0) == 0)
    def _():
        cnt_ref[...] = jnp.zeros(cnt_ref.shape, f32)

    _, _, selected = _route_select(lg_ref, bias_ref)
    cnt_ref[...] += jnp.sum(jnp.where(selected, 1.0, 0.0), axis=1, keepdims=True)


def _route_assign_kernel(lg_ref, bias_ref, pstart_ref, gate_ref, pos_ref, run_scr):
    @pl.when(pl.program_id(0) == 0)
    def _():
        run_scr[...] = jnp.zeros(run_scr.shape, f32)

    tn = lg_ref.shape[1]
    s, rank, selected = _route_select(lg_ref, bias_ref)
    sel_f = jnp.where(selected, 1.0, 0.0)
    earlier = lax.broadcasted_iota(i32, (tn, tn), 0) < lax.broadcasted_iota(i32, (tn, tn), 1)
    prefix = jnp.dot(sel_f.astype(bf16), jnp.where(earlier, 1.0, 0.0).astype(bf16), preferred_element_type=f32)
    pos = pstart_ref[...] + run_scr[...] + prefix
    run_scr[...] += jnp.sum(sel_f, axis=1, keepdims=True)
    gate = jnp.where(selected, s, 0.0)
    gate = gate / jnp.sum(gate, axis=0, keepdims=True) * ROUTED_SCALE
    for j in range(EXPERT_TOPK):
        slot = rank == float(j)
        gate_ref[j:j + 1, :] = jnp.sum(jnp.where(slot, gate, 0.0), axis=0, keepdims=True)
        pos_ref[j:j + 1, :] = jnp.sum(jnp.where(slot, pos, 0.0), axis=0, keepdims=True).astype(i32)


def _route_counts(logits_t, bias_col):
    e, n = logits_t.shape
    tn = min(512, n)
    return pl.pallas_call(
        _route_count_kernel,
        grid=(n // tn,),
        in_specs=[pl.BlockSpec((e, tn), lambda t: (0, t)),
                  pl.BlockSpec((e, 1), lambda t: (0, 0))],
        out_specs=pl.BlockSpec((e, 1), lambda t: (0, 0)),
        out_shape=jax.ShapeDtypeStruct((e, 1), f32),
        compiler_params=_cparams(1),
        name="route_count",
    )(logits_t, bias_col)


def _route_assign(logits_t, bias_col, pstart_col):
    e, n = logits_t.shape
    tn = min(512, n)
    return pl.pallas_call(
        _route_assign_kernel,
        grid=(n // tn,),
        in_specs=[pl.BlockSpec((e, tn), lambda t: (0, t)),
                  pl.BlockSpec((e, 1), lambda t: (0, 0)),
                  pl.BlockSpec((e, 1), lambda t: (0, 0))],
        out_specs=[pl.BlockSpec((EXPERT_TOPK, tn), lambda t: (0, t)),
                   pl.BlockSpec((EXPERT_TOPK, tn), lambda t: (0, t))],
        out_shape=[jax.ShapeDtypeStruct((EXPERT_TOPK, n), f32),
                   jax.ShapeDtypeStruct((EXPERT_TOPK, n), i32)],
        scratch_shapes=[pltpu.VMEM((e, 1), f32)],
        compiler_params=_cparams(1),
        name="route_assign",
    )(logits_t, bias_col, pstart_col)


def _dispatch_kernel(pos_ref, pad_tile_ref, h_ref, xs_ref, zero_scr, sem, fill_sem, *, n_tokens, tn):
    base = pl.program_id(0) * tn

    @pl.when(pl.program_id(0) == 0)
    def _zero_padded_tiles():
        zero_scr[...] = jnp.zeros(zero_scr.shape, zero_scr.dtype)

        def fill(e):
            row = pl.multiple_of(pad_tile_ref[e], EXPERT_TILE)
            return pltpu.make_async_copy(zero_scr, xs_ref.at[pl.ds(row, EXPERT_TILE), :], fill_sem)

        def start(e, carry):
            @pl.when(pad_tile_ref[e] >= 0)
            def _():
                fill(e).start()
            return carry

        def wait(e, carry):
            @pl.when(pad_tile_ref[e] >= 0)
            def _():
                fill(e).wait()
            return carry

        def tail(tile):
            row = pl.multiple_of(tile * EXPERT_TILE, EXPERT_TILE)
            return pltpu.make_async_copy(zero_scr, xs_ref.at[pl.ds(row, EXPERT_TILE), :], fill_sem)

        def start_tail(tile, carry):
            tail(tile).start()
            return carry

        def wait_tail(tile, carry):
            tail(tile).wait()
            return carry

        n_tiles_total = xs_ref.shape[0] // EXPERT_TILE
        lax.fori_loop(0, N_EXPERTS, start, 0)
        lax.fori_loop(pad_tile_ref[N_EXPERTS], n_tiles_total, start_tail, 0)
        lax.fori_loop(0, N_EXPERTS, wait, 0)
        lax.fori_loop(pad_tile_ref[N_EXPERTS], n_tiles_total, wait_tail, 0)

    def row_copy(t, p):
        return pltpu.make_async_copy(h_ref.at[pl.ds(t, 1), :], xs_ref.at[pl.ds(p, 1), :], sem)

    def issue(t, carry):
        for j in range(EXPERT_TOPK):
            row_copy(t, pos_ref[j * n_tokens + base + t]).start()
        return carry

    lax.fori_loop(0, tn, issue, 0)

    for j in range(EXPERT_TOPK):
        pltpu.make_async_copy(h_ref, xs_ref.at[pl.ds(0, tn), :], sem).wait()


def _dispatch(pos_flat, pad_tile, h2, n_rows):
    n, w = h2.shape
    tn = min(256, n)
    kern = functools.partial(_dispatch_kernel, n_tokens=n, tn=tn)
    return pl.pallas_call(
        kern,
        grid=(n // tn,),
        in_specs=[pl.BlockSpec(memory_space=pltpu.SMEM),
                  pl.BlockSpec(memory_space=pltpu.SMEM),
                  pl.BlockSpec((tn, w), lambda i: (i, 0))],
        out_specs=pl.BlockSpec(memory_space=pl.ANY),
        out_shape=jax.ShapeDtypeStruct((n_rows, w), h2.dtype),
        scratch_shapes=[pltpu.VMEM((EXPERT_TILE, w), h2.dtype), pltpu.SemaphoreType.DMA(()),
                        pltpu.SemaphoreType.DMA(())],
        compiler_params=_cparams(1),
        name="dispatch",
    )(pos_flat, pad_tile, h2)


def _expert_kernel(ord_ref, ue_ref, nu_ref, xs_ref, w1_hbm, w3_hbm, w2_hbm, ys_ref,
                   w13f, w2f, w1b, w3b, w2b, sems):
    i = pl.program_id(0)
    k = ord_ref[i]

    def fetch(kk, slot):
        e = ue_ref[kk]
        return (pltpu.make_async_copy(w1_hbm.at[e], w13f.at[slot, 0], sems.at[slot, 0]),
                pltpu.make_async_copy(w3_hbm.at[e], w13f.at[slot, 1], sems.at[slot, 1]),
                pltpu.make_async_copy(w2_hbm.at[e], w2f.at[slot], sems.at[slot, 2]))

    @pl.when(i == 0)
    def _first_fetch():
        for cp in fetch(0, 0):
            cp.start()

    @pl.when((i < nu_ref[0]) & ((i == 0) | (k != ord_ref[jnp.maximum(i - 1, 0)])))
    def _next_expert():
        slot = k % 2
        for cp in fetch(k, slot):
            cp.wait()
        w1b[...] = w13f[slot, 0].astype(bf16)
        w3b[...] = w13f[slot, 1].astype(bf16)
        w2b[...] = w2f[slot].astype(bf16)

        @pl.when(k + 1 < nu_ref[1])
        def _prefetch():
            for cp in fetch(k + 1, 1 - slot):
                cp.start()

    @pl.when(i < nu_ref[0])
    def _compute():
        xb = xs_ref[...].astype(bf16)
        h1 = jnp.dot(xb, w1b[...], preferred_element_type=f32)
        h3 = jnp.dot(xb, w3b[...], preferred_element_type=f32)
        a = (_silu(h1) * h3).astype(bf16)
        ys_ref[...] = jnp.dot(a, w2b[...], preferred_element_type=f32)

    @pl.when(i >= nu_ref[0])
    def _unused():
        ys_ref[...] = jnp.zeros(ys_ref.shape, ys_ref.dtype)


def _experts(tile_ord, used_experts, n_used, xs, w1, w3, w2):
    p, w = xs.shape
    _, d, hdim = w1.shape
    t = EXPERT_TILE
    grid_spec = pltpu.PrefetchScalarGridSpec(
        num_scalar_prefetch=3,
        grid=(p // t,),
        in_specs=[pl.BlockSpec((t, w), lambda i, o, ue, nu: (jnp.minimum(i, nu[0] - 1), 0)),
                  pl.BlockSpec(memory_space=pl.ANY),
                  pl.BlockSpec(memory_space=pl.ANY),
                  pl.BlockSpec(memory_space=pl.ANY)],
        out_specs=pl.BlockSpec((t, w), lambda i, o, ue, nu: (i, 0)),
        scratch_shapes=[pltpu.VMEM((2, 2, d, hdim), f32), pltpu.VMEM((2, hdim, d), f32),
                        pltpu.VMEM((d, hdim), bf16), pltpu.VMEM((d, hdim), bf16), pltpu.VMEM((hdim, d), bf16),
                        pltpu.SemaphoreType.DMA((2, 3))],
    )
    return pl.pallas_call(
        _expert_kernel,
        grid_spec=grid_spec,
        out_shape=jax.ShapeDtypeStruct((p, w), f32),
        compiler_params=_cparams(1),
        name="experts",
    )(tile_ord, used_experts, n_used, xs, w1, w3, w2)


def _combine_kernel(pos_ref, gate_ref, base_ref, mod_ref, ys_ref, o_ref, gbuf, sems, *, n_tokens, tn):
    step = pl.program_id(0)
    n_steps = pl.num_programs(0)
    slot = step % 2

    def gather_tile(s):
        base = s * tn
        sl = s % 2

        def issue(t, carry):
            for j in range(EXPERT_TOPK):
                p = pos_ref[j * n_tokens + base + t]
                pltpu.make_async_copy(ys_ref.at[pl.ds(p, 1), :], gbuf.at[sl, j, pl.ds(t, 1), :], sems.at[sl]).start()
            return carry

        lax.fori_loop(0, tn, issue, 0)

    @pl.when(step == 0)
    def _():
        gather_tile(step)

    @pl.when(step + 1 < n_steps)
    def _():
        gather_tile(step + 1)

    for j in range(EXPERT_TOPK):
        pltpu.make_async_copy(ys_ref.at[pl.ds(0, tn), :], gbuf.at[slot, j], sems.at[slot]).wait()

    routed = gate_ref[:, 0:1] * gbuf[slot, 0]
    for j in range(1, EXPERT_TOPK):
        routed = routed + gate_ref[:, j:j + 1] * gbuf[slot, j]
    o_ref[...] = base_ref[...] + mod_ref[0, 5:6, :] * routed


def _combine(pos_flat, gate_t, base, mod8, ys, seq):
    n, d = base.shape
    tn = min(128, seq)
    per_seq = seq // tn
    kern = functools.partial(_combine_kernel, n_tokens=n, tn=tn)
    return pl.pallas_call(
        kern,
        grid=(n // tn,),
        in_specs=[pl.BlockSpec(memory_space=pltpu.SMEM),
                  pl.BlockSpec((tn, EXPERT_TOPK), lambda i: (i, 0)),
                  pl.BlockSpec((tn, d), lambda i: (i, 0)),
                  pl.BlockSpec((1, 8, d), lambda i: (i // per_seq, 0, 0)),
                  pl.BlockSpec(memory_space=pl.ANY)],
        out_specs=pl.BlockSpec((tn, d), lambda i: (i, 0)),
        out_shape=jax.ShapeDtypeStruct((n, d), f32),
        scratch_shapes=[pltpu.VMEM((2, EXPERT_TOPK, tn, d), f32), pltpu.SemaphoreType.DMA((2,))],
        compiler_params=_cparams(1),
        name="combine",
    )(pos_flat, gate_t, base, mod8, ys)


def _layer(x, c, w_ada, b_ada, g_mix, w_in, q_norm, k_norm, w_attn_up, pool_lin, pool_scale, w_pool_up,
           w_out, g_ffn, w_router, router_bias, w1, w3, w2, ws1, ws3, ws2):
    batch, seq, d = x.shape
    n = batch * seq
    x2 = x.reshape(n, d)

    offs = [0, ATTN_WIDTH, ATTN_WIDTH + KV_WIDTH, ATTN_WIDTH + 2 * KV_WIDTH]
    w_q = w_in[:, offs[0]:offs[1]]
    w_k = w_in[:, offs[1]:offs[2]]
    w_v = w_in[:, offs[2]:offs[3]]
    o_qi = offs[3]
    w_qi = w_in[:, o_qi:o_qi + IDX_HEADS * IDX_DIM]
    o_ki = o_qi + IDX_HEADS * IDX_DIM
    w_ki = w_in[:, o_ki:o_ki + IDX_DIM]
    o_wi = o_ki + IDX_DIM
    w_wi = w_in[:, o_wi:o_wi + IDX_HEADS]
    o_u = o_wi + IDX_HEADS
    w_u = w_in[:, o_u:o_u + POOL_WIDTH]
    o_g = o_u + POOL_WIDTH
    w_g = w_in[:, o_g:o_g + 2 * d]
    w_main = jnp.concatenate([w_g, w_q, w_qi, w_u, w_k, w_v], axis=1).astype(bf16)
    w_kiwi = jnp.concatenate(
        [w_ki, w_wi, jnp.zeros((d, KIWI_WIDTH - IDX_DIM - IDX_HEADS), w_in.dtype)], axis=1).astype(bf16)

    c8 = jnp.zeros((8, d), f32).at[:batch].set(c)
    mod = _ada(c8, w_ada, b_ada.reshape(1, -1))[:batch]
    mod8 = jnp.zeros((batch, 8, d), f32).at[:, :N_MOD].set(mod.reshape(batch, N_MOD, d))

    proj, kiwi = _inproj(x2, mod8, g_mix.reshape(1, d), w_main, w_kiwi, seq)
    attn = _attention(proj, kiwi, q_norm.reshape(1, -1), k_norm.reshape(1, -1), batch, seq)
    pool = _pool(proj, pool_lin.astype(bf16), pool_scale.reshape(1, -1), seq)
    merged = _merge(attn, pool, w_attn_up.astype(bf16), w_pool_up.astype(bf16), proj)
    wr_hi = w_router.astype(bf16)
    wr_lo = (w_router - wr_hi.astype(f32)).astype(bf16)
    base, h2, logits = _outproj(merged, x2, mod8, g_ffn.reshape(1, d), w_out.astype(bf16),
                                jnp.concatenate([wr_hi, wr_lo], axis=1),
                                ws1.astype(bf16), ws3.astype(bf16), ws2.astype(bf16), seq)

    logits_t = logits.T
    bias_col = router_bias.reshape(N_EXPERTS, 1)
    counts = _route_counts(logits_t, bias_col)[:, 0].astype(i32)
    t = EXPERT_TILE
    tiles_e = (counts + t - 1) // t
    tile_end = jnp.cumsum(tiles_e)
    pstart = ((tile_end - tiles_e) * t).astype(f32).reshape(N_EXPERTS, 1)
    n_tiles = n * EXPERT_TOPK // t + N_EXPERTS
    n_used = tile_end[-1]
    tile_ids = jnp.minimum(jnp.arange(n_tiles, dtype=i32), n_used - 1)
    block_e = jnp.sum((tile_end[None, :] <= tile_ids[:, None]).astype(i32), axis=1)
    block_e = jnp.minimum(block_e, N_EXPERTS - 1)
    pad_tile = jnp.where(tiles_e > 0, (tile_end - 1) * t, -1).astype(i32)
    pad_tile = jnp.concatenate([pad_tile, n_used.reshape(1).astype(i32)])
    gate8, pos8 = _route_assign(logits_t, bias_col, pstart)
    pos_flat = pos8.reshape(-1)

    cum_used = jnp.cumsum((tiles_e > 0).astype(i32))
    slots = jnp.arange(N_EXPERTS, dtype=i32)
    used_experts = jnp.minimum(jnp.sum((cum_used[None, :] <= slots[:, None]).astype(i32), axis=1), N_EXPERTS - 1)
    tile_ord = jnp.sum(jnp.where(block_e[:, None] == slots[None, :], cum_used[None, :] - 1, 0), axis=1).astype(i32)
    n_used2 = jnp.stack([n_used, cum_used[-1]]).astype(i32)

    xs = _dispatch(pos_flat, pad_tile, h2, n_tiles * t)
    ys = _experts(tile_ord, used_experts, n_used2, xs, w1, w3, w2)
    out = _combine(pos_flat, gate8.T, base, mod8, ys, seq)
    return out.reshape(batch, seq, d)


def kernel(x, c, w_ada, b_ada, g_mix, w_in, q_norm, k_norm, w_attn_up, pool_lin, pool_scale, w_pool_up, w_out, g_ffn, w_router, router_bias, w1, w3, w2, ws1, ws3, ws2):
    for l in range(w_ada.shape[0]):
        x = _layer(x, c, w_ada[l], b_ada[l], g_mix[l], w_in[l], q_norm[l], k_norm[l], w_attn_up[l], pool_lin[l],
                   pool_scale[l], w_pool_up[l], w_out[l], g_ffn[l], w_router[l], router_bias[l], w1[l], w3[l],
                   w2[l], ws1[l], ws3[l], ws2[l])
    return x
```

```python
import functools

import jax
import jax.numpy as jnp
from jax import lax
from jax.experimental import pallas as pl
from jax.experimental.pallas import tpu as pltpu

f32 = jnp.float32
bf16 = jnp.bfloat16
i32 = jnp.int32

N_HEADS = 8
HEAD_DIM = 128
N_KV_HEADS = 2
Q_PER_KV = N_HEADS // N_KV_HEADS
ATTN_WIDTH = N_HEADS * HEAD_DIM
KV_WIDTH = N_KV_HEADS * HEAD_DIM
IDX_HEADS = 16
IDX_DIM = 64
IDX_TOPK_MAX = 256
Q_BLOCK = 128
LANE = 128
POOL_WINDOWS = (2, 4, 8, 16)
POOL_GROUP_DIM = 256
POOL_WIDTH = 1024
POOL_HALO = 16
N_EXPERTS = 64
N_EXPERT_GROUPS = 8
EXPERTS_PER_GROUP = 8
TOPK_GROUPS = 4
EXPERT_TOPK = 8
EXPERT_HIDDEN = 512
ROUTED_SCALE = 2.5
NORM_EPS = 1e-6
N_MOD = 6

COL_GATE_A = 0
COL_GATE_P = 2048
COL_Q = 4096
COL_QI = 5120
COL_U = 6144
COL_K = 7168
COL_V = 7424
MAIN_WIDTH = 7680
KIWI_WIDTH = 128

KEY_CHUNK = 512
EXPERT_TILE = 256
VMEM_LIMIT = 56 * 1024 * 1024
INT_MIN = -2147483648
KEY_OF_LOWEST_FINITE = -2139095040
NEG_BIG = -1e30
LOG2E = 1.4426950408889634


def _alibi_slope(h):
    return 2.0 ** (-8.0 * (h + 1) / N_HEADS)


def _cparams(n_axes, vmem=VMEM_LIMIT):
    return pltpu.CompilerParams(dimension_semantics=("arbitrary",) * n_axes, vmem_limit_bytes=vmem)


def _sigmoid(x):
    return 1.0 / (1.0 + jnp.exp(-x))


def _silu(x):
    return x * _sigmoid(x)


def _rows_to_slabs(ref, val):
    rows, width = val.shape
    s = width // LANE
    for j in range(s):
        ref[pl.ds(j, rows, stride=s), :] = val[:, j * LANE:(j + 1) * LANE]


def _slabs_to_rows(ref, rows):
    s = ref.shape[0] // rows
    return jnp.concatenate([ref[pl.ds(j, rows, stride=s), :] for j in range(s)], axis=1)


def _ada_kernel(c_ref, w_ref, b_ref, o_ref):
    sc = _silu(c_ref[...]).astype(bf16)
    o_ref[...] = jnp.dot(sc, w_ref[...].astype(bf16), preferred_element_type=f32) + b_ref[...]


def _ada(c8, w_ada, b_ada):
    d, n = w_ada.shape
    tn = 1024
    return pl.pallas_call(
        _ada_kernel,
        grid=(n // tn,),
        in_specs=[pl.BlockSpec((8, d), lambda j: (0, 0)),
                  pl.BlockSpec((d, tn), lambda j: (0, j)),
                  pl.BlockSpec((1, tn), lambda j: (0, j))],
        out_specs=pl.BlockSpec((8, tn), lambda j: (0, j)),
        out_shape=jax.ShapeDtypeStruct((8, n), f32),
        compiler_params=_cparams(1),
        name="adaln",
    )(c8, w_ada, b_ada)


def _inproj_kernel(x_ref, mod_ref, g_ref, w_ref, wk_ref, o_ref, kiwi_ref, h_scr):
    @pl.when(pl.program_id(1) == 0)
    def _():
        x = x_ref[...]
        y = x * lax.rsqrt(jnp.mean(x * x, axis=-1, keepdims=True) + NORM_EPS) * g_ref[...]
        h = y * (1.0 + mod_ref[0, 1:2, :]) + mod_ref[0, 0:1, :]
        hb = h.astype(bf16)
        h_scr[...] = hb
        kiwi_ref[...] = jnp.dot(hb, wk_ref[...], preferred_element_type=f32)

    o_ref[...] = jnp.dot(h_scr[...], w_ref[...], preferred_element_type=f32).astype(o_ref.dtype)


def _inproj(x2, mod8, g_mix, w_main, w_kiwi, seq):
    n, d = x2.shape
    tm, tn = 1024, 1536
    tm = min(tm, seq)
    per_seq = seq // tm
    return pl.pallas_call(
        _inproj_kernel,
        grid=(n // tm, MAIN_WIDTH // tn),
        in_specs=[pl.BlockSpec((tm, d), lambda i, j: (i, 0)),
                  pl.BlockSpec((1, 8, d), lambda i, j: (i // per_seq, 0, 0)),
                  pl.BlockSpec((1, d), lambda i, j: (0, 0)),
                  pl.BlockSpec((d, tn), lambda i, j: (0, j)),
                  pl.BlockSpec((d, KIWI_WIDTH), lambda i, j: (0, 0))],
        out_specs=[pl.BlockSpec((tm, tn), lambda i, j: (i, j)),
                   pl.BlockSpec((tm, KIWI_WIDTH), lambda i, j: (i, 0))],
        out_shape=[jax.ShapeDtypeStruct((n, MAIN_WIDTH), bf16),
                   jax.ShapeDtypeStruct((n, KIWI_WIDTH), f32)],
        scratch_shapes=[pltpu.VMEM((tm, d), bf16)],
        compiler_params=_cparams(2),
        name="inproj",
    )(x2, mod8, g_mix, w_main, w_kiwi)


def _tree_sum(parts):
    while len(parts) > 1:
        parts = [parts[a] + parts[a + 1] for a in range(0, len(parts) - 1, 2)] + ([parts[-1]] if len(parts) % 2 else [])
    return parts[0]


def _attn_t_kernel(q_ref, qi_ref, k_ref, v_ref, kiwi_all_ref, kiwi_blk_ref, qn_ref, kn_ref, o_ref,
                   kn_scr, ki_scr, vt_scr, score_scr, qt_scr, qit_scr, bias_scr, m_scr, l_scr, acc_scr, *, seq, topk):
    i = pl.program_id(1)
    ck = min(KEY_CHUNK, seq)
    n_chunks = i // (ck // Q_BLOCK) + 1
    heads_per_dot = 4

    @pl.when(i == 0)
    def _prep_keys():
        for n in range(N_KV_HEADS):
            kf = k_ref[:, n * HEAD_DIM:(n + 1) * HEAD_DIM].astype(f32)
            r = lax.rsqrt(jnp.mean(kf * kf, axis=-1, keepdims=True) + NORM_EPS)
            kn_scr[:, n * HEAD_DIM:(n + 1) * HEAD_DIM] = (kf * r * kn_ref[...]).astype(bf16)
        ki_scr[...] = kiwi_all_ref[:, 0:IDX_DIM].astype(bf16)
        for c in range(seq // ck):
            vt_scr[c] = v_ref[c * ck:(c + 1) * ck, :].astype(f32).T.astype(bf16)
        rel = (lax.broadcasted_iota(i32, (ck, Q_BLOCK), 1) - lax.broadcasted_iota(i32, (ck, Q_BLOCK), 0)).astype(f32)
        for h in range(N_HEADS):
            bias_scr[h] = rel * (-_alibi_slope(h) * LOG2E)

    q_t = []
    for h in range(N_HEADS):
        qf = q_ref[:, h * HEAD_DIM:(h + 1) * HEAD_DIM].astype(f32)
        r = lax.rsqrt(jnp.mean(qf * qf, axis=-1, keepdims=True) + NORM_EPS)
        q_t.append((qf * r * qn_ref[...] * (HEAD_DIM ** -0.5 * LOG2E)).T)
    for n in range(N_KV_HEADS):
        qt_scr[n] = jnp.concatenate(q_t[n * Q_PER_KV:(n + 1) * Q_PER_KV], axis=1).astype(bf16)
    qi_t = qi_ref[...].astype(f32).T
    for a in range(IDX_HEADS // heads_per_dot):
        qit_scr[a] = jnp.concatenate(
            [qi_t[(a * heads_per_dot + b) * IDX_DIM:(a * heads_per_dot + b + 1) * IDX_DIM, :]
             for b in range(heads_per_dot)], axis=1).astype(bf16)
    wi_t = kiwi_blk_ref[...].T[IDX_DIM:IDX_DIM + IDX_HEADS, :] * (IDX_HEADS ** -0.5 * IDX_DIM ** -0.5)
    qpos = i * Q_BLOCK + lax.broadcasted_iota(i32, (1, Q_BLOCK), 1)

    def index_chunk(c, carry):
        start = pl.multiple_of(c * ck, ck)
        kc = ki_scr[pl.ds(start, ck), :]
        acc = jnp.zeros((ck, Q_BLOCK), f32)
        for a in range(IDX_HEADS // heads_per_dot):
            d = jnp.dot(kc, qit_scr[a], preferred_element_type=f32)
            for b in range(heads_per_dot):
                h = a * heads_per_dot + b
                acc = acc + jnp.maximum(d[:, b * Q_BLOCK:(b + 1) * Q_BLOCK], 0.0) * wi_t[h:h + 1, :]
        kpos = start + lax.broadcasted_iota(i32, (ck, 1), 0)
        score_scr[c] = jnp.where(kpos <= qpos, acc, -jnp.inf)
        return carry

    lax.fori_loop(0, n_chunks, index_chunk, 0)

    def key_to_float(key):
        bits = key ^ (lax.shift_right_arithmetic(key, 31) & jnp.int32(0x7FFFFFFF))
        return lax.bitcast_convert_type(bits, f32)

    def bit_step(b, t_u):
        bit = lax.shift_left(jnp.int32(1), 31 - b)
        cand_u = t_u | bit
        cand = key_to_float(cand_u ^ jnp.int32(INT_MIN))

        def count_chunk(c, cnt):
            ge = jnp.where(score_scr[c] >= cand, 1.0, 0.0)
            return cnt + _tree_sum([ge[s * 8:(s + 1) * 8, :] for s in range(ck // 8)])

        cnt = lax.fori_loop(0, n_chunks, count_chunk, jnp.zeros((8, Q_BLOCK), f32))
        total = jnp.sum(cnt, axis=0, keepdims=True)
        return jnp.where(total >= float(topk), cand_u, t_u)

    t_u = lax.fori_loop(0, 32, bit_step, jnp.zeros((1, Q_BLOCK), i32))
    thr = key_to_float(jnp.maximum(t_u ^ jnp.int32(INT_MIN), jnp.int32(KEY_OF_LOWEST_FINITE)))

    def count_ties(c, carry):
        ge, gt = carry
        sc = score_scr[c]
        ge = ge + _tree_sum([jnp.where(sc[s * 8:(s + 1) * 8, :] >= thr, 1.0, 0.0) for s in range(ck // 8)])
        gt = gt + _tree_sum([jnp.where(sc[s * 8:(s + 1) * 8, :] > thr, 1.0, 0.0) for s in range(ck // 8)])
        return ge, gt

    zeros8 = jnp.zeros((8, Q_BLOCK), f32)
    ge8, gt8 = lax.fori_loop(0, n_chunks, count_ties, (zeros8, zeros8))
    excess = jnp.sum(ge8, axis=0, keepdims=True) > float(topk)
    need = float(topk) - jnp.sum(gt8, axis=0, keepdims=True)

    @pl.when(jnp.max(jnp.where(excess, 1.0, 0.0)) > 0.0)
    def _break_ties():
        pos_bits = (seq - 1).bit_length()

        def key_positions(c):
            return c * ck + lax.broadcasted_iota(i32, (ck, Q_BLOCK), 0)

        def pos_step(b, q):
            cand = q | lax.shift_left(jnp.int32(1), pos_bits - 1 - b)

            def count_before(c, cnt):
                hit = jnp.where((score_scr[c] == thr) & (key_positions(c) < cand), 1.0, 0.0)
                return cnt + _tree_sum([hit[s * 8:(s + 1) * 8, :] for s in range(ck // 8)])

            before = jnp.sum(lax.fori_loop(0, n_chunks, count_before, zeros8), axis=0, keepdims=True)
            return jnp.where(before < need, cand, q)

        q = lax.fori_loop(0, pos_bits, pos_step, jnp.zeros((1, Q_BLOCK), i32))
        q = jnp.where(excess, q, jnp.int32(seq))

        def demote(c, carry):
            sc = score_scr[c]
            score_scr[c] = jnp.where((sc == thr) & (key_positions(c) > q), -jnp.inf, sc)
            return carry

        lax.fori_loop(0, n_chunks, demote, 0)

    m_scr[...] = jnp.full(m_scr.shape, NEG_BIG, f32)
    l_scr[...] = jnp.zeros(l_scr.shape, f32)
    acc_scr[...] = jnp.zeros(acc_scr.shape, f32)
    lane_head = lax.broadcasted_iota(i32, (1, Q_PER_KV * Q_BLOCK), 1) // Q_BLOCK

    def attn_chunk(c, carry):
        start = pl.multiple_of(c * ck, ck)
        sel = score_scr[c] >= thr
        tile_dist = (i * Q_BLOCK - start).astype(f32)
        for n in range(N_KV_HEADS):
            kc = kn_scr[pl.ds(start, ck), n * HEAD_DIM:(n + 1) * HEAD_DIM]
            vt = vt_scr[c, n * HEAD_DIM:(n + 1) * HEAD_DIM, :]
            z_all = jnp.dot(kc, qt_scr[n], preferred_element_type=f32)
            parts = []
            slope_row = jnp.zeros((1, Q_PER_KV * Q_BLOCK), f32)
            for g in range(Q_PER_KV):
                h = n * Q_PER_KV + g
                parts.append(jnp.where(sel, z_all[:, g * Q_BLOCK:(g + 1) * Q_BLOCK] + bias_scr[h], NEG_BIG))
                slope_row = jnp.where(lane_head == g, _alibi_slope(h) * LOG2E, slope_row)
            z = jnp.concatenate(parts, axis=1)
            off = slope_row * (-tile_dist)
            m_old = m_scr[n]
            m_new = jnp.maximum(m_old, jnp.max(z, axis=0, keepdims=True) + off)
            alpha = jnp.exp2(m_old - m_new)
            p = jnp.exp2(z + (off - m_new))
            l_scr[n] = alpha * l_scr[n] + jnp.sum(p, axis=0, keepdims=True)
            acc_scr[n] = alpha * acc_scr[n] + jnp.dot(vt, p.astype(bf16), preferred_element_type=f32)
            m_scr[n] = m_new
        return carry

    lax.fori_loop(0, n_chunks, attn_chunk, 0)
    for n in range(N_KV_HEADS):
        out_t = acc_scr[n] / l_scr[n]
        for g in range(Q_PER_KV):
            h = n * Q_PER_KV + g
            o_ref[:, h * HEAD_DIM:(h + 1) * HEAD_DIM] = out_t[:, g * Q_BLOCK:(g + 1) * Q_BLOCK].T.astype(o_ref.dtype)


def _attention(proj, kiwi, q_norm, k_norm, batch, seq):
    n = batch * seq
    nq = seq // Q_BLOCK
    ck = min(KEY_CHUNK, seq)
    topk = min(IDX_TOPK_MAX, seq // 4)
    kern = functools.partial(_attn_t_kernel, seq=seq, topk=topk)
    return pl.pallas_call(
        kern,
        grid=(batch, nq),
        in_specs=[pl.BlockSpec((Q_BLOCK, ATTN_WIDTH), lambda b, i: (b * nq + i, COL_Q // ATTN_WIDTH)),
                  pl.BlockSpec((Q_BLOCK, IDX_HEADS * IDX_DIM), lambda b, i: (b * nq + i, COL_QI // (IDX_HEADS * IDX_DIM))),
                  pl.BlockSpec((seq, KV_WIDTH), lambda b, i: (b, COL_K // KV_WIDTH)),
                  pl.BlockSpec((seq, KV_WIDTH), lambda b, i: (b, COL_V // KV_WIDTH)),
                  pl.BlockSpec((seq, KIWI_WIDTH), lambda b, i: (b, 0)),
                  pl.BlockSpec((Q_BLOCK, KIWI_WIDTH), lambda b, i: (b * nq + i, 0)),
                  pl.BlockSpec((1, HEAD_DIM), lambda b, i: (0, 0)),
                  pl.BlockSpec((1, HEAD_DIM), lambda b, i: (0, 0))],
        out_specs=pl.BlockSpec((Q_BLOCK, ATTN_WIDTH), lambda b, i: (b * nq + i, 0)),
        out_shape=jax.ShapeDtypeStruct((n, ATTN_WIDTH), bf16),
        scratch_shapes=[pltpu.VMEM((seq, KV_WIDTH), bf16),
                        pltpu.VMEM((seq, IDX_DIM), bf16),
                        pltpu.VMEM((seq // ck, KV_WIDTH, ck), bf16),
                        pltpu.VMEM((seq // ck, ck, Q_BLOCK), f32),
                        pltpu.VMEM((N_KV_HEADS, HEAD_DIM, Q_PER_KV * Q_BLOCK), bf16),
                        pltpu.VMEM((IDX_HEADS // 4, IDX_DIM, 4 * Q_BLOCK), bf16),
                        pltpu.VMEM((N_HEADS, ck, Q_BLOCK), f32),
                        pltpu.VMEM((N_KV_HEADS, 1, Q_PER_KV * Q_BLOCK), f32),
                        pltpu.VMEM((N_KV_HEADS, 1, Q_PER_KV * Q_BLOCK), f32),
                        pltpu.VMEM((N_KV_HEADS, HEAD_DIM, Q_PER_KV * Q_BLOCK), f32)],
        compiler_params=_cparams(2),
        name="sparse_attn",
    )(proj, proj, proj, proj, kiwi, kiwi, q_norm, k_norm)


def _pool_kernel(u_ref, halo_ref, lin_ref, ps_ref, o_ref, scr, *, tm, per_seq):
    i = pl.program_id(0)
    first = (i % per_seq) == 0
    scr[0:POOL_HALO, :] = jnp.where(first, 0.0, halo_ref[...].astype(f32))
    scr[POOL_HALO:POOL_HALO + tm, :] = u_ref[...].astype(f32)
    t_in_seq = (i % per_seq) * tm + lax.broadcasted_iota(i32, (tm, 1), 0)
    for g, w in enumerate(POOL_WINDOWS):
        c0, c1 = g * POOL_GROUP_DIM, (g + 1) * POOL_GROUP_DIM
        cur = scr[POOL_HALO:POOL_HALO + tm, c0:c1]
        s = cur
        for j in range(1, w):
            s = s + scr[POOL_HALO - j:POOL_HALO - j + tm, c0:c1]
        count = jnp.minimum(t_in_seq + 1, w).astype(f32)
        pooled = s / count - cur
        mixed = jnp.dot(pooled.astype(bf16), lin_ref[g], preferred_element_type=f32)
        o_ref[:, c0:c1] = (mixed * ps_ref[:, c0:c1]).astype(o_ref.dtype)


def _pool(proj, pool_lin_b, pool_scale, seq):
    n = proj.shape[0]
    tm = min(512, seq)
    per_seq = seq // tm
    hb = tm // POOL_HALO
    kern = functools.partial(_pool_kernel, tm=tm, per_seq=per_seq)
    return pl.pallas_call(
        kern,
        grid=(n // tm,),
        in_specs=[pl.BlockSpec((tm, POOL_WIDTH), lambda i: (i, COL_U // POOL_WIDTH)),
                  pl.BlockSpec((POOL_HALO, POOL_WIDTH), lambda i: (jnp.maximum(i * hb - 1, 0), COL_U // POOL_WIDTH)),
                  pl.BlockSpec((len(POOL_WINDOWS), POOL_GROUP_DIM, POOL_GROUP_DIM), lambda i: (0, 0, 0)),
                  pl.BlockSpec((1, POOL_WIDTH), lambda i: (0, 0))],
        out_specs=pl.BlockSpec((tm, POOL_WIDTH), lambda i: (i, 0)),
        out_shape=jax.ShapeDtypeStruct((n, POOL_WIDTH), bf16),
        scratch_shapes=[pltpu.VMEM((POOL_HALO + tm, POOL_WIDTH), f32)],
        compiler_params=_cparams(1),
        name="pool",
    )(proj, proj, pool_lin_b, pool_scale)


def _merge_kernel(a_ref, p_ref, wa_ref, wp_ref, ga_ref, gp_ref, o_ref):
    ya = jnp.dot(a_ref[...], wa_ref[...], preferred_element_type=f32)
    yp = jnp.dot(p_ref[...], wp_ref[...], preferred_element_type=f32)
    o = _sigmoid(ga_ref[...].astype(f32)) * ya + _sigmoid(gp_ref[...].astype(f32)) * yp
    o_ref[...] = o.astype(o_ref.dtype)


def _merge(attn, pool, w_au, w_pu, proj):
    n = attn.shape[0]
    d = w_au.shape[1]
    tm, tn = min(1024, n), 1024
    ga0, gp0 = COL_GATE_A // tn, COL_GATE_P // tn
    return pl.pallas_call(
        _merge_kernel,
        grid=(n // tm, d // tn),
        in_specs=[pl.BlockSpec((tm, ATTN_WIDTH), lambda i, j: (i, 0)),
                  pl.BlockSpec((tm, POOL_WIDTH), lambda i, j: (i, 0)),
                  pl.BlockSpec((ATTN_WIDTH, tn), lambda i, j: (0, j)),
                  pl.BlockSpec((POOL_WIDTH, tn), lambda i, j: (0, j)),
                  pl.BlockSpec((tm, tn), lambda i, j: (i, ga0 + j)),
                  pl.BlockSpec((tm, tn), lambda i, j: (i, gp0 + j))],
        out_specs=pl.BlockSpec((tm, tn), lambda i, j: (i, j)),
        out_shape=jax.ShapeDtypeStruct((n, d), bf16),
        compiler_params=_cparams(2),
        name="merge",
    )(attn, pool, w_au, w_pu, proj, proj)


def _outproj_kernel(m_ref, x_ref, mod_ref, g_ref, wo_ref, wr_ref, ws1_ref, ws3_ref, ws2_ref,
                    base_ref, h2_ref, lg_ref):
    y = jnp.dot(m_ref[...], wo_ref[...], preferred_element_type=f32)
    x1 = x_ref[...] + mod_ref[0, 2:3, :] * y
    hn = x1 * lax.rsqrt(jnp.mean(x1 * x1, axis=-1, keepdims=True) + NORM_EPS) * g_ref[...]
    h2 = hn * (1.0 + mod_ref[0, 4:5, :]) + mod_ref[0, 3:4, :]
    _rows_to_slabs(h2_ref, h2)
    h_hi = h2.astype(bf16)
    act = (_silu(jnp.dot(h_hi, ws1_ref[...], preferred_element_type=f32))
           * jnp.dot(h_hi, ws3_ref[...], preferred_element_type=f32)).astype(bf16)
    shared = jnp.dot(act, ws2_ref[...], preferred_element_type=f32)
    base_ref[...] = x1 + mod_ref[0, 5:6, :] * shared
    h_lo = (h2 - h_hi.astype(f32)).astype(bf16)
    a = jnp.dot(h_hi, wr_ref[...], preferred_element_type=f32)
    b = jnp.dot(h_lo, wr_ref[:, 0:N_EXPERTS], preferred_element_type=f32)
    lg_ref[...] = a[:, 0:N_EXPERTS] + (a[:, N_EXPERTS:2 * N_EXPERTS] + b)


def _outproj(merged, x2, mod8, g_ffn, w_out_b, w_router, ws1b, ws3b, ws2b, seq):
    n, d = x2.shape
    hdim = ws1b.shape[1]
    tm = min(256, seq)
    per_seq = seq // tm
    return pl.pallas_call(
        _outproj_kernel,
        grid=(n // tm,),
        in_specs=[pl.BlockSpec((tm, d), lambda i: (i, 0)),
                  pl.BlockSpec((tm, d), lambda i: (i, 0)),
                  pl.BlockSpec((1, 8, d), lambda i: (i // per_seq, 0, 0)),
                  pl.BlockSpec((1, d), lambda i: (0, 0)),
                  pl.BlockSpec((d, d), lambda i: (0, 0)),
                  pl.BlockSpec((d, 2 * N_EXPERTS), lambda i: (0, 0)),
                  pl.BlockSpec((d, hdim), lambda i: (0, 0)),
                  pl.BlockSpec((d, hdim), lambda i: (0, 0)),
                  pl.BlockSpec((hdim, d), lambda i: (0, 0))],
        out_specs=[pl.BlockSpec((tm, d), lambda i: (i, 0)),
                   pl.BlockSpec((tm * d // LANE, LANE), lambda i: (i, 0)),
                   pl.BlockSpec((tm, N_EXPERTS), lambda i: (i, 0))],
        out_shape=[jax.ShapeDtypeStruct((n, d), f32),
                   jax.ShapeDtypeStruct((n * d // LANE, LANE), f32),
                   jax.ShapeDtypeStruct((n, N_EXPERTS), f32)],
        compiler_params=_cparams(1),
        name="outproj",
    )(merged, x2, mod8, g_ffn, w_out_b, w_router, ws1b, ws3b, ws2b)


def _route_select(lg_ref, bias_ref):
    s = _sigmoid(lg_ref[...])
    sel = s + bias_ref[...]
    rows = [sel[EXPERTS_PER_GROUP * g:EXPERTS_PER_GROUP * (g + 1), :] for g in range(N_EXPERT_GROUPS)]
    grp = []
    for r in rows:
        m1 = jnp.max(r, axis=0, keepdims=True)
        eq = r == m1
        n_eq = jnp.sum(jnp.where(eq, 1.0, 0.0), axis=0, keepdims=True)
        m2 = jnp.max(jnp.where(eq, -jnp.inf, r), axis=0, keepdims=True)
        grp.append(m1 + jnp.where(n_eq >= 2.0, m1, m2))
    masked = []
    for g in range(N_EXPERT_GROUPS):
        rank = jnp.zeros_like(grp[g])
        for g2 in range(N_EXPERT_GROUPS):
            if g2 == g:
                continue
            beats = (grp[g2] >= grp[g]) if g2 < g else (grp[g2] > grp[g])
            rank = rank + jnp.where(beats, 1.0, 0.0)
        masked.append(jnp.where(rank < float(TOPK_GROUPS), rows[g], -jnp.inf))
    masked = jnp.concatenate(masked, axis=0)
    eidx = lax.broadcasted_iota(i32, (N_EXPERTS, 1), 0)
    rank = jnp.zeros_like(masked)
    for e2 in range(N_EXPERTS):
        row = masked[e2:e2 + 1, :]
        beats = (row > masked) | ((row == masked) & (eidx > e2))
        rank = rank + jnp.where(beats, 1.0, 0.0)
    return s, rank, rank < float(EXPERT_TOPK)


def _route_count_kernel(lg_ref, bias_ref, cnt_ref):
    @pl.when(pl.program_id(0) == 0)
    def _():
        cnt_ref[...] = jnp.zeros(cnt_ref.shape, f32)

    _, _, selected = _route_select(lg_ref, bias_ref)
    cnt_ref[...] += jnp.sum(jnp.where(selected, 1.0, 0.0), axis=1, keepdims=True)


def _route_assign_kernel(lg_ref, bias_ref, pstart_ref, gate_ref, pos_ref, run_scr):
    @pl.when(pl.program_id(0) == 0)
    def _():
        run_scr[...] = jnp.zeros(run_scr.shape, f32)

    tn = lg_ref.shape[1]
    s, rank, selected = _route_select(lg_ref, bias_ref)
    sel_f = jnp.where(selected, 1.0, 0.0)
    earlier = lax.broadcasted_iota(i32, (tn, tn), 0) < lax.broadcasted_iota(i32, (tn, tn), 1)
    prefix = jnp.dot(sel_f.astype(bf16), jnp.where(earlier, 1.0, 0.0).astype(bf16), preferred_element_type=f32)
    pos = pstart_ref[...] + run_scr[...] + prefix
    run_scr[...] += jnp.sum(sel_f, axis=1, keepdims=True)
    gate = jnp.where(selected, s, 0.0)
    gate = gate / jnp.sum(gate, axis=0, keepdims=True) * ROUTED_SCALE
    for j in range(EXPERT_TOPK):
        slot = rank == float(j)
        gate_ref[j:j + 1, :] = jnp.sum(jnp.where(slot, gate, 0.0), axis=0, keepdims=True)
        pos_ref[j:j + 1, :] = jnp.sum(jnp.where(slot, pos, 0.0), axis=0, keepdims=True).astype(i32)


def _route_counts(logits_t, bias_col):
    e, n = logits_t.shape
    tn = min(512, n)
    return pl.pallas_call(
        _route_count_kernel,
        grid=(n // tn,),
        in_specs=[pl.BlockSpec((e, tn), lambda t: (0, t)),
                  pl.BlockSpec((e, 1), lambda t: (0, 0))],
        out_specs=pl.BlockSpec((e, 1), lambda t: (0, 0)),
        out_shape=jax.ShapeDtypeStruct((e, 1), f32),
        compiler_params=_cparams(1),
        name="route_count",
    )(logits_t, bias_col)


def _route_assign(logits_t, bias_col, pstart_col):
    e, n = logits_t.shape
    tn = min(512, n)
    return pl.pallas_call(
        _route_assign_kernel,
        grid=(n // tn,),
        in_specs=[pl.BlockSpec((e, tn), lambda t: (0, t)),
                  pl.BlockSpec((e, 1), lambda t: (0, 0)),
                  pl.BlockSpec((e, 1), lambda t: (0, 0))],
        out_specs=[pl.BlockSpec((EXPERT_TOPK, tn), lambda t: (0, t)),
                   pl.BlockSpec((EXPERT_TOPK, tn), lambda t: (0, t))],
        out_shape=[jax.ShapeDtypeStruct((EXPERT_TOPK, n), f32),
                   jax.ShapeDtypeStruct((EXPERT_TOPK, n), i32)],
        scratch_shapes=[pltpu.VMEM((e, 1), f32)],
        compiler_params=_cparams(1),
        name="route_assign",
    )(logits_t, bias_col, pstart_col)


def _dispatch_kernel(pos_ref, pad_tile_ref, h_ref, xs_ref, zero_scr, sem, fill_sem, *, n_tokens, tn):
    base = pl.program_id(0) * tn

    @pl.when(pl.program_id(0) == 0)
    def _zero_padded_tiles():
        zero_scr[...] = jnp.zeros(zero_scr.shape, zero_scr.dtype)

        def fill(e):
            row = pl.multiple_of(pad_tile_ref[e], EXPERT_TILE)
            return pltpu.make_async_copy(zero_scr, xs_ref.at[pl.ds(row, EXPERT_TILE)], fill_sem)

        def start(e, carry):
            @pl.when(pad_tile_ref[e] >= 0)
            def _():
                fill(e).start()
            return carry

        def wait(e, carry):
            @pl.when(pad_tile_ref[e] >= 0)
            def _():
                fill(e).wait()
            return carry

        def tail(tile):
            row = pl.multiple_of(tile * EXPERT_TILE, EXPERT_TILE)
            return pltpu.make_async_copy(zero_scr, xs_ref.at[pl.ds(row, EXPERT_TILE)], fill_sem)

        def start_tail(tile, carry):
            tail(tile).start()
            return carry

        def wait_tail(tile, carry):
            tail(tile).wait()
            return carry

        n_tiles_total = xs_ref.shape[0] // EXPERT_TILE
        lax.fori_loop(0, N_EXPERTS, start, 0)
        lax.fori_loop(pad_tile_ref[N_EXPERTS], n_tiles_total, start_tail, 0)
        lax.fori_loop(0, N_EXPERTS, wait, 0)
        lax.fori_loop(pad_tile_ref[N_EXPERTS], n_tiles_total, wait_tail, 0)

    def row_copy(t, p):
        return pltpu.make_async_copy(h_ref.at[t], xs_ref.at[p], sem)

    def issue(t, carry):
        for j in range(EXPERT_TOPK):
            row_copy(t, pos_ref[j * n_tokens + base + t]).start()
        return carry

    lax.fori_loop(0, tn, issue, 0)

    for j in range(EXPERT_TOPK):
        pltpu.make_async_copy(h_ref, xs_ref.at[pl.ds(0, tn)], sem).wait()


def _dispatch(pos_flat, pad_tile, h2, n_rows):
    n, s, w = h2.shape
    tn = min(256, n)
    kern = functools.partial(_dispatch_kernel, n_tokens=n, tn=tn)
    return pl.pallas_call(
        kern,
        grid=(n // tn,),
        in_specs=[pl.BlockSpec(memory_space=pltpu.SMEM),
                  pl.BlockSpec(memory_space=pltpu.SMEM),
                  pl.BlockSpec((tn, s, w), lambda i: (i, 0, 0))],
        out_specs=pl.BlockSpec(memory_space=pl.ANY),
        out_shape=jax.ShapeDtypeStruct((n_rows, s, w), h2.dtype),
        scratch_shapes=[pltpu.VMEM((EXPERT_TILE, s, w), h2.dtype), pltpu.SemaphoreType.DMA(()),
                        pltpu.SemaphoreType.DMA(())],
        compiler_params=_cparams(1),
        name="dispatch",
    )(pos_flat, pad_tile, h2)


def _expert_kernel(ord_ref, ue_ref, nu_ref, xs_ref, w1_hbm, w3_hbm, w2_hbm, ys_ref,
                   w13f, w2f, w1b, w3b, w2b, sems):
    i = pl.program_id(0)
    k = ord_ref[i]

    def fetch(kk, slot):
        e = ue_ref[kk]
        return (pltpu.make_async_copy(w1_hbm.at[e], w13f.at[slot, 0], sems.at[slot, 0]),
                pltpu.make_async_copy(w3_hbm.at[e], w13f.at[slot, 1], sems.at[slot, 1]),
                pltpu.make_async_copy(w2_hbm.at[e], w2f.at[slot], sems.at[slot, 2]))

    @pl.when(i == 0)
    def _first_fetch():
        for cp in fetch(0, 0):
            cp.start()

    @pl.when((i < nu_ref[0]) & ((i == 0) | (k != ord_ref[jnp.maximum(i - 1, 0)])))
    def _next_expert():
        slot = k % 2
        for cp in fetch(k, slot):
            cp.wait()
        w1b[...] = w13f[slot, 0].astype(bf16)
        w3b[...] = w13f[slot, 1].astype(bf16)
        w2b[...] = w2f[slot].astype(bf16)

        @pl.when(k + 1 < nu_ref[1])
        def _prefetch():
            for cp in fetch(k + 1, 1 - slot):
                cp.start()

    @pl.when(i < nu_ref[0])
    def _compute():
        t = EXPERT_TILE
        s = xs_ref.shape[0] // t
        kc = 2 * LANE
        h1 = jnp.zeros((t, w1b.shape[1]), f32)
        h3 = jnp.zeros((t, w1b.shape[1]), f32)
        for c in range(s // 2):
            xc = jnp.concatenate([xs_ref[pl.ds(2 * c, t, stride=s), :],
                                  xs_ref[pl.ds(2 * c + 1, t, stride=s), :]], axis=1).astype(bf16)
            h1 = h1 + jnp.dot(xc, w1b[c * kc:(c + 1) * kc, :], preferred_element_type=f32)
            h3 = h3 + jnp.dot(xc, w3b[c * kc:(c + 1) * kc, :], preferred_element_type=f32)
        a = (_silu(h1) * h3).astype(bf16)
        for c in range(s // 2):
            yc = jnp.dot(a, w2b[:, c * kc:(c + 1) * kc], preferred_element_type=f32)
            ys_ref[pl.ds(2 * c, t, stride=s), :] = yc[:, 0:LANE]
            ys_ref[pl.ds(2 * c + 1, t, stride=s), :] = yc[:, LANE:kc]

    @pl.when(i >= nu_ref[0])
    def _unused():
        ys_ref[...] = jnp.zeros(ys_ref.shape, ys_ref.dtype)


def _experts(tile_ord, used_experts, n_used, xs, w1, w3, w2):
    _, d, hdim = w1.shape
    t = EXPERT_TILE
    tb = t * d // LANE
    p, w = xs.shape
    grid_spec = pltpu.PrefetchScalarGridSpec(
        num_scalar_prefetch=3,
        grid=(p // tb,),
        in_specs=[pl.BlockSpec((tb, w), lambda i, o, ue, nu: (jnp.minimum(i, nu[0] - 1), 0)),
                  pl.BlockSpec(memory_space=pl.ANY),
                  pl.BlockSpec(memory_space=pl.ANY),
                  pl.BlockSpec(memory_space=pl.ANY)],
        out_specs=pl.BlockSpec((tb, w), lambda i, o, ue, nu: (i, 0)),
        scratch_shapes=[pltpu.VMEM((2, 2, d, hdim), f32), pltpu.VMEM((2, hdim, d), f32),
                        pltpu.VMEM((d, hdim), bf16), pltpu.VMEM((d, hdim), bf16), pltpu.VMEM((hdim, d), bf16),
                        pltpu.SemaphoreType.DMA((2, 3))],
    )
    return pl.pallas_call(
        _expert_kernel,
        grid_spec=grid_spec,
        out_shape=jax.ShapeDtypeStruct((p, w), f32),
        compiler_params=_cparams(1),
        name="experts",
    )(tile_ord, used_experts, n_used, xs, w1, w3, w2)


def _combine_kernel(pos_ref, gate_ref, base_ref, mod_ref, ys_ref, o_ref, gbuf, r_scr, sems, *, n_tokens, tn):
    step = pl.program_id(0)
    n_steps = pl.num_programs(0)
    slot = step % 2

    def gather_tile(s):
        base = s * tn
        sl = s % 2

        def issue(t, carry):
            for j in range(EXPERT_TOPK):
                p = pos_ref[j * n_tokens + base + t]
                pltpu.make_async_copy(ys_ref.at[p], gbuf.at[sl, j, t], sems.at[sl]).start()
            return carry

        lax.fori_loop(0, tn, issue, 0)

    @pl.when(step == 0)
    def _():
        gather_tile(step)

    @pl.when(step + 1 < n_steps)
    def _():
        gather_tile(step + 1)

    for j in range(EXPERT_TOPK):
        pltpu.make_async_copy(ys_ref.at[pl.ds(0, tn)], gbuf.at[slot, j], sems.at[slot]).wait()

    routed = gate_ref[:, 0:1, :] * gbuf[slot, 0]
    for j in range(1, EXPERT_TOPK):
        routed = routed + gate_ref[:, j:j + 1, :] * gbuf[slot, j]
    r_scr[...] = routed.reshape(r_scr.shape)
    o_ref[...] = base_ref[...] + mod_ref[0, 5:6, :] * _slabs_to_rows(r_scr, tn)


def _combine(pos_flat, gate_rep, base, mod8, ys, seq):
    n, d = base.shape
    s = d // LANE
    tn = min(128, seq)
    per_seq = seq // tn
    kern = functools.partial(_combine_kernel, n_tokens=n, tn=tn)
    return pl.pallas_call(
        kern,
        grid=(n // tn,),
        in_specs=[pl.BlockSpec(memory_space=pltpu.SMEM),
                  pl.BlockSpec((tn, EXPERT_TOPK, LANE), lambda i: (i, 0, 0)),
                  pl.BlockSpec((tn, d), lambda i: (i, 0)),
                  pl.BlockSpec((1, 8, d), lambda i: (i // per_seq, 0, 0)),
                  pl.BlockSpec(memory_space=pl.ANY)],
        out_specs=pl.BlockSpec((tn, d), lambda i: (i, 0)),
        out_shape=jax.ShapeDtypeStruct((n, d), f32),
        scratch_shapes=[pltpu.VMEM((2, EXPERT_TOPK, tn, s, LANE), f32), pltpu.VMEM((tn * s, LANE), f32),
                        pltpu.SemaphoreType.DMA((2,))],
        compiler_params=_cparams(1),
        name="combine",
    )(pos_flat, gate_rep, base, mod8, ys)


def _layer(x, c, w_ada, b_ada, g_mix, w_in, q_norm, k_norm, w_attn_up, pool_lin, pool_scale, w_pool_up,
           w_out, g_ffn, w_router, router_bias, w1, w3, w2, ws1, ws3, ws2):
    batch, seq, d = x.shape
    n = batch * seq
    x2 = x.reshape(n, d)

    offs = [0, ATTN_WIDTH, ATTN_WIDTH + KV_WIDTH, ATTN_WIDTH + 2 * KV_WIDTH]
    w_q = w_in[:, offs[0]:offs[1]]
    w_k = w_in[:, offs[1]:offs[2]]
    w_v = w_in[:, offs[2]:offs[3]]
    o_qi = offs[3]
    w_qi = w_in[:, o_qi:o_qi + IDX_HEADS * IDX_DIM]
    o_ki = o_qi + IDX_HEADS * IDX_DIM
    w_ki = w_in[:, o_ki:o_ki + IDX_DIM]
    o_wi = o_ki + IDX_DIM
    w_wi = w_in[:, o_wi:o_wi + IDX_HEADS]
    o_u = o_wi + IDX_HEADS
    w_u = w_in[:, o_u:o_u + POOL_WIDTH]
    o_g = o_u + POOL_WIDTH
    w_g = w_in[:, o_g:o_g + 2 * d]
    w_main = jnp.concatenate([w_g, w_q, w_qi, w_u, w_k, w_v], axis=1).astype(bf16)
    w_kiwi = jnp.concatenate(
        [w_ki, w_wi, jnp.zeros((d, KIWI_WIDTH - IDX_DIM - IDX_HEADS), w_in.dtype)], axis=1).astype(bf16)

    c8 = jnp.zeros((8, d), f32).at[:batch].set(c)
    mod = _ada(c8, w_ada, b_ada.reshape(1, -1))[:batch]
    mod8 = jnp.zeros((batch, 8, d), f32).at[:, :N_MOD].set(mod.reshape(batch, N_MOD, d))

    proj, kiwi = _inproj(x2, mod8, g_mix.reshape(1, d), w_main, w_kiwi, seq)
    attn = _attention(proj, kiwi, q_norm.reshape(1, -1), k_norm.reshape(1, -1), batch, seq)
    pool = _pool(proj, pool_lin.astype(bf16), pool_scale.reshape(1, -1), seq)
    merged = _merge(attn, pool, w_attn_up.astype(bf16), w_pool_up.astype(bf16), proj)
    wr_hi = w_router.astype(bf16)
    wr_lo = (w_router - wr_hi.astype(f32)).astype(bf16)
    base, h2, logits = _outproj(merged, x2, mod8, g_ffn.reshape(1, d), w_out.astype(bf16),
                                jnp.concatenate([wr_hi, wr_lo], axis=1),
                                ws1.astype(bf16), ws3.astype(bf16), ws2.astype(bf16), seq)

    logits_t = logits.T
    bias_col = router_bias.reshape(N_EXPERTS, 1)
    counts = _route_counts(logits_t, bias_col)[:, 0].astype(i32)
    t = EXPERT_TILE
    tiles_e = (counts + t - 1) // t
    tile_end = jnp.cumsum(tiles_e)
    pstart = ((tile_end - tiles_e) * t).astype(f32).reshape(N_EXPERTS, 1)
    n_tiles = n * EXPERT_TOPK // t + N_EXPERTS
    n_used = tile_end[-1]
    tile_ids = jnp.minimum(jnp.arange(n_tiles, dtype=i32), n_used - 1)
    block_e = jnp.sum((tile_end[None, :] <= tile_ids[:, None]).astype(i32), axis=1)
    block_e = jnp.minimum(block_e, N_EXPERTS - 1)
    pad_tile = jnp.where(tiles_e > 0, (tile_end - 1) * t, -1).astype(i32)
    pad_tile = jnp.concatenate([pad_tile, n_used.reshape(1).astype(i32)])
    gate8, pos8 = _route_assign(logits_t, bias_col, pstart)
    pos_flat = pos8.reshape(-1)

    cum_used = jnp.cumsum((tiles_e > 0).astype(i32))
    slots = jnp.arange(N_EXPERTS, dtype=i32)
    used_experts = jnp.minimum(jnp.sum((cum_used[None, :] <= slots[:, None]).astype(i32), axis=1), N_EXPERTS - 1)
    tile_ord = jnp.sum(jnp.where(block_e[:, None] == slots[None, :], cum_used[None, :] - 1, 0), axis=1).astype(i32)
    n_used2 = jnp.stack([n_used, cum_used[-1]]).astype(i32)

    slabs = d // LANE
    xs = _dispatch(pos_flat, pad_tile, h2.reshape(n, slabs, LANE), n_tiles * t)
    ys = _experts(tile_ord, used_experts, n_used2, xs.reshape(-1, LANE), w1, w3, w2)
    gate_rep = jnp.broadcast_to(gate8.T[:, :, None], (n, EXPERT_TOPK, LANE))
    out = _combine(pos_flat, gate_rep, base, mod8, ys.reshape(-1, slabs, LANE), seq)
    return out.reshape(batch, seq, d)


def kernel(x, c, w_ada, b_ada, g_mix, w_in, q_norm, k_norm, w_attn_up, pool_lin, pool_scale, w_pool_up, w_out, g_ffn, w_router, router_bias, w1, w3, w2, ws1, ws3, ws2):
    for l in range(w_ada.shape[0]):
        x = _layer(x, c, w_ada[l], b_ada[l], g_mix[l], w_in[l], q_norm[l], k_norm[l], w_attn_up[l], pool_lin[l],
                   pool_scale[l], w_pool_up[l], w_out[l], g_ffn[l], w_router[l], router_bias[l], w1[l], w3[l],
                   w2[l], ws1[l], ws3[l], ws2[l])
    return x
```

```python
import functools

import jax
import jax.numpy as jnp
from jax import lax
from jax.experimental import pallas as pl
from jax.experimental.pallas import tpu as pltpu

f32 = jnp.float32
bf16 = jnp.bfloat16
i32 = jnp.int32

N_HEADS = 8
HEAD_DIM = 128
N_KV_HEADS = 2
Q_PER_KV = N_HEADS // N_KV_HEADS
ATTN_WIDTH = N_HEADS * HEAD_DIM
KV_WIDTH = N_KV_HEADS * HEAD_DIM
IDX_HEADS = 16
IDX_DIM = 64
IDX_TOPK_MAX = 256
Q_BLOCK = 128
LANE = 128
POOL_WINDOWS = (2, 4, 8, 16)
POOL_GROUP_DIM = 256
POOL_WIDTH = 1024
POOL_HALO = 16
N_EXPERTS = 64
N_EXPERT_GROUPS = 8
EXPERTS_PER_GROUP = 8
TOPK_GROUPS = 4
EXPERT_TOPK = 8
EXPERT_HIDDEN = 512
ROUTED_SCALE = 2.5
NORM_EPS = 1e-6
N_MOD = 6

COL_GATE_A = 0
COL_GATE_P = 2048
COL_Q = 4096
COL_QI = 5120
COL_U = 6144
COL_K = 7168
COL_V = 7424
MAIN_WIDTH = 7680
KIWI_WIDTH = 128

KEY_CHUNK = 512
EXPERT_TILE = 256
VMEM_LIMIT = 56 * 1024 * 1024
INT_MIN = -2147483648
KEY_OF_LOWEST_FINITE = -2139095040
NEG_BIG = -1e30
LOG2E = 1.4426950408889634


def _alibi_slope(h):
    return 2.0 ** (-8.0 * (h + 1) / N_HEADS)


def _cparams(n_axes, vmem=VMEM_LIMIT):
    return pltpu.CompilerParams(dimension_semantics=("arbitrary",) * n_axes, vmem_limit_bytes=vmem)


def _sigmoid(x):
    return 1.0 / (1.0 + jnp.exp(-x))


def _silu(x):
    return x * _sigmoid(x)


def _rows_to_slabs(ref, val):
    rows, width = val.shape
    s = width // LANE
    for j in range(s):
        ref[pl.ds(j, rows, stride=s), :] = val[:, j * LANE:(j + 1) * LANE]


def _slabs_to_rows(ref, rows):
    s = ref.shape[0] // rows
    return jnp.concatenate([ref[pl.ds(j, rows, stride=s), :] for j in range(s)], axis=1)


def _ada_kernel(c_ref, w_ref, b_ref, o_ref):
    sc = _silu(c_ref[...]).astype(bf16)
    o_ref[...] = jnp.dot(sc, w_ref[...].astype(bf16), preferred_element_type=f32) + b_ref[...]


def _ada(c8, w_ada, b_ada):
    d, n = w_ada.shape
    tn = 1024
    return pl.pallas_call(
        _ada_kernel,
        grid=(n // tn,),
        in_specs=[pl.BlockSpec((8, d), lambda j: (0, 0)),
                  pl.BlockSpec((d, tn), lambda j: (0, j)),
                  pl.BlockSpec((1, tn), lambda j: (0, j))],
        out_specs=pl.BlockSpec((8, tn), lambda j: (0, j)),
        out_shape=jax.ShapeDtypeStruct((8, n), f32),
        compiler_params=_cparams(1),
        name="adaln",
    )(c8, w_ada, b_ada)


def _inproj_kernel(x_ref, mod_ref, g_ref, w_ref, wk_ref, o_ref, kiwi_ref, h_scr):
    @pl.when(pl.program_id(1) == 0)
    def _():
        x = x_ref[...]
        y = x * lax.rsqrt(jnp.mean(x * x, axis=-1, keepdims=True) + NORM_EPS) * g_ref[...]
        h = y * (1.0 + mod_ref[0, 1:2, :]) + mod_ref[0, 0:1, :]
        hb = h.astype(bf16)
        h_scr[...] = hb
        kiwi_ref[...] = jnp.dot(hb, wk_ref[...], preferred_element_type=f32)

    o_ref[...] = jnp.dot(h_scr[...], w_ref[...], preferred_element_type=f32).astype(o_ref.dtype)


def _inproj(x2, mod8, g_mix, w_main, w_kiwi, seq):
    n, d = x2.shape
    tm, tn = 1024, 1536
    tm = min(tm, seq)
    per_seq = seq // tm
    return pl.pallas_call(
        _inproj_kernel,
        grid=(n // tm, MAIN_WIDTH // tn),
        in_specs=[pl.BlockSpec((tm, d), lambda i, j: (i, 0)),
                  pl.BlockSpec((1, 8, d), lambda i, j: (i // per_seq, 0, 0)),
                  pl.BlockSpec((1, d), lambda i, j: (0, 0)),
                  pl.BlockSpec((d, tn), lambda i, j: (0, j)),
                  pl.BlockSpec((d, KIWI_WIDTH), lambda i, j: (0, 0))],
        out_specs=[pl.BlockSpec((tm, tn), lambda i, j: (i, j)),
                   pl.BlockSpec((tm, KIWI_WIDTH), lambda i, j: (i, 0))],
        out_shape=[jax.ShapeDtypeStruct((n, MAIN_WIDTH), bf16),
                   jax.ShapeDtypeStruct((n, KIWI_WIDTH), f32)],
        scratch_shapes=[pltpu.VMEM((tm, d), bf16)],
        compiler_params=_cparams(2),
        name="inproj",
    )(x2, mod8, g_mix, w_main, w_kiwi)


def _tree_sum(parts):
    while len(parts) > 1:
        parts = [parts[a] + parts[a + 1] for a in range(0, len(parts) - 1, 2)] + ([parts[-1]] if len(parts) % 2 else [])
    return parts[0]


def _attn_t_kernel(q_ref, qi_ref, k_ref, v_ref, kiwi_all_ref, kiwi_blk_ref, qn_ref, kn_ref, o_ref,
                   kn_scr, ki_scr, vt_scr, score_scr, qt_scr, qit_scr, bias_scr, m_scr, l_scr, acc_scr, *, seq, topk):
    i = pl.program_id(1)
    ck = min(KEY_CHUNK, seq)
    n_chunks = i // (ck // Q_BLOCK) + 1
    heads_per_dot = 4

    @pl.when(i == 0)
    def _prep_keys():
        for n in range(N_KV_HEADS):
            kf = k_ref[:, n * HEAD_DIM:(n + 1) * HEAD_DIM].astype(f32)
            r = lax.rsqrt(jnp.mean(kf * kf, axis=-1, keepdims=True) + NORM_EPS)
            kn_scr[:, n * HEAD_DIM:(n + 1) * HEAD_DIM] = (kf * r * kn_ref[...]).astype(bf16)
        ki_scr[...] = kiwi_all_ref[:, 0:IDX_DIM].astype(bf16)
        for c in range(seq // ck):
            vt_scr[c] = v_ref[c * ck:(c + 1) * ck, :].astype(f32).T.astype(bf16)
        rel = (lax.broadcasted_iota(i32, (ck, Q_BLOCK), 1) - lax.broadcasted_iota(i32, (ck, Q_BLOCK), 0)).astype(f32)
        for h in range(N_HEADS):
            bias_scr[h] = rel * (-_alibi_slope(h) * LOG2E)

    q_t = []
    for h in range(N_HEADS):
        qf = q_ref[:, h * HEAD_DIM:(h + 1) * HEAD_DIM].astype(f32)
        r = lax.rsqrt(jnp.mean(qf * qf, axis=-1, keepdims=True) + NORM_EPS)
        q_t.append((qf * r * qn_ref[...] * (HEAD_DIM ** -0.5 * LOG2E)).T)
    for n in range(N_KV_HEADS):
        qt_scr[n] = jnp.concatenate(q_t[n * Q_PER_KV:(n + 1) * Q_PER_KV], axis=1).astype(bf16)
    qi_t = qi_ref[...].astype(f32).T
    for a in range(IDX_HEADS // heads_per_dot):
        qit_scr[a] = jnp.concatenate(
            [qi_t[(a * heads_per_dot + b) * IDX_DIM:(a * heads_per_dot + b + 1) * IDX_DIM, :]
             for b in range(heads_per_dot)], axis=1).astype(bf16)
    wi_t = kiwi_blk_ref[...].T[IDX_DIM:IDX_DIM + IDX_HEADS, :] * (IDX_HEADS ** -0.5 * IDX_DIM ** -0.5)
    qpos = i * Q_BLOCK + lax.broadcasted_iota(i32, (1, Q_BLOCK), 1)

    def index_chunk(c, carry):
        start = pl.multiple_of(c * ck, ck)
        kc = ki_scr[pl.ds(start, ck), :]
        acc = jnp.zeros((ck, Q_BLOCK), f32)
        for a in range(IDX_HEADS // heads_per_dot):
            d = jnp.dot(kc, qit_scr[a], preferred_element_type=f32)
            for b in range(heads_per_dot):
                h = a * heads_per_dot + b
                acc = acc + jnp.maximum(d[:, b * Q_BLOCK:(b + 1) * Q_BLOCK], 0.0) * wi_t[h:h + 1, :]
        kpos = start + lax.broadcasted_iota(i32, (ck, 1), 0)
        score_scr[c] = jnp.where(kpos <= qpos, acc, -jnp.inf)
        return carry

    lax.fori_loop(0, n_chunks, index_chunk, 0)

    def key_to_float(key):
        bits = key ^ (lax.shift_right_arithmetic(key, 31) & jnp.int32(0x7FFFFFFF))
        return lax.bitcast_convert_type(bits, f32)

    def bit_step(b, t_u):
        bit = lax.shift_left(jnp.int32(1), 31 - b)
        cand_u = t_u | bit
        cand = key_to_float(cand_u ^ jnp.int32(INT_MIN))

        def count_chunk(c, cnt):
            ge = jnp.where(score_scr[c] >= cand, 1.0, 0.0)
            return cnt + _tree_sum([ge[s * 8:(s + 1) * 8, :] for s in range(ck // 8)])

        cnt = lax.fori_loop(0, n_chunks, count_chunk, jnp.zeros((8, Q_BLOCK), f32))
        total = jnp.sum(cnt, axis=0, keepdims=True)
        return jnp.where(total >= float(topk), cand_u, t_u)

    t_u = lax.fori_loop(0, 32, bit_step, jnp.zeros((1, Q_BLOCK), i32))
    thr = key_to_float(jnp.maximum(t_u ^ jnp.int32(INT_MIN), jnp.int32(KEY_OF_LOWEST_FINITE)))

    def count_ties(c, carry):
        ge, gt = carry
        sc = score_scr[c]
        ge = ge + _tree_sum([jnp.where(sc[s * 8:(s + 1) * 8, :] >= thr, 1.0, 0.0) for s in range(ck // 8)])
        gt = gt + _tree_sum([jnp.where(sc[s * 8:(s + 1) * 8, :] > thr, 1.0, 0.0) for s in range(ck // 8)])
        return ge, gt

    zeros8 = jnp.zeros((8, Q_BLOCK), f32)
    ge8, gt8 = lax.fori_loop(0, n_chunks, count_ties, (zeros8, zeros8))
    excess = jnp.sum(ge8, axis=0, keepdims=True) > float(topk)
    need = float(topk) - jnp.sum(gt8, axis=0, keepdims=True)

    @pl.when(jnp.max(jnp.where(excess, 1.0, 0.0)) > 0.0)
    def _break_ties():
        pos_bits = (seq - 1).bit_length()

        def key_positions(c):
            return c * ck + lax.broadcasted_iota(i32, (ck, Q_BLOCK), 0)

        def pos_step(b, q):
            cand = q | lax.shift_left(jnp.int32(1), pos_bits - 1 - b)

            def count_before(c, cnt):
                hit = jnp.where((score_scr[c] == thr) & (key_positions(c) < cand), 1.0, 0.0)
                return cnt + _tree_sum([hit[s * 8:(s + 1) * 8, :] for s in range(ck // 8)])

            before = jnp.sum(lax.fori_loop(0, n_chunks, count_before, zeros8), axis=0, keepdims=True)
            return jnp.where(before < need, cand, q)

        q = lax.fori_loop(0, pos_bits, pos_step, jnp.zeros((1, Q_BLOCK), i32))
        q = jnp.where(excess, q, jnp.int32(seq))

        def demote(c, carry):
            sc = score_scr[c]
            score_scr[c] = jnp.where((sc == thr) & (key_positions(c) > q), -jnp.inf, sc)
            return carry

        lax.fori_loop(0, n_chunks, demote, 0)

    m_scr[...] = jnp.full(m_scr.shape, NEG_BIG, f32)
    l_scr[...] = jnp.zeros(l_scr.shape, f32)
    acc_scr[...] = jnp.zeros(acc_scr.shape, f32)
    lane_head = lax.broadcasted_iota(i32, (1, Q_PER_KV * Q_BLOCK), 1) // Q_BLOCK

    def attn_chunk(c, carry):
        start = pl.multiple_of(c * ck, ck)
        sel = score_scr[c] >= thr
        tile_dist = (i * Q_BLOCK - start).astype(f32)
        for n in range(N_KV_HEADS):
            kc = kn_scr[pl.ds(start, ck), n * HEAD_DIM:(n + 1) * HEAD_DIM]
            vt = vt_scr[c, n * HEAD_DIM:(n + 1) * HEAD_DIM, :]
            z_all = jnp.dot(kc, qt_scr[n], preferred_element_type=f32)
            parts = []
            slope_row = jnp.zeros((1, Q_PER_KV * Q_BLOCK), f32)
            for g in range(Q_PER_KV):
                h = n * Q_PER_KV + g
                parts.append(jnp.where(sel, z_all[:, g * Q_BLOCK:(g + 1) * Q_BLOCK] + bias_scr[h], NEG_BIG))
                slope_row = jnp.where(lane_head == g, _alibi_slope(h) * LOG2E, slope_row)
            z = jnp.concatenate(parts, axis=1)
            off = slope_row * (-tile_dist)
            m_old = m_scr[n]
            m_new = jnp.maximum(m_old, jnp.max(z, axis=0, keepdims=True) + off)
            alpha = jnp.exp2(m_old - m_new)
            p = jnp.exp2(z + (off - m_new))
            l_scr[n] = alpha * l_scr[n] + jnp.sum(p, axis=0, keepdims=True)
            acc_scr[n] = alpha * acc_scr[n] + jnp.dot(vt, p.astype(bf16), preferred_element_type=f32)
            m_scr[n] = m_new
        return carry

    lax.fori_loop(0, n_chunks, attn_chunk, 0)
    for n in range(N_KV_HEADS):
        out_t = acc_scr[n] / l_scr[n]
        for g in range(Q_PER_KV):
            h = n * Q_PER_KV + g
            o_ref[:, h * HEAD_DIM:(h + 1) * HEAD_DIM] = out_t[:, g * Q_BLOCK:(g + 1) * Q_BLOCK].T.astype(o_ref.dtype)


def _attention(proj, kiwi, q_norm, k_norm, batch, seq):
    n = batch * seq
    nq = seq // Q_BLOCK
    ck = min(KEY_CHUNK, seq)
    topk = min(IDX_TOPK_MAX, seq // 4)
    kern = functools.partial(_attn_t_kernel, seq=seq, topk=topk)
    return pl.pallas_call(
        kern,
        grid=(batch, nq),
        in_specs=[pl.BlockSpec((Q_BLOCK, ATTN_WIDTH), lambda b, i: (b * nq + i, COL_Q // ATTN_WIDTH)),
                  pl.BlockSpec((Q_BLOCK, IDX_HEADS * IDX_DIM), lambda b, i: (b * nq + i, COL_QI // (IDX_HEADS * IDX_DIM))),
                  pl.BlockSpec((seq, KV_WIDTH), lambda b, i: (b, COL_K // KV_WIDTH)),
                  pl.BlockSpec((seq, KV_WIDTH), lambda b, i: (b, COL_V // KV_WIDTH)),
                  pl.BlockSpec((seq, KIWI_WIDTH), lambda b, i: (b, 0)),
                  pl.BlockSpec((Q_BLOCK, KIWI_WIDTH), lambda b, i: (b * nq + i, 0)),
                  pl.BlockSpec((1, HEAD_DIM), lambda b, i: (0, 0)),
                  pl.BlockSpec((1, HEAD_DIM), lambda b, i: (0, 0))],
        out_specs=pl.BlockSpec((Q_BLOCK, ATTN_WIDTH), lambda b, i: (b * nq + i, 0)),
        out_shape=jax.ShapeDtypeStruct((n, ATTN_WIDTH), bf16),
        scratch_shapes=[pltpu.VMEM((seq, KV_WIDTH), bf16),
                        pltpu.VMEM((seq, IDX_DIM), bf16),
                        pltpu.VMEM((seq // ck, KV_WIDTH, ck), bf16),
                        pltpu.VMEM((seq // ck, ck, Q_BLOCK), f32),
                        pltpu.VMEM((N_KV_HEADS, HEAD_DIM, Q_PER_KV * Q_BLOCK), bf16),
                        pltpu.VMEM((IDX_HEADS // 4, IDX_DIM, 4 * Q_BLOCK), bf16),
                        pltpu.VMEM((N_HEADS, ck, Q_BLOCK), f32),
                        pltpu.VMEM((N_KV_HEADS, 1, Q_PER_KV * Q_BLOCK), f32),
                        pltpu.VMEM((N_KV_HEADS, 1, Q_PER_KV * Q_BLOCK), f32),
                        pltpu.VMEM((N_KV_HEADS, HEAD_DIM, Q_PER_KV * Q_BLOCK), f32)],
        compiler_params=_cparams(2),
        name="sparse_attn",
    )(proj, proj, proj, proj, kiwi, kiwi, q_norm, k_norm)


def _pool_kernel(u_ref, halo_ref, lin_ref, ps_ref, o_ref, scr, *, tm, per_seq):
    i = pl.program_id(0)
    first = (i % per_seq) == 0
    scr[0:POOL_HALO, :] = jnp.where(first, 0.0, halo_ref[...].astype(f32))
    scr[POOL_HALO:POOL_HALO + tm, :] = u_ref[...].astype(f32)
    t_in_seq = (i % per_seq) * tm + lax.broadcasted_iota(i32, (tm, 1), 0)
    for g, w in enumerate(POOL_WINDOWS):
        c0, c1 = g * POOL_GROUP_DIM, (g + 1) * POOL_GROUP_DIM
        cur = scr[POOL_HALO:POOL_HALO + tm, c0:c1]
        s = cur
        for j in range(1, w):
            s = s + scr[POOL_HALO - j:POOL_HALO - j + tm, c0:c1]
        count = jnp.minimum(t_in_seq + 1, w).astype(f32)
        pooled = s / count - cur
        mixed = jnp.dot(pooled.astype(bf16), lin_ref[g], preferred_element_type=f32)
        o_ref[:, c0:c1] = (mixed * ps_ref[:, c0:c1]).astype(o_ref.dtype)


def _pool(proj, pool_lin_b, pool_scale, seq):
    n = proj.shape[0]
    tm = min(512, seq)
    per_seq = seq // tm
    hb = tm // POOL_HALO
    kern = functools.partial(_pool_kernel, tm=tm, per_seq=per_seq)
    return pl.pallas_call(
        kern,
        grid=(n // tm,),
        in_specs=[pl.BlockSpec((tm, POOL_WIDTH), lambda i: (i, COL_U // POOL_WIDTH)),
                  pl.BlockSpec((POOL_HALO, POOL_WIDTH), lambda i: (jnp.maximum(i * hb - 1, 0), COL_U // POOL_WIDTH)),
                  pl.BlockSpec((len(POOL_WINDOWS), POOL_GROUP_DIM, POOL_GROUP_DIM), lambda i: (0, 0, 0)),
                  pl.BlockSpec((1, POOL_WIDTH), lambda i: (0, 0))],
        out_specs=pl.BlockSpec((tm, POOL_WIDTH), lambda i: (i, 0)),
        out_shape=jax.ShapeDtypeStruct((n, POOL_WIDTH), bf16),
        scratch_shapes=[pltpu.VMEM((POOL_HALO + tm, POOL_WIDTH), f32)],
        compiler_params=_cparams(1),
        name="pool",
    )(proj, proj, pool_lin_b, pool_scale)


def _merge_kernel(a_ref, p_ref, wa_ref, wp_ref, ga_ref, gp_ref, o_ref):
    ya = jnp.dot(a_ref[...], wa_ref[...], preferred_element_type=f32)
    yp = jnp.dot(p_ref[...], wp_ref[...], preferred_element_type=f32)
    o = _sigmoid(ga_ref[...].astype(f32)) * ya + _sigmoid(gp_ref[...].astype(f32)) * yp
    o_ref[...] = o.astype(o_ref.dtype)


def _merge(attn, pool, w_au, w_pu, proj):
    n = attn.shape[0]
    d = w_au.shape[1]
    tm, tn = min(1024, n), 1024
    ga0, gp0 = COL_GATE_A // tn, COL_GATE_P // tn
    return pl.pallas_call(
        _merge_kernel,
        grid=(n // tm, d // tn),
        in_specs=[pl.BlockSpec((tm, ATTN_WIDTH), lambda i, j: (i, 0)),
                  pl.BlockSpec((tm, POOL_WIDTH), lambda i, j: (i, 0)),
                  pl.BlockSpec((ATTN_WIDTH, tn), lambda i, j: (0, j)),
                  pl.BlockSpec((POOL_WIDTH, tn), lambda i, j: (0, j)),
                  pl.BlockSpec((tm, tn), lambda i, j: (i, ga0 + j)),
                  pl.BlockSpec((tm, tn), lambda i, j: (i, gp0 + j))],
        out_specs=pl.BlockSpec((tm, tn), lambda i, j: (i, j)),
        out_shape=jax.ShapeDtypeStruct((n, d), bf16),
        compiler_params=_cparams(2),
        name="merge",
    )(attn, pool, w_au, w_pu, proj, proj)


def _outproj_kernel(m_ref, x_ref, mod_ref, g_ref, wo_ref, wr_ref, ws1_ref, ws3_ref, ws2_ref,
                    base_ref, h2_ref, lg_ref):
    y = jnp.dot(m_ref[...], wo_ref[...], preferred_element_type=f32)
    x1 = x_ref[...] + mod_ref[0, 2:3, :] * y
    hn = x1 * lax.rsqrt(jnp.mean(x1 * x1, axis=-1, keepdims=True) + NORM_EPS) * g_ref[...]
    h2 = hn * (1.0 + mod_ref[0, 4:5, :]) + mod_ref[0, 3:4, :]
    _rows_to_slabs(h2_ref, h2)
    h_hi = h2.astype(bf16)
    act = (_silu(jnp.dot(h_hi, ws1_ref[...], preferred_element_type=f32))
           * jnp.dot(h_hi, ws3_ref[...], preferred_element_type=f32)).astype(bf16)
    shared = jnp.dot(act, ws2_ref[...], preferred_element_type=f32)
    base_ref[...] = x1 + mod_ref[0, 5:6, :] * shared
    h_lo = (h2 - h_hi.astype(f32)).astype(bf16)
    a = jnp.dot(h_hi, wr_ref[...], preferred_element_type=f32)
    b = jnp.dot(h_lo, wr_ref[:, 0:N_EXPERTS], preferred_element_type=f32)
    lg_ref[...] = a[:, 0:N_EXPERTS] + (a[:, N_EXPERTS:2 * N_EXPERTS] + b)


def _outproj(merged, x2, mod8, g_ffn, w_out_b, w_router, ws1b, ws3b, ws2b, seq):
    n, d = x2.shape
    hdim = ws1b.shape[1]
    tm = min(256, seq)
    per_seq = seq // tm
    return pl.pallas_call(
        _outproj_kernel,
        grid=(n // tm,),
        in_specs=[pl.BlockSpec((tm, d), lambda i: (i, 0)),
                  pl.BlockSpec((tm, d), lambda i: (i, 0)),
                  pl.BlockSpec((1, 8, d), lambda i: (i // per_seq, 0, 0)),
                  pl.BlockSpec((1, d), lambda i: (0, 0)),
                  pl.BlockSpec((d, d), lambda i: (0, 0)),
                  pl.BlockSpec((d, 2 * N_EXPERTS), lambda i: (0, 0)),
                  pl.BlockSpec((d, hdim), lambda i: (0, 0)),
                  pl.BlockSpec((d, hdim), lambda i: (0, 0)),
                  pl.BlockSpec((hdim, d), lambda i: (0, 0))],
        out_specs=[pl.BlockSpec((tm, d), lambda i: (i, 0)),
                   pl.BlockSpec((tm * d // LANE, LANE), lambda i: (i, 0)),
                   pl.BlockSpec((tm, N_EXPERTS), lambda i: (i, 0))],
        out_shape=[jax.ShapeDtypeStruct((n, d), f32),
                   jax.ShapeDtypeStruct((n * d // LANE, LANE), f32),
                   jax.ShapeDtypeStruct((n, N_EXPERTS), f32)],
        compiler_params=_cparams(1),
        name="outproj",
    )(merged, x2, mod8, g_ffn, w_out_b, w_router, ws1b, ws3b, ws2b)


def _route_select(lg_ref, bias_ref):
    s = _sigmoid(lg_ref[...])
    sel = s + bias_ref[...]
    rows = [sel[EXPERTS_PER_GROUP * g:EXPERTS_PER_GROUP * (g + 1), :] for g in range(N_EXPERT_GROUPS)]
    grp = []
    for r in rows:
        m1 = jnp.max(r, axis=0, keepdims=True)
        eq = r == m1
        n_eq = jnp.sum(jnp.where(eq, 1.0, 0.0), axis=0, keepdims=True)
        m2 = jnp.max(jnp.where(eq, -jnp.inf, r), axis=0, keepdims=True)
        grp.append(m1 + jnp.where(n_eq >= 2.0, m1, m2))
    masked = []
    for g in range(N_EXPERT_GROUPS):
        rank = jnp.zeros_like(grp[g])
        for g2 in range(N_EXPERT_GROUPS):
            if g2 == g:
                continue
            beats = (grp[g2] >= grp[g]) if g2 < g else (grp[g2] > grp[g])
            rank = rank + jnp.where(beats, 1.0, 0.0)
        masked.append(jnp.where(rank < float(TOPK_GROUPS), rows[g], -jnp.inf))
    masked = jnp.concatenate(masked, axis=0)
    eidx = lax.broadcasted_iota(i32, (N_EXPERTS, 1), 0)
    rank = jnp.zeros_like(masked)
    for e2 in range(N_EXPERTS):
        row = masked[e2:e2 + 1, :]
        beats = (row > masked) | ((row == masked) & (eidx > e2))
        rank = rank + jnp.where(beats, 1.0, 0.0)
    return s, rank, rank < float(EXPERT_TOPK)


def _route_count_kernel(lg_ref, bias_ref, cnt_ref):
    @pl.when(pl.program_id(0) == 0)
    def _():
        cnt_ref[...] = jnp.zeros(cnt_ref.shape, f32)

    _, _, selected = _route_select(lg_ref, bias_ref)
    cnt_ref[...] += jnp.sum(jnp.where(selected, 1.0, 0.0), axis=1, keepdims=True)


def _route_assign_kernel(lg_ref, bias_ref, pstart_ref, gate_ref, pos_ref, run_scr):
    @pl.when(pl.program_id(0) == 0)
    def _():
        run_scr[...] = jnp.zeros(run_scr.shape, f32)

    tn = lg_ref.shape[1]
    s, rank, selected = _route_select(lg_ref, bias_ref)
    sel_f = jnp.where(selected, 1.0, 0.0)
    earlier = lax.broadcasted_iota(i32, (tn, tn), 0) < lax.broadcasted_iota(i32, (tn, tn), 1)
    prefix = jnp.dot(sel_f.astype(bf16), jnp.where(earlier, 1.0, 0.0).astype(bf16), preferred_element_type=f32)
    pos = pstart_ref[...] + run_scr[...] + prefix
    run_scr[...] += jnp.sum(sel_f, axis=1, keepdims=True)
    gate = jnp.where(selected, s, 0.0)
    gate = gate / jnp.sum(gate, axis=0, keepdims=True) * ROUTED_SCALE
    for j in range(EXPERT_TOPK):
        slot = rank == float(j)
        gate_ref[j:j + 1, :] = jnp.sum(jnp.where(slot, gate, 0.0), axis=0, keepdims=True)
        pos_ref[j:j + 1, :] = jnp.sum(jnp.where(slot, pos, 0.0), axis=0, keepdims=True).astype(i32)


def _route_counts(logits_t, bias_col):
    e, n = logits_t.shape
    tn = min(512, n)
    return pl.pallas_call(
        _route_count_kernel,
        grid=(n // tn,),
        in_specs=[pl.BlockSpec((e, tn), lambda t: (0, t)),
                  pl.BlockSpec((e, 1), lambda t: (0, 0))],
        out_specs=pl.BlockSpec((e, 1), lambda t: (0, 0)),
        out_shape=jax.ShapeDtypeStruct((e, 1), f32),
        compiler_params=_cparams(1),
        name="route_count",
    )(logits_t, bias_col)


def _route_assign(logits_t, bias_col, pstart_col):
    e, n = logits_t.shape
    tn = min(512, n)
    return pl.pallas_call(
        _route_assign_kernel,
        grid=(n // tn,),
        in_specs=[pl.BlockSpec((e, tn), lambda t: (0, t)),
                  pl.BlockSpec((e, 1), lambda t: (0, 0)),
                  pl.BlockSpec((e, 1), lambda t: (0, 0))],
        out_specs=[pl.BlockSpec((EXPERT_TOPK, tn), lambda t: (0, t)),
                   pl.BlockSpec((EXPERT_TOPK, tn), lambda t: (0, t))],
        out_shape=[jax.ShapeDtypeStruct((EXPERT_TOPK, n), f32),
                   jax.ShapeDtypeStruct((EXPERT_TOPK, n), i32)],
        scratch_shapes=[pltpu.VMEM((e, 1), f32)],
        compiler_params=_cparams(1),
        name="route_assign",
    )(logits_t, bias_col, pstart_col)


def _dispatch_kernel(pos_ref, pad_tile_ref, h_ref, xs_ref, zero_scr, sem, fill_sem, *, n_tokens, tn):
    base = pl.program_id(0) * tn

    @pl.when(pl.program_id(0) == 0)
    def _zero_padded_tiles():
        zero_scr[...] = jnp.zeros(zero_scr.shape, zero_scr.dtype)

        def fill(e):
            row = pl.multiple_of(pad_tile_ref[e], EXPERT_TILE)
            return pltpu.make_async_copy(zero_scr, xs_ref.at[pl.ds(row, EXPERT_TILE)], fill_sem)

        def start(e, carry):
            @pl.when(pad_tile_ref[e] >= 0)
            def _():
                fill(e).start()
            return carry

        def wait(e, carry):
            @pl.when(pad_tile_ref[e] >= 0)
            def _():
                fill(e).wait()
            return carry

        def tail(tile):
            row = pl.multiple_of(tile * EXPERT_TILE, EXPERT_TILE)
            return pltpu.make_async_copy(zero_scr, xs_ref.at[pl.ds(row, EXPERT_TILE)], fill_sem)

        def start_tail(tile, carry):
            tail(tile).start()
            return carry

        def wait_tail(tile, carry):
            tail(tile).wait()
            return carry

        n_tiles_total = xs_ref.shape[0] // EXPERT_TILE
        lax.fori_loop(0, N_EXPERTS, start, 0)
        lax.fori_loop(pad_tile_ref[N_EXPERTS], n_tiles_total, start_tail, 0)
        lax.fori_loop(0, N_EXPERTS, wait, 0)
        lax.fori_loop(pad_tile_ref[N_EXPERTS], n_tiles_total, wait_tail, 0)

    def row_copy(t, p):
        return pltpu.make_async_copy(h_ref.at[t], xs_ref.at[p], sem)

    def issue(t, carry):
        for j in range(EXPERT_TOPK):
            row_copy(t, pos_ref[j * n_tokens + base + t]).start(priority=j % 2)
        return carry

    lax.fori_loop(0, tn, issue, 0)

    for j in range(EXPERT_TOPK):
        pltpu.make_async_copy(h_ref, xs_ref.at[pl.ds(0, tn)], sem).wait()


def _dispatch(pos_flat, pad_tile, h2, n_rows):
    n, s, w = h2.shape
    tn = min(256, n)
    kern = functools.partial(_dispatch_kernel, n_tokens=n, tn=tn)
    return pl.pallas_call(
        kern,
        grid=(n // tn,),
        in_specs=[pl.BlockSpec(memory_space=pltpu.SMEM),
                  pl.BlockSpec(memory_space=pltpu.SMEM),
                  pl.BlockSpec((tn, s, w), lambda i: (i, 0, 0))],
        out_specs=pl.BlockSpec(memory_space=pl.ANY),
        out_shape=jax.ShapeDtypeStruct((n_rows, s, w), h2.dtype),
        scratch_shapes=[pltpu.VMEM((EXPERT_TILE, s, w), h2.dtype), pltpu.SemaphoreType.DMA(()),
                        pltpu.SemaphoreType.DMA(())],
        compiler_params=_cparams(1),
        name="dispatch",
    )(pos_flat, pad_tile, h2)


def _expert_kernel(ord_ref, ue_ref, nu_ref, xs_ref, w1_hbm, w3_hbm, w2_hbm, ys_ref,
                   w13f, w2f, w1b, w3b, w2b, sems):
    i = pl.program_id(0)
    k = ord_ref[i]

    def fetch(kk, slot):
        e = ue_ref[kk]
        return (pltpu.make_async_copy(w1_hbm.at[e], w13f.at[slot, 0], sems.at[slot, 0]),
                pltpu.make_async_copy(w3_hbm.at[e], w13f.at[slot, 1], sems.at[slot, 1]),
                pltpu.make_async_copy(w2_hbm.at[e], w2f.at[slot], sems.at[slot, 2]))

    @pl.when(i == 0)
    def _first_fetch():
        for cp in fetch(0, 0):
            cp.start()

    def compute_tile(cast_slot):
        t = EXPERT_TILE
        s = xs_ref.shape[0] // t
        kc = 2 * LANE
        h1 = jnp.zeros((t, w1b.shape[1]), f32)
        h3 = jnp.zeros((t, w1b.shape[1]), f32)
        for c in range(s // 2):
            rows = slice(c * kc, (c + 1) * kc)
            if cast_slot is not None:
                w1b[rows, :] = w13f[cast_slot, 0, rows, :].astype(bf16)
                w3b[rows, :] = w13f[cast_slot, 1, rows, :].astype(bf16)
            xc = jnp.concatenate([xs_ref[pl.ds(2 * c, t, stride=s), :],
                                  xs_ref[pl.ds(2 * c + 1, t, stride=s), :]], axis=1).astype(bf16)
            h1 = h1 + jnp.dot(xc, w1b[rows, :], preferred_element_type=f32)
            h3 = h3 + jnp.dot(xc, w3b[rows, :], preferred_element_type=f32)
        a = (_silu(h1) * h3).astype(bf16)
        for c in range(s // 2):
            cols = slice(c * kc, (c + 1) * kc)
            if cast_slot is not None:
                w2b[:, cols] = w2f[cast_slot, :, cols].astype(bf16)
            yc = jnp.dot(a, w2b[:, cols], preferred_element_type=f32)
            ys_ref[pl.ds(2 * c, t, stride=s), :] = yc[:, 0:LANE]
            ys_ref[pl.ds(2 * c + 1, t, stride=s), :] = yc[:, LANE:kc]

    first_of_expert = (i == 0) | (k != ord_ref[jnp.maximum(i - 1, 0)])

    @pl.when((i < nu_ref[0]) & first_of_expert)
    def _first_tile():
        slot = k % 2
        for cp in fetch(k, slot):
            cp.wait()

        @pl.when(k + 1 < nu_ref[1])
        def _prefetch():
            for cp in fetch(k + 1, 1 - slot):
                cp.start()

        compute_tile(slot)

    @pl.when((i < nu_ref[0]) & jnp.logical_not(first_of_expert))
    def _later_tile():
        compute_tile(None)

    @pl.when(i >= nu_ref[0])
    def _unused():
        ys_ref[...] = jnp.zeros(ys_ref.shape, ys_ref.dtype)


def _experts(tile_ord, used_experts, n_used, xs, w1, w3, w2):
    _, d, hdim = w1.shape
    t = EXPERT_TILE
    tb = t * d // LANE
    p, w = xs.shape
    grid_spec = pltpu.PrefetchScalarGridSpec(
        num_scalar_prefetch=3,
        grid=(p // tb,),
        in_specs=[pl.BlockSpec((tb, w), lambda i, o, ue, nu: (jnp.minimum(i, nu[0] - 1), 0)),
                  pl.BlockSpec(memory_space=pl.ANY),
                  pl.BlockSpec(memory_space=pl.ANY),
                  pl.BlockSpec(memory_space=pl.ANY)],
        out_specs=pl.BlockSpec((tb, w), lambda i, o, ue, nu: (i, 0)),
        scratch_shapes=[pltpu.VMEM((2, 2, d, hdim), f32), pltpu.VMEM((2, hdim, d), f32),
                        pltpu.VMEM((d, hdim), bf16), pltpu.VMEM((d, hdim), bf16), pltpu.VMEM((hdim, d), bf16),
                        pltpu.SemaphoreType.DMA((2, 3))],
    )
    return pl.pallas_call(
        _expert_kernel,
        grid_spec=grid_spec,
        out_shape=jax.ShapeDtypeStruct((p, w), f32),
        compiler_params=_cparams(1),
        name="experts",
    )(tile_ord, used_experts, n_used, xs, w1, w3, w2)


def _combine_kernel(pos_ref, gate_ref, base_ref, mod_ref, ys_ref, o_ref, gbuf, r_scr, sems, *, n_tokens, tn):
    step = pl.program_id(0)
    n_steps = pl.num_programs(0)
    slot = step % 2

    def gather_tile(s):
        base = s * tn
        sl = s % 2

        def issue(t, carry):
            for j in range(EXPERT_TOPK):
                p = pos_ref[j * n_tokens + base + t]
                pltpu.make_async_copy(ys_ref.at[p], gbuf.at[sl, j, t], sems.at[sl]).start(priority=j % 2)
            return carry

        lax.fori_loop(0, tn, issue, 0)

    @pl.when(step == 0)
    def _():
        gather_tile(step)

    @pl.when(step + 1 < n_steps)
    def _():
        gather_tile(step + 1)

    for j in range(EXPERT_TOPK):
        pltpu.make_async_copy(ys_ref.at[pl.ds(0, tn)], gbuf.at[slot, j], sems.at[slot]).wait()

    routed = gate_ref[:, 0:1, :] * gbuf[slot, 0]
    for j in range(1, EXPERT_TOPK):
        routed = routed + gate_ref[:, j:j + 1, :] * gbuf[slot, j]
    r_scr[...] = routed.reshape(r_scr.shape)
    o_ref[...] = base_ref[...] + mod_ref[0, 5:6, :] * _slabs_to_rows(r_scr, tn)


def _combine(pos_flat, gate_rep, base, mod8, ys, seq):
    n, d = base.shape
    s = d // LANE
    tn = min(128, seq)
    per_seq = seq // tn
    kern = functools.partial(_combine_kernel, n_tokens=n, tn=tn)
    return pl.pallas_call(
        kern,
        grid=(n // tn,),
        in_specs=[pl.BlockSpec(memory_space=pltpu.SMEM),
                  pl.BlockSpec((tn, EXPERT_TOPK, LANE), lambda i: (i, 0, 0)),
                  pl.BlockSpec((tn, d), lambda i: (i, 0)),
                  pl.BlockSpec((1, 8, d), lambda i: (i // per_seq, 0, 0)),
                  pl.BlockSpec(memory_space=pl.ANY)],
        out_specs=pl.BlockSpec((tn, d), lambda i: (i, 0)),
        out_shape=jax.ShapeDtypeStruct((n, d), f32),
        scratch_shapes=[pltpu.VMEM((2, EXPERT_TOPK, tn, s, LANE), f32), pltpu.VMEM((tn * s, LANE), f32),
                        pltpu.SemaphoreType.DMA((2,))],
        compiler_params=_cparams(1),
        name="combine",
    )(pos_flat, gate_rep, base, mod8, ys)


def _layer(x, c, w_ada, b_ada, g_mix, w_in, q_norm, k_norm, w_attn_up, pool_lin, pool_scale, w_pool_up,
           w_out, g_ffn, w_router, router_bias, w1, w3, w2, ws1, ws3, ws2):
    batch, seq, d = x.shape
    n = batch * seq
    x2 = x.reshape(n, d)

    offs = [0, ATTN_WIDTH, ATTN_WIDTH + KV_WIDTH, ATTN_WIDTH + 2 * KV_WIDTH]
    w_q = w_in[:, offs[0]:offs[1]]
    w_k = w_in[:, offs[1]:offs[2]]
    w_v = w_in[:, offs[2]:offs[3]]
    o_qi = offs[3]
    w_qi = w_in[:, o_qi:o_qi + IDX_HEADS * IDX_DIM]
    o_ki = o_qi + IDX_HEADS * IDX_DIM
    w_ki = w_in[:, o_ki:o_ki + IDX_DIM]
    o_wi = o_ki + IDX_DIM
    w_wi = w_in[:, o_wi:o_wi + IDX_HEADS]
    o_u = o_wi + IDX_HEADS
    w_u = w_in[:, o_u:o_u + POOL_WIDTH]
    o_g = o_u + POOL_WIDTH
    w_g = w_in[:, o_g:o_g + 2 * d]
    w_main = jnp.concatenate([w_g, w_q, w_qi, w_u, w_k, w_v], axis=1).astype(bf16)
    w_kiwi = jnp.concatenate(
        [w_ki, w_wi, jnp.zeros((d, KIWI_WIDTH - IDX_DIM - IDX_HEADS), w_in.dtype)], axis=1).astype(bf16)

    c8 = jnp.zeros((8, d), f32).at[:batch].set(c)
    mod = _ada(c8, w_ada, b_ada.reshape(1, -1))[:batch]
    mod8 = jnp.zeros((batch, 8, d), f32).at[:, :N_MOD].set(mod.reshape(batch, N_MOD, d))

    proj, kiwi = _inproj(x2, mod8, g_mix.reshape(1, d), w_main, w_kiwi, seq)
    attn = _attention(proj, kiwi, q_norm.reshape(1, -1), k_norm.reshape(1, -1), batch, seq)
    pool = _pool(proj, pool_lin.astype(bf16), pool_scale.reshape(1, -1), seq)
    merged = _merge(attn, pool, w_attn_up.astype(bf16), w_pool_up.astype(bf16), proj)
    wr_hi = w_router.astype(bf16)
    wr_lo = (w_router - wr_hi.astype(f32)).astype(bf16)
    base, h2, logits = _outproj(merged, x2, mod8, g_ffn.reshape(1, d), w_out.astype(bf16),
                                jnp.concatenate([wr_hi, wr_lo], axis=1),
                                ws1.astype(bf16), ws3.astype(bf16), ws2.astype(bf16), seq)

    logits_t = logits.T
    bias_col = router_bias.reshape(N_EXPERTS, 1)
    counts = _route_counts(logits_t, bias_col)[:, 0].astype(i32)
    t = EXPERT_TILE
    tiles_e = (counts + t - 1) // t
    tile_end = jnp.cumsum(tiles_e)
    pstart = ((tile_end - tiles_e) * t).astype(f32).reshape(N_EXPERTS, 1)
    n_tiles = n * EXPERT_TOPK // t + N_EXPERTS
    n_used = tile_end[-1]
    tile_ids = jnp.minimum(jnp.arange(n_tiles, dtype=i32), n_used - 1)
    block_e = jnp.sum((tile_end[None, :] <= tile_ids[:, None]).astype(i32), axis=1)
    block_e = jnp.minimum(block_e, N_EXPERTS - 1)
    pad_tile = jnp.where(tiles_e > 0, (tile_end - 1) * t, -1).astype(i32)
    pad_tile = jnp.concatenate([pad_tile, n_used.reshape(1).astype(i32)])
    gate8, pos8 = _route_assign(logits_t, bias_col, pstart)
    pos_flat = pos8.reshape(-1)

    cum_used = jnp.cumsum((tiles_e > 0).astype(i32))
    slots = jnp.arange(N_EXPERTS, dtype=i32)
    used_experts = jnp.minimum(jnp.sum((cum_used[None, :] <= slots[:, None]).astype(i32), axis=1), N_EXPERTS - 1)
    tile_ord = jnp.sum(jnp.where(block_e[:, None] == slots[None, :], cum_used[None, :] - 1, 0), axis=1).astype(i32)
    n_used2 = jnp.stack([n_used, cum_used[-1]]).astype(i32)

    slabs = d // LANE
    xs = _dispatch(pos_flat, pad_tile, h2.reshape(n, slabs, LANE), n_tiles * t)
    ys = _experts(tile_ord, used_experts, n_used2, xs.reshape(-1, LANE), w1, w3, w2)
    gate_rep = jnp.broadcast_to(gate8.T[:, :, None], (n, EXPERT_TOPK, LANE))
    out = _combine(pos_flat, gate_rep, base, mod8, ys.reshape(-1, slabs, LANE), seq)
    return out.reshape(batch, seq, d)


def kernel(x, c, w_ada, b_ada, g_mix, w_in, q_norm, k_norm, w_attn_up, pool_lin, pool_scale, w_pool_up, w_out, g_ffn, w_router, router_bias, w1, w3, w2, ws1, ws3, ws2):
    for l in range(w_ada.shape[0]):
        x = _layer(x, c, w_ada[l], b_ada[l], g_mix[l], w_in[l], q_norm[l], k_norm[l], w_attn_up[l], pool_lin[l],
                   pool_scale[l], w_pool_up[l], w_out[l], g_ffn[l], w_router[l], router_bias[l], w1[l], w3[l],
                   w2[l], ws1[l], ws3[l], ws2[l])
    return x
```

```python
import functools

import jax
import jax.numpy as jnp
from jax import lax
from jax.experimental import pallas as pl
from jax.experimental.pallas import tpu as pltpu

f32 = jnp.float32
bf16 = jnp.bfloat16
i32 = jnp.int32

N_HEADS = 8
HEAD_DIM = 128
N_KV_HEADS = 2
Q_PER_KV = N_HEADS // N_KV_HEADS
ATTN_WIDTH = N_HEADS * HEAD_DIM
KV_WIDTH = N_KV_HEADS * HEAD_DIM
IDX_HEADS = 16
IDX_DIM = 64
IDX_TOPK_MAX = 256
Q_BLOCK = 128
LANE = 128
POOL_WINDOWS = (2, 4, 8, 16)
POOL_GROUP_DIM = 256
POOL_WIDTH = 1024
POOL_HALO = 16
N_EXPERTS = 64
N_EXPERT_GROUPS = 8
EXPERTS_PER_GROUP = 8
TOPK_GROUPS = 4
EXPERT_TOPK = 8
EXPERT_HIDDEN = 512
ROUTED_SCALE = 2.5
NORM_EPS = 1e-6
N_MOD = 6

COL_GATE_A = 0
COL_GATE_P = 2048
COL_Q = 4096
COL_QI = 5120
COL_U = 6144
COL_K = 7168
COL_V = 7424
MAIN_WIDTH = 7680
KIWI_WIDTH = 128

KEY_CHUNK = 512
EXPERT_TILE = 256
VMEM_LIMIT = 56 * 1024 * 1024
INT_MIN = -2147483648
KEY_OF_LOWEST_FINITE = -2139095040
NEG_BIG = -1e30
LOG2E = 1.4426950408889634


def _alibi_slope(h):
    return 2.0 ** (-8.0 * (h + 1) / N_HEADS)


def _cparams(n_axes, vmem=VMEM_LIMIT):
    return pltpu.CompilerParams(dimension_semantics=("arbitrary",) * n_axes, vmem_limit_bytes=vmem)


def _sigmoid(x):
    return 1.0 / (1.0 + jnp.exp(-x))


def _silu(x):
    return x * _sigmoid(x)


def _rows_to_slabs(ref, val):
    rows, width = val.shape
    s = width // LANE
    for j in range(s):
        ref[pl.ds(j, rows, stride=s), :] = val[:, j * LANE:(j + 1) * LANE]


def _slabs_to_rows(ref, rows):
    s = ref.shape[0] // rows
    return jnp.concatenate([ref[pl.ds(j, rows, stride=s), :] for j in range(s)], axis=1)


def _ada_kernel(c_ref, w_ref, b_ref, o_ref):
    sc = _silu(c_ref[...]).astype(bf16)
    o_ref[...] = jnp.dot(sc, w_ref[...].astype(bf16), preferred_element_type=f32) + b_ref[...]


def _ada(c8, w_ada, b_ada):
    d, n = w_ada.shape
    tn = 1024
    return pl.pallas_call(
        _ada_kernel,
        grid=(n // tn,),
        in_specs=[pl.BlockSpec((8, d), lambda j: (0, 0)),
                  pl.BlockSpec((d, tn), lambda j: (0, j)),
                  pl.BlockSpec((1, tn), lambda j: (0, j))],
        out_specs=pl.BlockSpec((8, tn), lambda j: (0, j)),
        out_shape=jax.ShapeDtypeStruct((8, n), f32),
        compiler_params=_cparams(1),
        name="adaln",
    )(c8, w_ada, b_ada)


def _inproj_kernel(x_ref, mod_ref, g_ref, w_ref, wk_ref, o_ref, kiwi_ref, h_scr):
    @pl.when(pl.program_id(1) == 0)
    def _():
        x = x_ref[...]
        y = x * lax.rsqrt(jnp.mean(x * x, axis=-1, keepdims=True) + NORM_EPS) * g_ref[...]
        h = y * (1.0 + mod_ref[0, 1:2, :]) + mod_ref[0, 0:1, :]
        hb = h.astype(bf16)
        h_scr[...] = hb
        kiwi_ref[...] = jnp.dot(hb, wk_ref[...], preferred_element_type=f32)

    o_ref[...] = jnp.dot(h_scr[...], w_ref[...], preferred_element_type=f32).astype(o_ref.dtype)


def _inproj(x2, mod8, g_mix, w_main, w_kiwi, seq):
    n, d = x2.shape
    tm, tn = 1024, 1536
    tm = min(tm, seq)
    per_seq = seq // tm
    return pl.pallas_call(
        _inproj_kernel,
        grid=(n // tm, MAIN_WIDTH // tn),
        in_specs=[pl.BlockSpec((tm, d), lambda i, j: (i, 0)),
                  pl.BlockSpec((1, 8, d), lambda i, j: (i // per_seq, 0, 0)),
                  pl.BlockSpec((1, d), lambda i, j: (0, 0)),
                  pl.BlockSpec((d, tn), lambda i, j: (0, j)),
                  pl.BlockSpec((d, KIWI_WIDTH), lambda i, j: (0, 0))],
        out_specs=[pl.BlockSpec((tm, tn), lambda i, j: (i, j)),
                   pl.BlockSpec((tm, KIWI_WIDTH), lambda i, j: (i, 0))],
        out_shape=[jax.ShapeDtypeStruct((n, MAIN_WIDTH), bf16),
                   jax.ShapeDtypeStruct((n, KIWI_WIDTH), f32)],
        scratch_shapes=[pltpu.VMEM((tm, d), bf16)],
        compiler_params=_cparams(2),
        name="inproj",
    )(x2, mod8, g_mix, w_main, w_kiwi)


def _tree_sum(parts):
    while len(parts) > 1:
        parts = [parts[a] + parts[a + 1] for a in range(0, len(parts) - 1, 2)] + ([parts[-1]] if len(parts) % 2 else [])
    return parts[0]


def _attn_t_kernel(q_ref, qi_ref, k_ref, v_ref, kiwi_all_ref, kiwi_blk_ref, qn_ref, kn_ref, o_ref,
                   kn_scr, ki_scr, vt_scr, score_scr, qt_scr, qit_scr, bias_scr, m_scr, l_scr, acc_scr, *, seq, topk):
    i = pl.program_id(1)
    ck = min(KEY_CHUNK, seq)
    n_chunks = i // (ck // Q_BLOCK) + 1
    heads_per_dot = 4

    @pl.when(i == 0)
    def _prep_keys():
        for n in range(N_KV_HEADS):
            kf = k_ref[:, n * HEAD_DIM:(n + 1) * HEAD_DIM].astype(f32)
            r = lax.rsqrt(jnp.mean(kf * kf, axis=-1, keepdims=True) + NORM_EPS)
            kn_scr[:, n * HEAD_DIM:(n + 1) * HEAD_DIM] = (kf * r * kn_ref[...]).astype(bf16)
        ki_scr[...] = kiwi_all_ref[:, 0:IDX_DIM].astype(bf16)
        for c in range(seq // ck):
            vt_scr[c] = v_ref[c * ck:(c + 1) * ck, :].astype(f32).T.astype(bf16)
        rel = (lax.broadcasted_iota(i32, (ck, Q_BLOCK), 1) - lax.broadcasted_iota(i32, (ck, Q_BLOCK), 0)).astype(f32)
        for h in range(N_HEADS):
            bias_scr[h] = rel * (-_alibi_slope(h) * LOG2E)

    q_t = []
    for h in range(N_HEADS):
        qf = q_ref[:, h * HEAD_DIM:(h + 1) * HEAD_DIM].astype(f32)
        r = lax.rsqrt(jnp.mean(qf * qf, axis=-1, keepdims=True) + NORM_EPS)
        q_t.append((qf * r * qn_ref[...] * (HEAD_DIM ** -0.5 * LOG2E)).T)
    for n in range(N_KV_HEADS):
        qt_scr[n] = jnp.concatenate(q_t[n * Q_PER_KV:(n + 1) * Q_PER_KV], axis=1).astype(bf16)
    qi_t = qi_ref[...].astype(f32).T
    for a in range(IDX_HEADS // heads_per_dot):
        qit_scr[a] = jnp.concatenate(
            [qi_t[(a * heads_per_dot + b) * IDX_DIM:(a * heads_per_dot + b + 1) * IDX_DIM, :]
             for b in range(heads_per_dot)], axis=1).astype(bf16)
    wi_t = kiwi_blk_ref[...].T[IDX_DIM:IDX_DIM + IDX_HEADS, :] * (IDX_HEADS ** -0.5 * IDX_DIM ** -0.5)
    qpos = i * Q_BLOCK + lax.broadcasted_iota(i32, (1, Q_BLOCK), 1)

    def index_chunk(c, carry):
        start = pl.multiple_of(c * ck, ck)
        kc = ki_scr[pl.ds(start, ck), :]
        acc = jnp.zeros((ck, Q_BLOCK), f32)
        for a in range(IDX_HEADS // heads_per_dot):
            d = jnp.dot(kc, qit_scr[a], preferred_element_type=f32)
            for b in range(heads_per_dot):
                h = a * heads_per_dot + b
                acc = acc + jnp.maximum(d[:, b * Q_BLOCK:(b + 1) * Q_BLOCK], 0.0) * wi_t[h:h + 1, :]
        kpos = start + lax.broadcasted_iota(i32, (ck, 1), 0)
        score_scr[c] = jnp.where(kpos <= qpos, acc, -jnp.inf)
        return carry

    lax.fori_loop(0, n_chunks, index_chunk, 0)

    def key_to_float(key):
        bits = key ^ (lax.shift_right_arithmetic(key, 31) & jnp.int32(0x7FFFFFFF))
        return lax.bitcast_convert_type(bits, f32)

    def bit_step(b, t_u):
        bit = lax.shift_left(jnp.int32(1), 31 - b)
        cand_u = t_u | bit
        cand = key_to_float(cand_u ^ jnp.int32(INT_MIN))

        def count_chunk(c, cnt):
            ge = jnp.where(score_scr[c] >= cand, 1.0, 0.0)
            return cnt + _tree_sum([ge[s * 8:(s + 1) * 8, :] for s in range(ck // 8)])

        cnt = lax.fori_loop(0, n_chunks, count_chunk, jnp.zeros((8, Q_BLOCK), f32))
        total = jnp.sum(cnt, axis=0, keepdims=True)
        return jnp.where(total >= float(topk), cand_u, t_u)

    t_u = lax.fori_loop(0, 32, bit_step, jnp.zeros((1, Q_BLOCK), i32))
    thr = key_to_float(jnp.maximum(t_u ^ jnp.int32(INT_MIN), jnp.int32(KEY_OF_LOWEST_FINITE)))

    def count_ties(c, carry):
        ge, gt = carry
        sc = score_scr[c]
        ge = ge + _tree_sum([jnp.where(sc[s * 8:(s + 1) * 8, :] >= thr, 1.0, 0.0) for s in range(ck // 8)])
        gt = gt + _tree_sum([jnp.where(sc[s * 8:(s + 1) * 8, :] > thr, 1.0, 0.0) for s in range(ck // 8)])
        return ge, gt

    zeros8 = jnp.zeros((8, Q_BLOCK), f32)
    ge8, gt8 = lax.fori_loop(0, n_chunks, count_ties, (zeros8, zeros8))
    excess = jnp.sum(ge8, axis=0, keepdims=True) > float(topk)
    need = float(topk) - jnp.sum(gt8, axis=0, keepdims=True)

    @pl.when(jnp.max(jnp.where(excess, 1.0, 0.0)) > 0.0)
    def _break_ties():
        pos_bits = (seq - 1).bit_length()

        def key_positions(c):
            return c * ck + lax.broadcasted_iota(i32, (ck, Q_BLOCK), 0)

        def pos_step(b, q):
            cand = q | lax.shift_left(jnp.int32(1), pos_bits - 1 - b)

            def count_before(c, cnt):
                hit = jnp.where((score_scr[c] == thr) & (key_positions(c) < cand), 1.0, 0.0)
                return cnt + _tree_sum([hit[s * 8:(s + 1) * 8, :] for s in range(ck // 8)])

            before = jnp.sum(lax.fori_loop(0, n_chunks, count_before, zeros8), axis=0, keepdims=True)
            return jnp.where(before < need, cand, q)

        q = lax.fori_loop(0, pos_bits, pos_step, jnp.zeros((1, Q_BLOCK), i32))
        q = jnp.where(excess, q, jnp.int32(seq))

        def demote(c, carry):
            sc = score_scr[c]
            score_scr[c] = jnp.where((sc == thr) & (key_positions(c) > q), -jnp.inf, sc)
            return carry

        lax.fori_loop(0, n_chunks, demote, 0)

    m_scr[...] = jnp.full(m_scr.shape, NEG_BIG, f32)
    l_scr[...] = jnp.zeros(l_scr.shape, f32)
    acc_scr[...] = jnp.zeros(acc_scr.shape, f32)
    lane_head = lax.broadcasted_iota(i32, (1, Q_PER_KV * Q_BLOCK), 1) // Q_BLOCK

    def attn_chunk(c, carry):
        start = pl.multiple_of(c * ck, ck)
        sel = score_scr[c] >= thr
        tile_dist = (i * Q_BLOCK - start).astype(f32)
        for n in range(N_KV_HEADS):
            kc = kn_scr[pl.ds(start, ck), n * HEAD_DIM:(n + 1) * HEAD_DIM]
            vt = vt_scr[c, n * HEAD_DIM:(n + 1) * HEAD_DIM, :]
            z_all = jnp.dot(kc, qt_scr[n], preferred_element_type=f32)
            parts = []
            slope_row = jnp.zeros((1, Q_PER_KV * Q_BLOCK), f32)
            for g in range(Q_PER_KV):
                h = n * Q_PER_KV + g
                parts.append(jnp.where(sel, z_all[:, g * Q_BLOCK:(g + 1) * Q_BLOCK] + bias_scr[h], NEG_BIG))
                slope_row = jnp.where(lane_head == g, _alibi_slope(h) * LOG2E, slope_row)
            z = jnp.concatenate(parts, axis=1)
            off = slope_row * (-tile_dist)
            m_old = m_scr[n]
            m_new = jnp.maximum(m_old, jnp.max(z, axis=0, keepdims=True) + off)
            alpha = jnp.exp2(m_old - m_new)
            p = jnp.exp2(z + (off - m_new))
            l_scr[n] = alpha * l_scr[n] + jnp.sum(p, axis=0, keepdims=True)
            acc_scr[n] = alpha * acc_scr[n] + jnp.dot(vt, p.astype(bf16), preferred_element_type=f32)
            m_scr[n] = m_new
        return carry

    lax.fori_loop(0, n_chunks, attn_chunk, 0)
    for n in range(N_KV_HEADS):
        out_t = acc_scr[n] / l_scr[n]
        for g in range(Q_PER_KV):
            h = n * Q_PER_KV + g
            o_ref[:, h * HEAD_DIM:(h + 1) * HEAD_DIM] = out_t[:, g * Q_BLOCK:(g + 1) * Q_BLOCK].T.astype(o_ref.dtype)


def _attention(proj, kiwi, q_norm, k_norm, batch, seq):
    n = batch * seq
    nq = seq // Q_BLOCK
    ck = min(KEY_CHUNK, seq)
    topk = min(IDX_TOPK_MAX, seq // 4)
    kern = functools.partial(_attn_t_kernel, seq=seq, topk=topk)
    return pl.pallas_call(
        kern,
        grid=(batch, nq),
        in_specs=[pl.BlockSpec((Q_BLOCK, ATTN_WIDTH), lambda b, i: (b * nq + i, COL_Q // ATTN_WIDTH)),
                  pl.BlockSpec((Q_BLOCK, IDX_HEADS * IDX_DIM), lambda b, i: (b * nq + i, COL_QI // (IDX_HEADS * IDX_DIM))),
                  pl.BlockSpec((seq, KV_WIDTH), lambda b, i: (b, COL_K // KV_WIDTH)),
                  pl.BlockSpec((seq, KV_WIDTH), lambda b, i: (b, COL_V // KV_WIDTH)),
                  pl.BlockSpec((seq, KIWI_WIDTH), lambda b, i: (b, 0)),
                  pl.BlockSpec((Q_BLOCK, KIWI_WIDTH), lambda b, i: (b * nq + i, 0)),
                  pl.BlockSpec((1, HEAD_DIM), lambda b, i: (0, 0)),
                  pl.BlockSpec((1, HEAD_DIM), lambda b, i: (0, 0))],
        out_specs=pl.BlockSpec((Q_BLOCK, ATTN_WIDTH), lambda b, i: (b * nq + i, 0)),
        out_shape=jax.ShapeDtypeStruct((n, ATTN_WIDTH), bf16),
        scratch_shapes=[pltpu.VMEM((seq, KV_WIDTH), bf16),
                        pltpu.VMEM((seq, IDX_DIM), bf16),
                        pltpu.VMEM((seq // ck, KV_WIDTH, ck), bf16),
                        pltpu.VMEM((seq // ck, ck, Q_BLOCK), f32),
                        pltpu.VMEM((N_KV_HEADS, HEAD_DIM, Q_PER_KV * Q_BLOCK), bf16),
                        pltpu.VMEM((IDX_HEADS // 4, IDX_DIM, 4 * Q_BLOCK), bf16),
                        pltpu.VMEM((N_HEADS, ck, Q_BLOCK), f32),
                        pltpu.VMEM((N_KV_HEADS, 1, Q_PER_KV * Q_BLOCK), f32),
                        pltpu.VMEM((N_KV_HEADS, 1, Q_PER_KV * Q_BLOCK), f32),
                        pltpu.VMEM((N_KV_HEADS, HEAD_DIM, Q_PER_KV * Q_BLOCK), f32)],
        compiler_params=_cparams(2),
        name="sparse_attn",
    )(proj, proj, proj, proj, kiwi, kiwi, q_norm, k_norm)


def _pool_kernel(u_ref, halo_ref, lin_ref, ps_ref, o_ref, scr, *, tm, per_seq):
    i = pl.program_id(0)
    first = (i % per_seq) == 0
    scr[0:POOL_HALO, :] = jnp.where(first, 0.0, halo_ref[...].astype(f32))
    scr[POOL_HALO:POOL_HALO + tm, :] = u_ref[...].astype(f32)
    t_in_seq = (i % per_seq) * tm + lax.broadcasted_iota(i32, (tm, 1), 0)
    for g, w in enumerate(POOL_WINDOWS):
        c0, c1 = g * POOL_GROUP_DIM, (g + 1) * POOL_GROUP_DIM
        cur = scr[POOL_HALO:POOL_HALO + tm, c0:c1]
        s = cur
        for j in range(1, w):
            s = s + scr[POOL_HALO - j:POOL_HALO - j + tm, c0:c1]
        count = jnp.minimum(t_in_seq + 1, w).astype(f32)
        pooled = s / count - cur
        mixed = jnp.dot(pooled.astype(bf16), lin_ref[g], preferred_element_type=f32)
        o_ref[:, c0:c1] = (mixed * ps_ref[:, c0:c1]).astype(o_ref.dtype)


def _pool(proj, pool_lin_b, pool_scale, seq):
    n = proj.shape[0]
    tm = min(512, seq)
    per_seq = seq // tm
    hb = tm // POOL_HALO
    kern = functools.partial(_pool_kernel, tm=tm, per_seq=per_seq)
    return pl.pallas_call(
        kern,
        grid=(n // tm,),
        in_specs=[pl.BlockSpec((tm, POOL_WIDTH), lambda i: (i, COL_U // POOL_WIDTH)),
                  pl.BlockSpec((POOL_HALO, POOL_WIDTH), lambda i: (jnp.maximum(i * hb - 1, 0), COL_U // POOL_WIDTH)),
                  pl.BlockSpec((len(POOL_WINDOWS), POOL_GROUP_DIM, POOL_GROUP_DIM), lambda i: (0, 0, 0)),
                  pl.BlockSpec((1, POOL_WIDTH), lambda i: (0, 0))],
        out_specs=pl.BlockSpec((tm, POOL_WIDTH), lambda i: (i, 0)),
        out_shape=jax.ShapeDtypeStruct((n, POOL_WIDTH), bf16),
        scratch_shapes=[pltpu.VMEM((POOL_HALO + tm, POOL_WIDTH), f32)],
        compiler_params=_cparams(1),
        name="pool",
    )(proj, proj, pool_lin_b, pool_scale)


def _merge_kernel(a_ref, p_ref, wa_ref, wp_ref, ga_ref, gp_ref, o_ref):
    ya = jnp.dot(a_ref[...], wa_ref[...], preferred_element_type=f32)
    yp = jnp.dot(p_ref[...], wp_ref[...], preferred_element_type=f32)
    o = _sigmoid(ga_ref[...].astype(f32)) * ya + _sigmoid(gp_ref[...].astype(f32)) * yp
    o_ref[...] = o.astype(o_ref.dtype)


def _merge(attn, pool, w_au, w_pu, proj):
    n = attn.shape[0]
    d = w_au.shape[1]
    tm, tn = min(1024, n), 1024
    ga0, gp0 = COL_GATE_A // tn, COL_GATE_P // tn
    return pl.pallas_call(
        _merge_kernel,
        grid=(n // tm, d // tn),
        in_specs=[pl.BlockSpec((tm, ATTN_WIDTH), lambda i, j: (i, 0)),
                  pl.BlockSpec((tm, POOL_WIDTH), lambda i, j: (i, 0)),
                  pl.BlockSpec((ATTN_WIDTH, tn), lambda i, j: (0, j)),
                  pl.BlockSpec((POOL_WIDTH, tn), lambda i, j: (0, j)),
                  pl.BlockSpec((tm, tn), lambda i, j: (i, ga0 + j)),
                  pl.BlockSpec((tm, tn), lambda i, j: (i, gp0 + j))],
        out_specs=pl.BlockSpec((tm, tn), lambda i, j: (i, j)),
        out_shape=jax.ShapeDtypeStruct((n, d), bf16),
        compiler_params=_cparams(2),
        name="merge",
    )(attn, pool, w_au, w_pu, proj, proj)


def _outproj_kernel(m_ref, x_ref, mod_ref, g_ref, wo_ref, wr_ref, x1_ref, h2_ref, lg_ref):
    y = jnp.dot(m_ref[...], wo_ref[...], preferred_element_type=f32)
    x1 = x_ref[...] + mod_ref[0, 2:3, :] * y
    x1_ref[...] = x1
    hn = x1 * lax.rsqrt(jnp.mean(x1 * x1, axis=-1, keepdims=True) + NORM_EPS) * g_ref[...]
    h2 = hn * (1.0 + mod_ref[0, 4:5, :]) + mod_ref[0, 3:4, :]
    _rows_to_slabs(h2_ref, h2)
    h_hi = h2.astype(bf16)
    h_lo = (h2 - h_hi.astype(f32)).astype(bf16)
    a = jnp.dot(h_hi, wr_ref[...], preferred_element_type=f32)
    b = jnp.dot(h_lo, wr_ref[:, 0:N_EXPERTS], preferred_element_type=f32)
    lg_ref[...] = a[:, 0:N_EXPERTS] + (a[:, N_EXPERTS:2 * N_EXPERTS] + b)


def _outproj(merged, x2, mod8, g_ffn, w_out_b, w_router, seq):
    n, d = x2.shape
    tm = min(512, seq)
    per_seq = seq // tm
    return pl.pallas_call(
        _outproj_kernel,
        grid=(n // tm,),
        in_specs=[pl.BlockSpec((tm, d), lambda i: (i, 0)),
                  pl.BlockSpec((tm, d), lambda i: (i, 0)),
                  pl.BlockSpec((1, 8, d), lambda i: (i // per_seq, 0, 0)),
                  pl.BlockSpec((1, d), lambda i: (0, 0)),
                  pl.BlockSpec((d, d), lambda i: (0, 0)),
                  pl.BlockSpec((d, 2 * N_EXPERTS), lambda i: (0, 0))],
        out_specs=[pl.BlockSpec((tm, d), lambda i: (i, 0)),
                   pl.BlockSpec((tm * d // LANE, LANE), lambda i: (i, 0)),
                   pl.BlockSpec((tm, N_EXPERTS), lambda i: (i, 0))],
        out_shape=[jax.ShapeDtypeStruct((n, d), f32),
                   jax.ShapeDtypeStruct((n * d // LANE, LANE), f32),
                   jax.ShapeDtypeStruct((n, N_EXPERTS), f32)],
        compiler_params=_cparams(1),
        name="outproj",
    )(merged, x2, mod8, g_ffn, w_out_b, w_router)


def _route_select(lg_ref, bias_ref):
    s = _sigmoid(lg_ref[...])
    sel = s + bias_ref[...]
    rows = [sel[EXPERTS_PER_GROUP * g:EXPERTS_PER_GROUP * (g + 1), :] for g in range(N_EXPERT_GROUPS)]
    grp = []
    for r in rows:
        m1 = jnp.max(r, axis=0, keepdims=True)
        eq = r == m1
        n_eq = jnp.sum(jnp.where(eq, 1.0, 0.0), axis=0, keepdims=True)
        m2 = jnp.max(jnp.where(eq, -jnp.inf, r), axis=0, keepdims=True)
        grp.append(m1 + jnp.where(n_eq >= 2.0, m1, m2))
    masked = []
    for g in range(N_EXPERT_GROUPS):
        rank = jnp.zeros_like(grp[g])
        for g2 in range(N_EXPERT_GROUPS):
            if g2 == g:
                continue
            beats = (grp[g2] >= grp[g]) if g2 < g else (grp[g2] > grp[g])
            rank = rank + jnp.where(beats, 1.0, 0.0)
        masked.append(jnp.where(rank < float(TOPK_GROUPS), rows[g], -jnp.inf))
    masked = jnp.concatenate(masked, axis=0)
    eidx = lax.broadcasted_iota(i32, (N_EXPERTS, 1), 0)
    rank = jnp.zeros_like(masked)
    for e2 in range(N_EXPERTS):
        row = masked[e2:e2 + 1, :]
        beats = (row > masked) | ((row == masked) & (eidx > e2))
        rank = rank + jnp.where(beats, 1.0, 0.0)
    return s, rank, rank < float(EXPERT_TOPK)


def _route_count_kernel(lg_ref, bias_ref, cnt_ref):
    @pl.when(pl.program_id(0) == 0)
    def _():
        cnt_ref[...] = jnp.zeros(cnt_ref.shape, f32)

    _, _, selected = _route_select(lg_ref, bias_ref)
    cnt_ref[...] += jnp.sum(jnp.where(selected, 1.0, 0.0), axis=1, keepdims=True)


def _route_assign_kernel(lg_ref, bias_ref, pstart_ref, gate_ref, pos_ref, run_scr):
    @pl.when(pl.program_id(0) == 0)
    def _():
        run_scr[...] = jnp.zeros(run_scr.shape, f32)

    tn = lg_ref.shape[1]
    s, rank, selected = _route_select(lg_ref, bias_ref)
    sel_f = jnp.where(selected, 1.0, 0.0)
    earlier = lax.broadcasted_iota(i32, (tn, tn), 0) < lax.broadcasted_iota(i32, (tn, tn), 1)
    prefix = jnp.dot(sel_f.astype(bf16), jnp.where(earlier, 1.0, 0.0).astype(bf16), preferred_element_type=f32)
    pos = pstart_ref[...] + run_scr[...] + prefix
    run_scr[...] += jnp.sum(sel_f, axis=1, keepdims=True)
    gate = jnp.where(selected, s, 0.0)
    gate = gate / jnp.sum(gate, axis=0, keepdims=True) * ROUTED_SCALE
    for j in range(EXPERT_TOPK):
        slot = rank == float(j)
        gate_ref[j:j + 1, :] = jnp.sum(jnp.where(slot, gate, 0.0), axis=0, keepdims=True)
        pos_ref[j:j + 1, :] = jnp.sum(jnp.where(slot, pos, 0.0), axis=0, keepdims=True).astype(i32)


def _route_counts(logits_t, bias_col):
    e, n = logits_t.shape
    tn = min(512, n)
    return pl.pallas_call(
        _route_count_kernel,
        grid=(n // tn,),
        in_specs=[pl.BlockSpec((e, tn), lambda t: (0, t)),
                  pl.BlockSpec((e, 1), lambda t: (0, 0))],
        out_specs=pl.BlockSpec((e, 1), lambda t: (0, 0)),
        out_shape=jax.ShapeDtypeStruct((e, 1), f32),
        compiler_params=_cparams(1),
        name="route_count",
    )(logits_t, bias_col)


def _route_assign(logits_t, bias_col, pstart_col):
    e, n = logits_t.shape
    tn = min(512, n)
    return pl.pallas_call(
        _route_assign_kernel,
        grid=(n // tn,),
        in_specs=[pl.BlockSpec((e, tn), lambda t: (0, t)),
                  pl.BlockSpec((e, 1), lambda t: (0, 0)),
                  pl.BlockSpec((e, 1), lambda t: (0, 0))],
        out_specs=[pl.BlockSpec((EXPERT_TOPK, tn), lambda t: (0, t)),
                   pl.BlockSpec((EXPERT_TOPK, tn), lambda t: (0, t))],
        out_shape=[jax.ShapeDtypeStruct((EXPERT_TOPK, n), f32),
                   jax.ShapeDtypeStruct((EXPERT_TOPK, n), i32)],
        scratch_shapes=[pltpu.VMEM((e, 1), f32)],
        compiler_params=_cparams(1),
        name="route_assign",
    )(logits_t, bias_col, pstart_col)


def _dispatch_kernel(pos_ref, pad_tile_ref, h_ref, hv_ref, x1_ref, mod_ref, ws1_ref, ws3_ref, ws2_ref,
                     xs_ref, base_ref, zero_scr, sem, fill_sem, *, n_tokens, tn):
    base = pl.program_id(0) * tn

    @pl.when(pl.program_id(0) == 0)
    def _zero_padded_tiles():
        zero_scr[...] = jnp.zeros(zero_scr.shape, zero_scr.dtype)

        def fill(e):
            row = pl.multiple_of(pad_tile_ref[e], EXPERT_TILE)
            return pltpu.make_async_copy(zero_scr, xs_ref.at[pl.ds(row, EXPERT_TILE)], fill_sem)

        def start(e, carry):
            @pl.when(pad_tile_ref[e] >= 0)
            def _():
                fill(e).start()
            return carry

        def wait(e, carry):
            @pl.when(pad_tile_ref[e] >= 0)
            def _():
                fill(e).wait()
            return carry

        def tail(tile):
            row = pl.multiple_of(tile * EXPERT_TILE, EXPERT_TILE)
            return pltpu.make_async_copy(zero_scr, xs_ref.at[pl.ds(row, EXPERT_TILE)], fill_sem)

        def start_tail(tile, carry):
            tail(tile).start()
            return carry

        def wait_tail(tile, carry):
            tail(tile).wait()
            return carry

        n_tiles_total = xs_ref.shape[0] // EXPERT_TILE
        lax.fori_loop(0, N_EXPERTS, start, 0)
        lax.fori_loop(pad_tile_ref[N_EXPERTS], n_tiles_total, start_tail, 0)
        lax.fori_loop(0, N_EXPERTS, wait, 0)
        lax.fori_loop(pad_tile_ref[N_EXPERTS], n_tiles_total, wait_tail, 0)

    def row_copy(t, p):
        return pltpu.make_async_copy(h_ref.at[t], xs_ref.at[p], sem)

    def issue(t, carry):
        for j in range(EXPERT_TOPK):
            row_copy(t, pos_ref[j * n_tokens + base + t]).start(priority=j % 2)
        return carry

    lax.fori_loop(0, tn, issue, 0)

    hb = _slabs_to_rows(hv_ref, tn).astype(bf16)
    act = (_silu(jnp.dot(hb, ws1_ref[...], preferred_element_type=f32))
           * jnp.dot(hb, ws3_ref[...], preferred_element_type=f32)).astype(bf16)
    shared = jnp.dot(act, ws2_ref[...], preferred_element_type=f32)
    base_ref[...] = x1_ref[...] + mod_ref[0, 5:6, :] * shared

    for j in range(EXPERT_TOPK):
        pltpu.make_async_copy(h_ref, xs_ref.at[pl.ds(0, tn)], sem).wait()


def _dispatch(pos_flat, pad_tile, h2_slabs, x1, mod8, ws1b, ws3b, ws2b, n_rows, seq):
    n, d = x1.shape
    s = d // LANE
    hdim = ws1b.shape[1]
    tn = min(256, seq)
    per_seq = seq // tn
    kern = functools.partial(_dispatch_kernel, n_tokens=n, tn=tn)
    return pl.pallas_call(
        kern,
        grid=(n // tn,),
        in_specs=[pl.BlockSpec(memory_space=pltpu.SMEM),
                  pl.BlockSpec(memory_space=pltpu.SMEM),
                  pl.BlockSpec((tn, s, LANE), lambda i: (i, 0, 0)),
                  pl.BlockSpec((tn * s, LANE), lambda i: (i, 0)),
                  pl.BlockSpec((tn, d), lambda i: (i, 0)),
                  pl.BlockSpec((1, 8, d), lambda i: (i // per_seq, 0, 0)),
                  pl.BlockSpec((d, hdim), lambda i: (0, 0)),
                  pl.BlockSpec((d, hdim), lambda i: (0, 0)),
                  pl.BlockSpec((hdim, d), lambda i: (0, 0))],
        out_specs=[pl.BlockSpec(memory_space=pl.ANY),
                   pl.BlockSpec((tn, d), lambda i: (i, 0))],
        out_shape=[jax.ShapeDtypeStruct((n_rows, s, LANE), f32),
                   jax.ShapeDtypeStruct((n, d), f32)],
        scratch_shapes=[pltpu.VMEM((EXPERT_TILE, s, LANE), f32), pltpu.SemaphoreType.DMA(()),
                        pltpu.SemaphoreType.DMA(())],
        compiler_params=_cparams(1),
        name="dispatch",
    )(pos_flat, pad_tile, h2_slabs.reshape(n, s, LANE), h2_slabs, x1, mod8, ws1b, ws3b, ws2b)


def _expert_kernel(ord_ref, ue_ref, nu_ref, xs_ref, w1_hbm, w3_hbm, w2_hbm, ys_ref,
                   w13f, w2f, w1b, w3b, w2b, sems):
    i = pl.program_id(0)
    k = ord_ref[i]

    def fetch(kk, slot):
        e = ue_ref[kk]
        return (pltpu.make_async_copy(w1_hbm.at[e], w13f.at[slot, 0], sems.at[slot, 0]),
                pltpu.make_async_copy(w3_hbm.at[e], w13f.at[slot, 1], sems.at[slot, 1]),
                pltpu.make_async_copy(w2_hbm.at[e], w2f.at[slot], sems.at[slot, 2]))

    @pl.when(i == 0)
    def _first_fetch():
        for cp in fetch(0, 0):
            cp.start()

    def compute_tile(cast_slot):
        t = EXPERT_TILE
        s = xs_ref.shape[0] // t
        kc = 2 * LANE
        h1 = jnp.zeros((t, w1b.shape[1]), f32)
        h3 = jnp.zeros((t, w1b.shape[1]), f32)
        for c in range(s // 2):
            rows = slice(c * kc, (c + 1) * kc)
            if cast_slot is not None:
                w1b[rows, :] = w13f[cast_slot, 0, rows, :].astype(bf16)
                w3b[rows, :] = w13f[cast_slot, 1, rows, :].astype(bf16)
            xc = jnp.concatenate([xs_ref[pl.ds(2 * c, t, stride=s), :],
                                  xs_ref[pl.ds(2 * c + 1, t, stride=s), :]], axis=1).astype(bf16)
            h1 = h1 + jnp.dot(xc, w1b[rows, :], preferred_element_type=f32)
            h3 = h3 + jnp.dot(xc, w3b[rows, :], preferred_element_type=f32)
        a = (_silu(h1) * h3).astype(bf16)
        for c in range(s // 2):
            cols = slice(c * kc, (c + 1) * kc)
            if cast_slot is not None:
                w2b[:, cols] = w2f[cast_slot, :, cols].astype(bf16)
            yc = jnp.dot(a, w2b[:, cols], preferred_element_type=f32)
            ys_ref[pl.ds(2 * c, t, stride=s), :] = yc[:, 0:LANE]
            ys_ref[pl.ds(2 * c + 1, t, stride=s), :] = yc[:, LANE:kc]

    first_of_expert = (i == 0) | (k != ord_ref[jnp.maximum(i - 1, 0)])

    @pl.when((i < nu_ref[0]) & first_of_expert)
    def _first_tile():
        slot = k % 2
        for cp in fetch(k, slot):
            cp.wait()

        @pl.when(k + 1 < nu_ref[1])
        def _prefetch():
            for cp in fetch(k + 1, 1 - slot):
                cp.start(priority=1)

        compute_tile(slot)

    @pl.when((i < nu_ref[0]) & jnp.logical_not(first_of_expert))
    def _later_tile():
        compute_tile(None)

    @pl.when(i >= nu_ref[0])
    def _unused():
        ys_ref[...] = jnp.zeros(ys_ref.shape, ys_ref.dtype)


def _experts(tile_ord, used_experts, n_used, xs, w1, w3, w2):
    _, d, hdim = w1.shape
    t = EXPERT_TILE
    tb = t * d // LANE
    p, w = xs.shape
    grid_spec = pltpu.PrefetchScalarGridSpec(
        num_scalar_prefetch=3,
        grid=(p // tb,),
        in_specs=[pl.BlockSpec((tb, w), lambda i, o, ue, nu: (jnp.minimum(i, nu[0] - 1), 0)),
                  pl.BlockSpec(memory_space=pl.ANY),
                  pl.BlockSpec(memory_space=pl.ANY),
                  pl.BlockSpec(memory_space=pl.ANY)],
        out_specs=pl.BlockSpec((tb, w), lambda i, o, ue, nu: (i, 0)),
        scratch_shapes=[pltpu.VMEM((2, 2, d, hdim), f32), pltpu.VMEM((2, hdim, d), f32),
                        pltpu.VMEM((d, hdim), bf16), pltpu.VMEM((d, hdim), bf16), pltpu.VMEM((hdim, d), bf16),
                        pltpu.SemaphoreType.DMA((2, 3))],
    )
    return pl.pallas_call(
        _expert_kernel,
        grid_spec=grid_spec,
        out_shape=jax.ShapeDtypeStruct((p, w), f32),
        compiler_params=_cparams(1),
        name="experts",
    )(tile_ord, used_experts, n_used, xs, w1, w3, w2)


def _combine_kernel(pos_ref, gate_ref, base_ref, mod_ref, ys_ref, o_ref, gbuf, r_scr, sems, *, n_tokens, tn):
    step = pl.program_id(0)
    n_steps = pl.num_programs(0)
    slot = step % 2

    def gather_tile(s):
        base = s * tn
        sl = s % 2

        def issue(t, carry):
            for j in range(EXPERT_TOPK):
                p = pos_ref[j * n_tokens + base + t]
                pltpu.make_async_copy(ys_ref.at[p], gbuf.at[sl, j, t], sems.at[sl]).start(priority=j % 2)
            return carry

        lax.fori_loop(0, tn, issue, 0)

    @pl.when(step == 0)
    def _():
        gather_tile(step)

    @pl.when(step + 1 < n_steps)
    def _():
        gather_tile(step + 1)

    for j in range(EXPERT_TOPK):
        pltpu.make_async_copy(ys_ref.at[pl.ds(0, tn)], gbuf.at[slot, j], sems.at[slot]).wait()

    routed = gate_ref[:, 0:1, :] * gbuf[slot, 0]
    for j in range(1, EXPERT_TOPK):
        routed = routed + gate_ref[:, j:j + 1, :] * gbuf[slot, j]
    r_scr[...] = routed.reshape(r_scr.shape)
    o_ref[...] = base_ref[...] + mod_ref[0, 5:6, :] * _slabs_to_rows(r_scr, tn)


def _combine(pos_flat, gate_rep, base, mod8, ys, seq):
    n, d = base.shape
    s = d // LANE
    tn = min(128, seq)
    per_seq = seq // tn
    kern = functools.partial(_combine_kernel, n_tokens=n, tn=tn)
    return pl.pallas_call(
        kern,
        grid=(n // tn,),
        in_specs=[pl.BlockSpec(memory_space=pltpu.SMEM),
                  pl.BlockSpec((tn, EXPERT_TOPK, LANE), lambda i: (i, 0, 0)),
                  pl.BlockSpec((tn, d), lambda i: (i, 0)),
                  pl.BlockSpec((1, 8, d), lambda i: (i // per_seq, 0, 0)),
                  pl.BlockSpec(memory_space=pl.ANY)],
        out_specs=pl.BlockSpec((tn, d), lambda i: (i, 0)),
        out_shape=jax.ShapeDtypeStruct((n, d), f32),
        scratch_shapes=[pltpu.VMEM((2, EXPERT_TOPK, tn, s, LANE), f32), pltpu.VMEM((tn * s, LANE), f32),
                        pltpu.SemaphoreType.DMA((2,))],
        compiler_params=_cparams(1),
        name="combine",
    )(pos_flat, gate_rep, base, mod8, ys)


def _layer(x, c, w_ada, b_ada, g_mix, w_in, q_norm, k_norm, w_attn_up, pool_lin, pool_scale, w_pool_up,
           w_out, g_ffn, w_router, router_bias, w1, w3, w2, ws1, ws3, ws2):
    batch, seq, d = x.shape
    n = batch * seq
    x2 = x.reshape(n, d)

    offs = [0, ATTN_WIDTH, ATTN_WIDTH + KV_WIDTH, ATTN_WIDTH + 2 * KV_WIDTH]
    w_q = w_in[:, offs[0]:offs[1]]
    w_k = w_in[:, offs[1]:offs[2]]
    w_v = w_in[:, offs[2]:offs[3]]
    o_qi = offs[3]
    w_qi = w_in[:, o_qi:o_qi + IDX_HEADS * IDX_DIM]
    o_ki = o_qi + IDX_HEADS * IDX_DIM
    w_ki = w_in[:, o_ki:o_ki + IDX_DIM]
    o_wi = o_ki + IDX_DIM
    w_wi = w_in[:, o_wi:o_wi + IDX_HEADS]
    o_u = o_wi + IDX_HEADS
    w_u = w_in[:, o_u:o_u + POOL_WIDTH]
    o_g = o_u + POOL_WIDTH
    w_g = w_in[:, o_g:o_g + 2 * d]
    w_main = jnp.concatenate([w_g, w_q, w_qi, w_u, w_k, w_v], axis=1).astype(bf16)
    w_kiwi = jnp.concatenate(
        [w_ki, w_wi, jnp.zeros((d, KIWI_WIDTH - IDX_DIM - IDX_HEADS), w_in.dtype)], axis=1).astype(bf16)

    c8 = jnp.zeros((8, d), f32).at[:batch].set(c)
    mod = _ada(c8, w_ada, b_ada.reshape(1, -1))[:batch]
    mod8 = jnp.zeros((batch, 8, d), f32).at[:, :N_MOD].set(mod.reshape(batch, N_MOD, d))

    proj, kiwi = _inproj(x2, mod8, g_mix.reshape(1, d), w_main, w_kiwi, seq)
    attn = _attention(proj, kiwi, q_norm.reshape(1, -1), k_norm.reshape(1, -1), batch, seq)
    pool = _pool(proj, pool_lin.astype(bf16), pool_scale.reshape(1, -1), seq)
    merged = _merge(attn, pool, w_attn_up.astype(bf16), w_pool_up.astype(bf16), proj)
    wr_hi = w_router.astype(bf16)
    wr_lo = (w_router - wr_hi.astype(f32)).astype(bf16)
    x1, h2, logits = _outproj(merged, x2, mod8, g_ffn.reshape(1, d), w_out.astype(bf16),
                              jnp.concatenate([wr_hi, wr_lo], axis=1), seq)

    logits_t = logits.T
    bias_col = router_bias.reshape(N_EXPERTS, 1)
    counts = _route_counts(logits_t, bias_col)[:, 0].astype(i32)
    t = EXPERT_TILE
    tiles_e = (counts + t - 1) // t
    tile_end = jnp.cumsum(tiles_e)
    pstart = ((tile_end - tiles_e) * t).astype(f32).reshape(N_EXPERTS, 1)
    n_tiles = n * EXPERT_TOPK // t + N_EXPERTS
    n_used = tile_end[-1]
    tile_ids = jnp.minimum(jnp.arange(n_tiles, dtype=i32), n_used - 1)
    block_e = jnp.sum((tile_end[None, :] <= tile_ids[:, None]).astype(i32), axis=1)
    block_e = jnp.minimum(block_e, N_EXPERTS - 1)
    pad_tile = jnp.where(tiles_e > 0, (tile_end - 1) * t, -1).astype(i32)
    pad_tile = jnp.concatenate([pad_tile, n_used.reshape(1).astype(i32)])
    gate8, pos8 = _route_assign(logits_t, bias_col, pstart)
    pos_flat = pos8.reshape(-1)

    cum_used = jnp.cumsum((tiles_e > 0).astype(i32))
    slots = jnp.arange(N_EXPERTS, dtype=i32)
    used_experts = jnp.minimum(jnp.sum((cum_used[None, :] <= slots[:, None]).astype(i32), axis=1), N_EXPERTS - 1)
    tile_ord = jnp.sum(jnp.where(block_e[:, None] == slots[None, :], cum_used[None, :] - 1, 0), axis=1).astype(i32)
    n_used2 = jnp.stack([n_used, cum_used[-1]]).astype(i32)

    slabs = d // LANE
    xs, base = _dispatch(pos_flat, pad_tile, h2, x1, mod8, ws1.astype(bf16), ws3.astype(bf16), ws2.astype(bf16),
                         n_tiles * t, seq)
    ys = _experts(tile_ord, used_experts, n_used2, xs.reshape(-1, LANE), w1, w3, w2)
    gate_rep = jnp.broadcast_to(gate8.T[:, :, None], (n, EXPERT_TOPK, LANE))
    out = _combine(pos_flat, gate_rep, base, mod8, ys.reshape(-1, slabs, LANE), seq)
    return out.reshape(batch, seq, d)


def kernel(x, c, w_ada, b_ada, g_mix, w_in, q_norm, k_norm, w_attn_up, pool_lin, pool_scale, w_pool_up, w_out, g_ffn, w_router, router_bias, w1, w3, w2, ws1, ws3, ws2):
    for l in range(w_ada.shape[0]):
        x = _layer(x, c, w_ada[l], b_ada[l], g_mix[l], w_in[l], q_norm[l], k_norm[l], w_attn_up[l], pool_lin[l],
                   pool_scale[l], w_pool_up[l], w_out[l], g_ffn[l], w_router[l], router_bias[l], w1[l], w3[l],
                   w2[l], ws1[l], ws3[l], ws2[l])
    return x
```

```python
import functools

import jax
import jax.numpy as jnp
from jax import lax
from jax.experimental import pallas as pl
from jax.experimental.pallas import tpu as pltpu

f32 = jnp.float32
bf16 = jnp.bfloat16
i32 = jnp.int32

N_HEADS = 8
HEAD_DIM = 128
N_KV_HEADS = 2
Q_PER_KV = N_HEADS // N_KV_HEADS
ATTN_WIDTH = N_HEADS * HEAD_DIM
KV_WIDTH = N_KV_HEADS * HEAD_DIM
IDX_HEADS = 16
IDX_DIM = 64
IDX_TOPK_MAX = 256
Q_BLOCK = 128
LANE = 128
POOL_WINDOWS = (2, 4, 8, 16)
POOL_GROUP_DIM = 256
POOL_WIDTH = 1024
POOL_HALO = 16
N_EXPERTS = 64
N_EXPERT_GROUPS = 8
EXPERTS_PER_GROUP = 8
TOPK_GROUPS = 4
EXPERT_TOPK = 8
EXPERT_HIDDEN = 512
ROUTED_SCALE = 2.5
NORM_EPS = 1e-6
N_MOD = 6

COL_GATE_A = 0
COL_GATE_P = 2048
COL_Q = 4096
COL_QI = 5120
COL_U = 6144
COL_K = 7168
COL_V = 7424
MAIN_WIDTH = 7680
KIWI_WIDTH = 128

KEY_CHUNK = 512
ATTN_PIECE = 256
SEARCH_CHECK_FROM = 24
SEARCH_GROUP = 2
EXPERT_TILE = 256
VMEM_LIMIT = 56 * 1024 * 1024
INT_MIN = -2147483648
KEY_OF_LOWEST_FINITE = -2139095040
NEG_BIG = -1e30
LOG2E = 1.4426950408889634


def _alibi_slope(h):
    return 2.0 ** (-8.0 * (h + 1) / N_HEADS)


def _cparams(n_axes, vmem=VMEM_LIMIT):
    return pltpu.CompilerParams(dimension_semantics=("arbitrary",) * n_axes, vmem_limit_bytes=vmem)


def _sigmoid(x):
    return 1.0 / (1.0 + jnp.exp(-x))


def _silu(x):
    return x * _sigmoid(x)


def _rows_to_slabs(ref, val):
    rows, width = val.shape
    s = width // LANE
    for j in range(s):
        ref[pl.ds(j, rows, stride=s), :] = val[:, j * LANE:(j + 1) * LANE]


def _slabs_to_rows(ref, rows):
    s = ref.shape[0] // rows
    return jnp.concatenate([ref[pl.ds(j, rows, stride=s), :] for j in range(s)], axis=1)


def _ada_kernel(c_ref, w_ref, b_ref, o_ref):
    sc = _silu(c_ref[...]).astype(bf16)
    o_ref[...] = jnp.dot(sc, w_ref[...].astype(bf16), preferred_element_type=f32) + b_ref[...]


def _ada(c8, w_ada, b_ada):
    d, n = w_ada.shape
    tn = 1024
    return pl.pallas_call(
        _ada_kernel,
        grid=(n // tn,),
        in_specs=[pl.BlockSpec((8, d), lambda j: (0, 0)),
                  pl.BlockSpec((d, tn), lambda j: (0, j)),
                  pl.BlockSpec((1, tn), lambda j: (0, j))],
        out_specs=pl.BlockSpec((8, tn), lambda j: (0, j)),
        out_shape=jax.ShapeDtypeStruct((8, n), f32),
        compiler_params=_cparams(1),
        name="adaln",
    )(c8, w_ada, b_ada)


def _inproj_kernel(x_ref, mod_ref, g_ref, w_ref, wk_ref, o_ref, kiwi_ref, h_scr):
    @pl.when(pl.program_id(1) == 0)
    def _():
        x = x_ref[...]
        y = x * lax.rsqrt(jnp.mean(x * x, axis=-1, keepdims=True) + NORM_EPS) * g_ref[...]
        h = y * (1.0 + mod_ref[0, 1:2, :]) + mod_ref[0, 0:1, :]
        hb = h.astype(bf16)
        h_scr[...] = hb
        kiwi_ref[...] = jnp.dot(hb, wk_ref[...], preferred_element_type=f32)

    o_ref[...] = jnp.dot(h_scr[...], w_ref[...], preferred_element_type=f32).astype(o_ref.dtype)


def _inproj(x2, mod8, g_mix, w_main, w_kiwi, seq):
    n, d = x2.shape
    tm, tn = 1024, 1536
    tm = min(tm, seq)
    per_seq = seq // tm
    return pl.pallas_call(
        _inproj_kernel,
        grid=(n // tm, MAIN_WIDTH // tn),
        in_specs=[pl.BlockSpec((tm, d), lambda i, j: (i, 0)),
                  pl.BlockSpec((1, 8, d), lambda i, j: (i // per_seq, 0, 0)),
                  pl.BlockSpec((1, d), lambda i, j: (0, 0)),
                  pl.BlockSpec((d, tn), lambda i, j: (0, j)),
                  pl.BlockSpec((d, KIWI_WIDTH), lambda i, j: (0, 0))],
        out_specs=[pl.BlockSpec((tm, tn), lambda i, j: (i, j)),
                   pl.BlockSpec((tm, KIWI_WIDTH), lambda i, j: (i, 0))],
        out_shape=[jax.ShapeDtypeStruct((n, MAIN_WIDTH), bf16),
                   jax.ShapeDtypeStruct((n, KIWI_WIDTH), f32)],
        scratch_shapes=[pltpu.VMEM((tm, d), bf16)],
        compiler_params=_cparams(2),
        name="inproj",
    )(x2, mod8, g_mix, w_main, w_kiwi)


def _tree_sum(parts):
    while len(parts) > 1:
        parts = [parts[a] + parts[a + 1] for a in range(0, len(parts) - 1, 2)] + ([parts[-1]] if len(parts) % 2 else [])
    return parts[0]


def _attn_t_kernel(q_ref, qi_ref, k_ref, v_ref, kiwi_all_ref, kiwi_blk_ref, qn_ref, kn_ref, o_ref,
                   kn_scr, ki_scr, vt_scr, score_scr, qt_scr, qit_scr, bias_scr, m_scr, l_scr, acc_scr, *, seq, topk):
    i = pl.program_id(1)
    ck = min(KEY_CHUNK, seq)
    n_chunks = i // (ck // Q_BLOCK) + 1
    heads_per_dot = 4

    @pl.when(i == 0)
    def _prep_keys():
        for n in range(N_KV_HEADS):
            kf = k_ref[:, n * HEAD_DIM:(n + 1) * HEAD_DIM].astype(f32)
            r = lax.rsqrt(jnp.mean(kf * kf, axis=-1, keepdims=True) + NORM_EPS)
            kn_scr[:, n * HEAD_DIM:(n + 1) * HEAD_DIM] = (kf * r * kn_ref[...]).astype(bf16)
        ki_scr[...] = kiwi_all_ref[:, 0:IDX_DIM].astype(bf16)
        for c in range(seq // ck):
            vt_scr[c] = v_ref[c * ck:(c + 1) * ck, :].astype(f32).T.astype(bf16)
        rel = (lax.broadcasted_iota(i32, (ck, Q_BLOCK), 1) - lax.broadcasted_iota(i32, (ck, Q_BLOCK), 0)).astype(f32)
        for h in range(N_HEADS):
            bias_scr[h] = rel * (-_alibi_slope(h) * LOG2E)

    q_t = []
    for h in range(N_HEADS):
        qf = q_ref[:, h * HEAD_DIM:(h + 1) * HEAD_DIM].astype(f32)
        r = lax.rsqrt(jnp.mean(qf * qf, axis=-1, keepdims=True) + NORM_EPS)
        q_t.append((qf * r * qn_ref[...] * (HEAD_DIM ** -0.5 * LOG2E)).T)
    for n in range(N_KV_HEADS):
        qt_scr[n] = jnp.concatenate(q_t[n * Q_PER_KV:(n + 1) * Q_PER_KV], axis=1).astype(bf16)
    qi_t = qi_ref[...].astype(f32).T
    for a in range(IDX_HEADS // heads_per_dot):
        qit_scr[a] = jnp.concatenate(
            [qi_t[(a * heads_per_dot + b) * IDX_DIM:(a * heads_per_dot + b + 1) * IDX_DIM, :]
             for b in range(heads_per_dot)], axis=1).astype(bf16)
    wi_t = kiwi_blk_ref[...].T[IDX_DIM:IDX_DIM + IDX_HEADS, :] * (IDX_HEADS ** -0.5 * IDX_DIM ** -0.5)
    qpos = i * Q_BLOCK + lax.broadcasted_iota(i32, (1, Q_BLOCK), 1)

    def index_chunk(c, carry):
        start = pl.multiple_of(c * ck, ck)
        kc = ki_scr[pl.ds(start, ck), :]
        acc = jnp.zeros((ck, Q_BLOCK), f32)
        for a in range(IDX_HEADS // heads_per_dot):
            d = jnp.dot(kc, qit_scr[a], preferred_element_type=f32)
            for b in range(heads_per_dot):
                h = a * heads_per_dot + b
                acc = acc + jnp.maximum(d[:, b * Q_BLOCK:(b + 1) * Q_BLOCK], 0.0) * wi_t[h:h + 1, :]
        kpos = start + lax.broadcasted_iota(i32, (ck, 1), 0)
        score_scr[c] = jnp.where(kpos <= qpos, acc, -jnp.inf)
        return carry

    lax.fori_loop(0, n_chunks, index_chunk, 0)

    def key_to_float(key):
        bits = key ^ (lax.shift_right_arithmetic(key, 31) & jnp.int32(0x7FFFFFFF))
        return lax.bitcast_convert_type(bits, f32)

    def bit_step(b, state):
        t_u, kept = state
        bit = lax.shift_left(jnp.int32(1), 31 - b)
        cand_u = t_u | bit
        cand = key_to_float(cand_u ^ jnp.int32(INT_MIN))

        def count_chunk(c, cnt):
            ge = jnp.where(score_scr[c] >= cand, 1.0, 0.0)
            return cnt + _tree_sum([ge[s * 8:(s + 1) * 8, :] for s in range(ck // 8)])

        cnt = lax.fori_loop(0, n_chunks, count_chunk, jnp.zeros((8, Q_BLOCK), f32))
        total = jnp.sum(cnt, axis=0, keepdims=True)
        accept = total >= float(topk)
        return jnp.where(accept, cand_u, t_u), jnp.where(accept, total, kept)

    state = lax.fori_loop(0, SEARCH_CHECK_FROM, bit_step,
                          (jnp.zeros((1, Q_BLOCK), i32), jnp.full((1, Q_BLOCK), float(seq + 1), f32)))

    def settled(kept):
        ok = (kept == float(topk)) | (qpos < topk)
        return jnp.min(jnp.where(ok, 1.0, 0.0)) > 0.0

    def more_bits(loop):
        b, _, done = loop
        return (b < 32) & jnp.logical_not(done)

    def bit_group(loop):
        b, st, _ = loop
        st = lax.fori_loop(b, b + SEARCH_GROUP, bit_step, st)
        return b + SEARCH_GROUP, st, settled(st[1])

    _, (t_u, _), _ = lax.while_loop(more_bits, bit_group, (jnp.int32(SEARCH_CHECK_FROM), state, settled(state[1])))
    thr = key_to_float(jnp.maximum(t_u ^ jnp.int32(INT_MIN), jnp.int32(KEY_OF_LOWEST_FINITE)))

    def count_ties(c, carry):
        ge, gt = carry
        sc = score_scr[c]
        ge = ge + _tree_sum([jnp.where(sc[s * 8:(s + 1) * 8, :] >= thr, 1.0, 0.0) for s in range(ck // 8)])
        gt = gt + _tree_sum([jnp.where(sc[s * 8:(s + 1) * 8, :] > thr, 1.0, 0.0) for s in range(ck // 8)])
        return ge, gt

    zeros8 = jnp.zeros((8, Q_BLOCK), f32)
    ge8, gt8 = lax.fori_loop(0, n_chunks, count_ties, (zeros8, zeros8))
    excess = jnp.sum(ge8, axis=0, keepdims=True) > float(topk)
    need = float(topk) - jnp.sum(gt8, axis=0, keepdims=True)

    @pl.when(jnp.max(jnp.where(excess, 1.0, 0.0)) > 0.0)
    def _break_ties():
        pos_bits = (seq - 1).bit_length()

        def key_positions(c):
            return c * ck + lax.broadcasted_iota(i32, (ck, Q_BLOCK), 0)

        def pos_step(b, q):
            cand = q | lax.shift_left(jnp.int32(1), pos_bits - 1 - b)

            def count_before(c, cnt):
                hit = jnp.where((score_scr[c] == thr) & (key_positions(c) < cand), 1.0, 0.0)
                return cnt + _tree_sum([hit[s * 8:(s + 1) * 8, :] for s in range(ck // 8)])

            before = jnp.sum(lax.fori_loop(0, n_chunks, count_before, zeros8), axis=0, keepdims=True)
            return jnp.where(before < need, cand, q)

        q = lax.fori_loop(0, pos_bits, pos_step, jnp.zeros((1, Q_BLOCK), i32))
        q = jnp.where(excess, q, jnp.int32(seq))

        def demote(c, carry):
            sc = score_scr[c]
            score_scr[c] = jnp.where((sc == thr) & (key_positions(c) > q), -jnp.inf, sc)
            return carry

        lax.fori_loop(0, n_chunks, demote, 0)

    m_scr[...] = jnp.full(m_scr.shape, NEG_BIG, f32)
    l_scr[...] = jnp.zeros(l_scr.shape, f32)
    acc_scr[...] = jnp.zeros(acc_scr.shape, f32)
    lane_head = lax.broadcasted_iota(i32, (1, Q_PER_KV * Q_BLOCK), 1) // Q_BLOCK

    piece = min(ATTN_PIECE, ck)
    slope_rows = []
    for n in range(N_KV_HEADS):
        slope_row = jnp.zeros((1, Q_PER_KV * Q_BLOCK), f32)
        for g in range(Q_PER_KV):
            slope_row = jnp.where(lane_head == g, _alibi_slope(n * Q_PER_KV + g) * LOG2E, slope_row)
        slope_rows.append(slope_row)

    def attn_chunk(c, carry):
        start = pl.multiple_of(c * ck, ck)
        tile_dist = (i * Q_BLOCK - start).astype(f32)
        units = [(sub, n) for sub in range(ck // piece) for n in range(N_KV_HEADS)]

        def scores(sub, n):
            kc = kn_scr[pl.ds(pl.multiple_of(start + sub * piece, piece), piece), n * HEAD_DIM:(n + 1) * HEAD_DIM]
            return jnp.dot(kc, qt_scr[n], preferred_element_type=f32)

        ahead = 2
        z_of = {u: scores(*u) for u in units[:ahead]}
        for idx, (sub, n) in enumerate(units):
            r0, r1 = sub * piece, (sub + 1) * piece
            sel = score_scr[c, r0:r1, :] >= thr
            z_all = z_of.pop((sub, n))
            z = jnp.concatenate(
                [jnp.where(sel, z_all[:, g * Q_BLOCK:(g + 1) * Q_BLOCK] + bias_scr[n * Q_PER_KV + g, r0:r1, :],
                           NEG_BIG) for g in range(Q_PER_KV)], axis=1)
            off = slope_rows[n] * (-tile_dist)
            m_old = m_scr[n]
            m_new = jnp.maximum(m_old, jnp.max(z, axis=0, keepdims=True) + off)
            alpha = jnp.exp2(m_old - m_new)
            p = jnp.exp2(z + (off - m_new))
            l_scr[n] = alpha * l_scr[n] + jnp.sum(p, axis=0, keepdims=True)
            vt = vt_scr[c, n * HEAD_DIM:(n + 1) * HEAD_DIM, r0:r1]
            acc_scr[n] = alpha * acc_scr[n] + jnp.dot(vt, p.astype(bf16), preferred_element_type=f32)
            m_scr[n] = m_new
            if idx + ahead < len(units):
                z_of[units[idx + ahead]] = scores(*units[idx + ahead])
        return carry

    lax.fori_loop(0, n_chunks, attn_chunk, 0)
    for n in range(N_KV_HEADS):
        out_t = acc_scr[n] / l_scr[n]
        for g in range(Q_PER_KV):
            h = n * Q_PER_KV + g
            o_ref[:, h * HEAD_DIM:(h + 1) * HEAD_DIM] = out_t[:, g * Q_BLOCK:(g + 1) * Q_BLOCK].T.astype(o_ref.dtype)


def _attention(proj, kiwi, q_norm, k_norm, batch, seq):
    n = batch * seq
    nq = seq // Q_BLOCK
    ck = min(KEY_CHUNK, seq)
    topk = min(IDX_TOPK_MAX, seq // 4)
    kern = functools.partial(_attn_t_kernel, seq=seq, topk=topk)
    return pl.pallas_call(
        kern,
        grid=(batch, nq),
        in_specs=[pl.BlockSpec((Q_BLOCK, ATTN_WIDTH), lambda b, i: (b * nq + i, COL_Q // ATTN_WIDTH)),
                  pl.BlockSpec((Q_BLOCK, IDX_HEADS * IDX_DIM), lambda b, i: (b * nq + i, COL_QI // (IDX_HEADS * IDX_DIM))),
                  pl.BlockSpec((seq, KV_WIDTH), lambda b, i: (b, COL_K // KV_WIDTH)),
                  pl.BlockSpec((seq, KV_WIDTH), lambda b, i: (b, COL_V // KV_WIDTH)),
                  pl.BlockSpec((seq, KIWI_WIDTH), lambda b, i: (b, 0)),
                  pl.BlockSpec((Q_BLOCK, KIWI_WIDTH), lambda b, i: (b * nq + i, 0)),
                  pl.BlockSpec((1, HEAD_DIM), lambda b, i: (0, 0)),
                  pl.BlockSpec((1, HEAD_DIM), lambda b, i: (0, 0))],
        out_specs=pl.BlockSpec((Q_BLOCK, ATTN_WIDTH), lambda b, i: (b * nq + i, 0)),
        out_shape=jax.ShapeDtypeStruct((n, ATTN_WIDTH), bf16),
        scratch_shapes=[pltpu.VMEM((seq, KV_WIDTH), bf16),
                        pltpu.VMEM((seq, IDX_DIM), bf16),
                        pltpu.VMEM((seq // ck, KV_WIDTH, ck), bf16),
                        pltpu.VMEM((seq // ck, ck, Q_BLOCK), f32),
                        pltpu.VMEM((N_KV_HEADS, HEAD_DIM, Q_PER_KV * Q_BLOCK), bf16),
                        pltpu.VMEM((IDX_HEADS // 4, IDX_DIM, 4 * Q_BLOCK), bf16),
                        pltpu.VMEM((N_HEADS, ck, Q_BLOCK), f32),
                        pltpu.VMEM((N_KV_HEADS, 1, Q_PER_KV * Q_BLOCK), f32),
                        pltpu.VMEM((N_KV_HEADS, 1, Q_PER_KV * Q_BLOCK), f32),
                        pltpu.VMEM((N_KV_HEADS, HEAD_DIM, Q_PER_KV * Q_BLOCK), f32)],
        compiler_params=_cparams(2),
        name="sparse_attn",
    )(proj, proj, proj, proj, kiwi, kiwi, q_norm, k_norm)


def _pool_kernel(u_ref, halo_ref, lin_ref, ps_ref, o_ref, scr, *, tm, per_seq):
    i = pl.program_id(0)
    first = (i % per_seq) == 0
    scr[0:POOL_HALO, :] = jnp.where(first, 0.0, halo_ref[...].astype(f32))
    scr[POOL_HALO:POOL_HALO + tm, :] = u_ref[...].astype(f32)
    t_in_seq = (i % per_seq) * tm + lax.broadcasted_iota(i32, (tm, 1), 0)
    for g, w in enumerate(POOL_WINDOWS):
        c0, c1 = g * POOL_GROUP_DIM, (g + 1) * POOL_GROUP_DIM
        cur = scr[POOL_HALO:POOL_HALO + tm, c0:c1]
        s = cur
        for j in range(1, w):
            s = s + scr[POOL_HALO - j:POOL_HALO - j + tm, c0:c1]
        count = jnp.minimum(t_in_seq + 1, w).astype(f32)
        pooled = s / count - cur
        mixed = jnp.dot(pooled.astype(bf16), lin_ref[g], preferred_element_type=f32)
        o_ref[:, c0:c1] = (mixed * ps_ref[:, c0:c1]).astype(o_ref.dtype)


def _pool(proj, pool_lin_b, pool_scale, seq):
    n = proj.shape[0]
    tm = min(512, seq)
    per_seq = seq // tm
    hb = tm // POOL_HALO
    kern = functools.partial(_pool_kernel, tm=tm, per_seq=per_seq)
    return pl.pallas_call(
        kern,
        grid=(n // tm,),
        in_specs=[pl.BlockSpec((tm, POOL_WIDTH), lambda i: (i, COL_U // POOL_WIDTH)),
                  pl.BlockSpec((POOL_HALO, POOL_WIDTH), lambda i: (jnp.maximum(i * hb - 1, 0), COL_U // POOL_WIDTH)),
                  pl.BlockSpec((len(POOL_WINDOWS), POOL_GROUP_DIM, POOL_GROUP_DIM), lambda i: (0, 0, 0)),
                  pl.BlockSpec((1, POOL_WIDTH), lambda i: (0, 0))],
        out_specs=pl.BlockSpec((tm, POOL_WIDTH), lambda i: (i, 0)),
        out_shape=jax.ShapeDtypeStruct((n, POOL_WIDTH), bf16),
        scratch_shapes=[pltpu.VMEM((POOL_HALO + tm, POOL_WIDTH), f32)],
        compiler_params=_cparams(1),
        name="pool",
    )(proj, proj, pool_lin_b, pool_scale)


def _merge_kernel(a_ref, p_ref, wa_ref, wp_ref, ga_ref, gp_ref, o_ref):
    ya = jnp.dot(a_ref[...], wa_ref[...], preferred_element_type=f32)
    yp = jnp.dot(p_ref[...], wp_ref[...], preferred_element_type=f32)
    o = _sigmoid(ga_ref[...].astype(f32)) * ya + _sigmoid(gp_ref[...].astype(f32)) * yp
    o_ref[...] = o.astype(o_ref.dtype)


def _merge(attn, pool, w_au, w_pu, proj):
    n = attn.shape[0]
    d = w_au.shape[1]
    tm, tn = min(1024, n), 1024
    ga0, gp0 = COL_GATE_A // tn, COL_GATE_P // tn
    return pl.pallas_call(
        _merge_kernel,
        grid=(n // tm, d // tn),
        in_specs=[pl.BlockSpec((tm, ATTN_WIDTH), lambda i, j: (i, 0)),
                  pl.BlockSpec((tm, POOL_WIDTH), lambda i, j: (i, 0)),
                  pl.BlockSpec((ATTN_WIDTH, tn), lambda i, j: (0, j)),
                  pl.BlockSpec((POOL_WIDTH, tn), lambda i, j: (0, j)),
                  pl.BlockSpec((tm, tn), lambda i, j: (i, ga0 + j)),
                  pl.BlockSpec((tm, tn), lambda i, j: (i, gp0 + j))],
        out_specs=pl.BlockSpec((tm, tn), lambda i, j: (i, j)),
        out_shape=jax.ShapeDtypeStruct((n, d), bf16),
        compiler_params=_cparams(2),
        name="merge",
    )(attn, pool, w_au, w_pu, proj, proj)


def _outproj_kernel(m_ref, x_ref, mod_ref, g_ref, wo_ref, wr_ref, x1_ref, h2_ref, lg_ref):
    y = jnp.dot(m_ref[...], wo_ref[...], preferred_element_type=f32)
    x1 = x_ref[...] + mod_ref[0, 2:3, :] * y
    x1_ref[...] = x1
    hn = x1 * lax.rsqrt(jnp.mean(x1 * x1, axis=-1, keepdims=True) + NORM_EPS) * g_ref[...]
    h2 = hn * (1.0 + mod_ref[0, 4:5, :]) + mod_ref[0, 3:4, :]
    _rows_to_slabs(h2_ref, h2)
    h_hi = h2.astype(bf16)
    h_lo = (h2 - h_hi.astype(f32)).astype(bf16)
    a = jnp.dot(h_hi, wr_ref[...], preferred_element_type=f32)
    b = jnp.dot(h_lo, wr_ref[:, 0:N_EXPERTS], preferred_element_type=f32)
    lg_ref[...] = a[:, 0:N_EXPERTS] + (a[:, N_EXPERTS:2 * N_EXPERTS] + b)


def _outproj(merged, x2, mod8, g_ffn, w_out_b, w_router, seq):
    n, d = x2.shape
    tm = min(512, seq)
    per_seq = seq // tm
    return pl.pallas_call(
        _outproj_kernel,
        grid=(n // tm,),
        in_specs=[pl.BlockSpec((tm, d), lambda i: (i, 0)),
                  pl.BlockSpec((tm, d), lambda i: (i, 0)),
                  pl.BlockSpec((1, 8, d), lambda i: (i // per_seq, 0, 0)),
                  pl.BlockSpec((1, d), lambda i: (0, 0)),
                  pl.BlockSpec((d, d), lambda i: (0, 0)),
                  pl.BlockSpec((d, 2 * N_EXPERTS), lambda i: (0, 0))],
        out_specs=[pl.BlockSpec((tm, d), lambda i: (i, 0)),
                   pl.BlockSpec((tm * d // LANE, LANE), lambda i: (i, 0)),
                   pl.BlockSpec((tm, N_EXPERTS), lambda i: (i, 0))],
        out_shape=[jax.ShapeDtypeStruct((n, d), f32),
                   jax.ShapeDtypeStruct((n * d // LANE, LANE), f32),
                   jax.ShapeDtypeStruct((n, N_EXPERTS), f32)],
        compiler_params=_cparams(1),
        name="outproj",
    )(merged, x2, mod8, g_ffn, w_out_b, w_router)


def _route_select(lg_ref, bias_ref):
    s = _sigmoid(lg_ref[...])
    sel = s + bias_ref[...]
    rows = [sel[EXPERTS_PER_GROUP * g:EXPERTS_PER_GROUP * (g + 1), :] for g in range(N_EXPERT_GROUPS)]
    grp = []
    for r in rows:
        m1 = jnp.max(r, axis=0, keepdims=True)
        eq = r == m1
        n_eq = jnp.sum(jnp.where(eq, 1.0, 0.0), axis=0, keepdims=True)
        m2 = jnp.max(jnp.where(eq, -jnp.inf, r), axis=0, keepdims=True)
        grp.append(m1 + jnp.where(n_eq >= 2.0, m1, m2))
    masked = []
    for g in range(N_EXPERT_GROUPS):
        rank = jnp.zeros_like(grp[g])
        for g2 in range(N_EXPERT_GROUPS):
            if g2 == g:
                continue
            beats = (grp[g2] >= grp[g]) if g2 < g else (grp[g2] > grp[g])
            rank = rank + jnp.where(beats, 1.0, 0.0)
        masked.append(jnp.where(rank < float(TOPK_GROUPS), rows[g], -jnp.inf))
    masked = jnp.concatenate(masked, axis=0)
    eidx = lax.broadcasted_iota(i32, (N_EXPERTS, 1), 0)
    rank = jnp.zeros_like(masked)
    for e2 in range(N_EXPERTS):
        row = masked[e2:e2 + 1, :]
        beats = (row > masked) | ((row == masked) & (eidx > e2))
        rank = rank + jnp.where(beats, 1.0, 0.0)
    return s, rank, rank < float(EXPERT_TOPK)


def _route_count_kernel(lg_ref, bias_ref, cnt_ref):
    @pl.when(pl.program_id(0) == 0)
    def _():
        cnt_ref[...] = jnp.zeros(cnt_ref.shape, f32)

    _, _, selected = _route_select(lg_ref, bias_ref)
    cnt_ref[...] += jnp.sum(jnp.where(selected, 1.0, 0.0), axis=1, keepdims=True)


def _route_assign_kernel(lg_ref, bias_ref, pstart_ref, gate_ref, pos_ref, run_scr):
    @pl.when(pl.program_id(0) == 0)
    def _():
        run_scr[...] = jnp.zeros(run_scr.shape, f32)

    tn = lg_ref.shape[1]
    s, rank, selected = _route_select(lg_ref, bias_ref)
    sel_f = jnp.where(selected, 1.0, 0.0)
    earlier = lax.broadcasted_iota(i32, (tn, tn), 0) < lax.broadcasted_iota(i32, (tn, tn), 1)
    prefix = jnp.dot(sel_f.astype(bf16), jnp.where(earlier, 1.0, 0.0).astype(bf16), preferred_element_type=f32)
    pos = pstart_ref[...] + run_scr[...] + prefix
    run_scr[...] += jnp.sum(sel_f, axis=1, keepdims=True)
    gate = jnp.where(selected, s, 0.0)
    gate = gate / jnp.sum(gate, axis=0, keepdims=True) * ROUTED_SCALE
    for j in range(EXPERT_TOPK):
        slot = rank == float(j)
        gate_ref[j:j + 1, :] = jnp.sum(jnp.where(slot, gate, 0.0), axis=0, keepdims=True)
        pos_ref[j:j + 1, :] = jnp.sum(jnp.where(slot, pos, 0.0), axis=0, keepdims=True).astype(i32)


def _route_counts(logits_t, bias_col):
    e, n = logits_t.shape
    tn = min(512, n)
    return pl.pallas_call(
        _route_count_kernel,
        grid=(n // tn,),
        in_specs=[pl.BlockSpec((e, tn), lambda t: (0, t)),
                  pl.BlockSpec((e, 1), lambda t: (0, 0))],
        out_specs=pl.BlockSpec((e, 1), lambda t: (0, 0)),
        out_shape=jax.ShapeDtypeStruct((e, 1), f32),
        compiler_params=_cparams(1),
        name="route_count",
    )(logits_t, bias_col)


def _route_assign(logits_t, bias_col, pstart_col):
    e, n = logits_t.shape
    tn = min(512, n)
    return pl.pallas_call(
        _route_assign_kernel,
        grid=(n // tn,),
        in_specs=[pl.BlockSpec((e, tn), lambda t: (0, t)),
                  pl.BlockSpec((e, 1), lambda t: (0, 0)),
                  pl.BlockSpec((e, 1), lambda t: (0, 0))],
        out_specs=[pl.BlockSpec((EXPERT_TOPK, tn), lambda t: (0, t)),
                   pl.BlockSpec((EXPERT_TOPK, tn), lambda t: (0, t))],
        out_shape=[jax.ShapeDtypeStruct((EXPERT_TOPK, n), f32),
                   jax.ShapeDtypeStruct((EXPERT_TOPK, n), i32)],
        scratch_shapes=[pltpu.VMEM((e, 1), f32)],
        compiler_params=_cparams(1),
        name="route_assign",
    )(logits_t, bias_col, pstart_col)


def _dispatch_kernel(pos_ref, pad_tile_ref, h_ref, x1_ref, mod_ref, ws1_ref, ws3_ref, ws2_ref,
                     xs_ref, base_ref, zero_scr, sem, fill_sem, *, n_tokens, tn, s):
    base = pl.program_id(0) * tn
    tile_slabs = EXPERT_TILE * s

    @pl.when(pl.program_id(0) == 0)
    def _zero_padded_tiles():
        zero_scr[...] = jnp.zeros(zero_scr.shape, zero_scr.dtype)

        def fill(e):
            first = pl.multiple_of(pad_tile_ref[e] * s, tile_slabs)
            return pltpu.make_async_copy(zero_scr, xs_ref.at[pl.ds(first, tile_slabs), :], fill_sem)

        def start(e, carry):
            @pl.when(pad_tile_ref[e] >= 0)
            def _():
                fill(e).start()
            return carry

        def wait(e, carry):
            @pl.when(pad_tile_ref[e] >= 0)
            def _():
                fill(e).wait()
            return carry

        def tail(tile):
            first = pl.multiple_of(tile * tile_slabs, tile_slabs)
            return pltpu.make_async_copy(zero_scr, xs_ref.at[pl.ds(first, tile_slabs), :], fill_sem)

        def start_tail(tile, carry):
            tail(tile).start()
            return carry

        def wait_tail(tile, carry):
            tail(tile).wait()
            return carry

        n_tiles_total = xs_ref.shape[0] // tile_slabs
        lax.fori_loop(0, N_EXPERTS, start, 0)
        lax.fori_loop(pad_tile_ref[N_EXPERTS], n_tiles_total, start_tail, 0)
        lax.fori_loop(0, N_EXPERTS, wait, 0)
        lax.fori_loop(pad_tile_ref[N_EXPERTS], n_tiles_total, wait_tail, 0)

    def row_copy(t, p):
        return pltpu.make_async_copy(h_ref.at[pl.ds(pl.multiple_of(t * s, s), s), :],
                                     xs_ref.at[pl.ds(pl.multiple_of(p * s, s), s), :], sem)

    def issue(t, carry):
        for j in range(EXPERT_TOPK):
            row_copy(t, pos_ref[j * n_tokens + base + t]).start(priority=j % 2)
        return carry

    lax.fori_loop(0, tn, issue, 0)

    hb = _slabs_to_rows(h_ref, tn).astype(bf16)
    act = (_silu(jnp.dot(hb, ws1_ref[...], preferred_element_type=f32))
           * jnp.dot(hb, ws3_ref[...], preferred_element_type=f32)).astype(bf16)
    shared = jnp.dot(act, ws2_ref[...], preferred_element_type=f32)
    base_ref[...] = x1_ref[...] + mod_ref[0, 5:6, :] * shared

    for j in range(EXPERT_TOPK):
        pltpu.make_async_copy(h_ref, xs_ref.at[pl.ds(0, tn * s), :], sem).wait()


def _dispatch(pos_flat, pad_tile, h2_slabs, x1, mod8, ws1b, ws3b, ws2b, n_rows, seq):
    n, d = x1.shape
    s = d // LANE
    hdim = ws1b.shape[1]
    tn = min(256, seq)
    per_seq = seq // tn
    kern = functools.partial(_dispatch_kernel, n_tokens=n, tn=tn, s=s)
    return pl.pallas_call(
        kern,
        grid=(n // tn,),
        in_specs=[pl.BlockSpec(memory_space=pltpu.SMEM),
                  pl.BlockSpec(memory_space=pltpu.SMEM),
                  pl.BlockSpec((tn * s, LANE), lambda i: (i, 0)),
                  pl.BlockSpec((tn, d), lambda i: (i, 0)),
                  pl.BlockSpec((1, 8, d), lambda i: (i // per_seq, 0, 0)),
                  pl.BlockSpec((d, hdim), lambda i: (0, 0)),
                  pl.BlockSpec((d, hdim), lambda i: (0, 0)),
                  pl.BlockSpec((hdim, d), lambda i: (0, 0))],
        out_specs=[pl.BlockSpec(memory_space=pl.ANY),
                   pl.BlockSpec((tn, d), lambda i: (i, 0))],
        out_shape=[jax.ShapeDtypeStruct((n_rows * s, LANE), f32),
                   jax.ShapeDtypeStruct((n, d), f32)],
        scratch_shapes=[pltpu.VMEM((EXPERT_TILE * s, LANE), f32), pltpu.SemaphoreType.DMA(()),
                        pltpu.SemaphoreType.DMA(())],
        compiler_params=_cparams(1),
        name="dispatch",
    )(pos_flat, pad_tile, h2_slabs, x1, mod8, ws1b, ws3b, ws2b)


def _expert_kernel(ord_ref, ue_ref, nu_ref, xs_ref, w1_hbm, w3_hbm, w2_hbm, ys_ref,
                   w13f, w2f, w1b, w3b, w2b, sems):
    i = pl.program_id(0)
    k = ord_ref[i]

    def fetch(kk, slot):
        e = ue_ref[kk]
        return (pltpu.make_async_copy(w1_hbm.at[e], w13f.at[slot, 0], sems.at[slot, 0]),
                pltpu.make_async_copy(w3_hbm.at[e], w13f.at[slot, 1], sems.at[slot, 1]),
                pltpu.make_async_copy(w2_hbm.at[e], w2f.at[slot], sems.at[slot, 2]))

    @pl.when(i == 0)
    def _first_fetch():
        for cp in fetch(0, 0):
            cp.start()

    def compute_tile(cast_slot):
        t = EXPERT_TILE
        s = xs_ref.shape[0] // t
        kc = 2 * LANE
        h1 = jnp.zeros((t, w1b.shape[1]), f32)
        h3 = jnp.zeros((t, w1b.shape[1]), f32)
        for c in range(s // 2):
            rows = slice(c * kc, (c + 1) * kc)
            if cast_slot is not None:
                w1b[rows, :] = w13f[cast_slot, 0, rows, :].astype(bf16)
                w3b[rows, :] = w13f[cast_slot, 1, rows, :].astype(bf16)
            xc = jnp.concatenate([xs_ref[pl.ds(2 * c, t, stride=s), :],
                                  xs_ref[pl.ds(2 * c + 1, t, stride=s), :]], axis=1).astype(bf16)
            h1 = h1 + jnp.dot(xc, w1b[rows, :], preferred_element_type=f32)
            h3 = h3 + jnp.dot(xc, w3b[rows, :], preferred_element_type=f32)
        a = (_silu(h1) * h3).astype(bf16)
        for c in range(s // 2):
            cols = slice(c * kc, (c + 1) * kc)
            if cast_slot is not None:
                w2b[:, cols] = w2f[cast_slot, :, cols].astype(bf16)
            yc = jnp.dot(a, w2b[:, cols], preferred_element_type=f32)
            ys_ref[pl.ds(2 * c, t, stride=s), :] = yc[:, 0:LANE]
            ys_ref[pl.ds(2 * c + 1, t, stride=s), :] = yc[:, LANE:kc]

    first_of_expert = (i == 0) | (k != ord_ref[jnp.maximum(i - 1, 0)])

    @pl.when((i < nu_ref[0]) & first_of_expert)
    def _first_tile():
        slot = k % 2
        for cp in fetch(k, slot):
            cp.wait()

        @pl.when(k + 1 < nu_ref[1])
        def _prefetch():
            for cp in fetch(k + 1, 1 - slot):
                cp.start(priority=1)

        compute_tile(slot)

    @pl.when((i < nu_ref[0]) & jnp.logical_not(first_of_expert))
    def _later_tile():
        compute_tile(None)

    @pl.when(i >= nu_ref[0])
    def _unused():
        ys_ref[...] = jnp.zeros(ys_ref.shape, ys_ref.dtype)


def _experts(tile_ord, used_experts, n_used, xs, w1, w3, w2):
    _, d, hdim = w1.shape
    t = EXPERT_TILE
    tb = t * d // LANE
    p, w = xs.shape
    grid_spec = pltpu.PrefetchScalarGridSpec(
        num_scalar_prefetch=3,
        grid=(p // tb,),
        in_specs=[pl.BlockSpec((tb, w), lambda i, o, ue, nu: (jnp.minimum(i, nu[0] - 1), 0)),
                  pl.BlockSpec(memory_space=pl.ANY),
                  pl.BlockSpec(memory_space=pl.ANY),
                  pl.BlockSpec(memory_space=pl.ANY)],
        out_specs=pl.BlockSpec((tb, w), lambda i, o, ue, nu: (i, 0)),
        scratch_shapes=[pltpu.VMEM((2, 2, d, hdim), f32), pltpu.VMEM((2, hdim, d), f32),
                        pltpu.VMEM((d, hdim), bf16), pltpu.VMEM((d, hdim), bf16), pltpu.VMEM((hdim, d), bf16),
                        pltpu.SemaphoreType.DMA((2, 3))],
    )
    return pl.pallas_call(
        _expert_kernel,
        grid_spec=grid_spec,
        out_shape=jax.ShapeDtypeStruct((p, w), f32),
        compiler_params=_cparams(1),
        name="experts",
    )(tile_ord, used_experts, n_used, xs, w1, w3, w2)


def _combine_kernel(pos_ref, gate_ref, base_ref, mod_ref, ys_ref, o_ref, gbuf, r_scr, sems, *, n_tokens, tn):
    step = pl.program_id(0)
    n_steps = pl.num_programs(0)
    slot = step % 2

    def gather_tile(s):
        base = s * tn
        sl = s % 2

        def issue(t, carry):
            for j in range(EXPERT_TOPK):
                p = pos_ref[j * n_tokens + base + t]
                pltpu.make_async_copy(ys_ref.at[p], gbuf.at[sl, j, t], sems.at[sl]).start(priority=j % 2)
            return carry

        lax.fori_loop(0, tn, issue, 0)

    @pl.when(step == 0)
    def _():
        gather_tile(step)

    @pl.when(step + 1 < n_steps)
    def _():
        gather_tile(step + 1)

    for j in range(EXPERT_TOPK):
        pltpu.make_async_copy(ys_ref.at[pl.ds(0, tn)], gbuf.at[slot, j], sems.at[slot]).wait()

    routed = gate_ref[:, 0:1, :] * gbuf[slot, 0]
    for j in range(1, EXPERT_TOPK):
        routed = routed + gate_ref[:, j:j + 1, :] * gbuf[slot, j]
    r_scr[...] = routed.reshape(r_scr.shape)
    o_ref[...] = base_ref[...] + mod_ref[0, 5:6, :] * _slabs_to_rows(r_scr, tn)


def _combine(pos_flat, gate_rep, base, mod8, ys, seq):
    n, d = base.shape
    s = d // LANE
    tn = min(128, seq)
    per_seq = seq // tn
    kern = functools.partial(_combine_kernel, n_tokens=n, tn=tn)
    return pl.pallas_call(
        kern,
        grid=(n // tn,),
        in_specs=[pl.BlockSpec(memory_space=pltpu.SMEM),
                  pl.BlockSpec((tn, EXPERT_TOPK, LANE), lambda i: (i, 0, 0)),
                  pl.BlockSpec((tn, d), lambda i: (i, 0)),
                  pl.BlockSpec((1, 8, d), lambda i: (i // per_seq, 0, 0)),
                  pl.BlockSpec(memory_space=pl.ANY)],
        out_specs=pl.BlockSpec((tn, d), lambda i: (i, 0)),
        out_shape=jax.ShapeDtypeStruct((n, d), f32),
        scratch_shapes=[pltpu.VMEM((2, EXPERT_TOPK, tn, s, LANE), f32), pltpu.VMEM((tn * s, LANE), f32),
                        pltpu.SemaphoreType.DMA((2,))],
        compiler_params=_cparams(1),
        name="combine",
    )(pos_flat, gate_rep, base, mod8, ys)


def _layer(x, c, w_ada, b_ada, g_mix, w_in, q_norm, k_norm, w_attn_up, pool_lin, pool_scale, w_pool_up,
           w_out, g_ffn, w_router, router_bias, w1, w3, w2, ws1, ws3, ws2):
    batch, seq, d = x.shape
    n = batch * seq
    x2 = x.reshape(n, d)

    offs = [0, ATTN_WIDTH, ATTN_WIDTH + KV_WIDTH, ATTN_WIDTH + 2 * KV_WIDTH]
    w_q = w_in[:, offs[0]:offs[1]]
    w_k = w_in[:, offs[1]:offs[2]]
    w_v = w_in[:, offs[2]:offs[3]]
    o_qi = offs[3]
    w_qi = w_in[:, o_qi:o_qi + IDX_HEADS * IDX_DIM]
    o_ki = o_qi + IDX_HEADS * IDX_DIM
    w_ki = w_in[:, o_ki:o_ki + IDX_DIM]
    o_wi = o_ki + IDX_DIM
    w_wi = w_in[:, o_wi:o_wi + IDX_HEADS]
    o_u = o_wi + IDX_HEADS
    w_u = w_in[:, o_u:o_u + POOL_WIDTH]
    o_g = o_u + POOL_WIDTH
    w_g = w_in[:, o_g:o_g + 2 * d]
    w_main = jnp.concatenate([w_g, w_q, w_qi, w_u, w_k, w_v], axis=1).astype(bf16)
    w_kiwi = jnp.concatenate(
        [w_ki, w_wi, jnp.zeros((d, KIWI_WIDTH - IDX_DIM - IDX_HEADS), w_in.dtype)], axis=1).astype(bf16)

    c8 = jnp.zeros((8, d), f32).at[:batch].set(c)
    mod = _ada(c8, w_ada, b_ada.reshape(1, -1))[:batch]
    mod8 = jnp.zeros((batch, 8, d), f32).at[:, :N_MOD].set(mod.reshape(batch, N_MOD, d))

    proj, kiwi = _inproj(x2, mod8, g_mix.reshape(1, d), w_main, w_kiwi, seq)
    attn = _attention(proj, kiwi, q_norm.reshape(1, -1), k_norm.reshape(1, -1), batch, seq)
    pool = _pool(proj, pool_lin.astype(bf16), pool_scale.reshape(1, -1), seq)
    merged = _merge(attn, pool, w_attn_up.astype(bf16), w_pool_up.astype(bf16), proj)
    wr_hi = w_router.astype(bf16)
    wr_lo = (w_router - wr_hi.astype(f32)).astype(bf16)
    x1, h2, logits = _outproj(merged, x2, mod8, g_ffn.reshape(1, d), w_out.astype(bf16),
                              jnp.concatenate([wr_hi, wr_lo], axis=1), seq)

    logits_t = logits.T
    bias_col = router_bias.reshape(N_EXPERTS, 1)
    counts = _route_counts(logits_t, bias_col)[:, 0].astype(i32)
    t = EXPERT_TILE
    tiles_e = (counts + t - 1) // t
    tile_end = jnp.cumsum(tiles_e)
    pstart = ((tile_end - tiles_e) * t).astype(f32).reshape(N_EXPERTS, 1)
    n_tiles = n * EXPERT_TOPK // t + N_EXPERTS
    n_used = tile_end[-1]
    tile_ids = jnp.minimum(jnp.arange(n_tiles, dtype=i32), n_used - 1)
    block_e = jnp.sum((tile_end[None, :] <= tile_ids[:, None]).astype(i32), axis=1)
    block_e = jnp.minimum(block_e, N_EXPERTS - 1)
    pad_tile = jnp.where(tiles_e > 0, (tile_end - 1) * t, -1).astype(i32)
    pad_tile = jnp.concatenate([pad_tile, n_used.reshape(1).astype(i32)])
    gate8, pos8 = _route_assign(logits_t, bias_col, pstart)
    pos_flat = pos8.reshape(-1)

    cum_used = jnp.cumsum((tiles_e > 0).astype(i32))
    slots = jnp.arange(N_EXPERTS, dtype=i32)
    used_experts = jnp.minimum(jnp.sum((cum_used[None, :] <= slots[:, None]).astype(i32), axis=1), N_EXPERTS - 1)
    tile_ord = jnp.sum(jnp.where(block_e[:, None] == slots[None, :], cum_used[None, :] - 1, 0), axis=1).astype(i32)
    n_used2 = jnp.stack([n_used, cum_used[-1]]).astype(i32)

    slabs = d // LANE
    xs, base = _dispatch(pos_flat, pad_tile, h2, x1, mod8, ws1.astype(bf16), ws3.astype(bf16), ws2.astype(bf16),
                         n_tiles * t, seq)
    ys = _experts(tile_ord, used_experts, n_used2, xs, w1, w3, w2)
    gate_rep = jnp.broadcast_to(gate8.T[:, :, None], (n, EXPERT_TOPK, LANE))
    out = _combine(pos_flat, gate_rep, base, mod8, ys.reshape(-1, slabs, LANE), seq)
    return out.reshape(batch, seq, d)


def kernel(x, c, w_ada, b_ada, g_mix, w_in, q_norm, k_norm, w_attn_up, pool_lin, pool_scale, w_pool_up, w_out, g_ffn, w_router, router_bias, w1, w3, w2, ws1, ws3, ws2):
    for l in range(w_ada.shape[0]):
        x = _layer(x, c, w_ada[l], b_ada[l], g_mix[l], w_in[l], q_norm[l], k_norm[l], w_attn_up[l], pool_lin[l],
                   pool_scale[l], w_pool_up[l], w_out[l], g_ffn[l], w_router[l], router_bias[l], w1[l], w3[l],
                   w2[l], ws1[l], ws3[l], ws2[l])
    return x
```

```python
import functools

import jax
import jax.numpy as jnp
from jax import lax
from jax.experimental import pallas as pl
from jax.experimental.pallas import tpu as pltpu

f32 = jnp.float32
bf16 = jnp.bfloat16
i32 = jnp.int32

N_HEADS = 8
HEAD_DIM = 128
N_KV_HEADS = 2
Q_PER_KV = N_HEADS // N_KV_HEADS
ATTN_WIDTH = N_HEADS * HEAD_DIM
KV_WIDTH = N_KV_HEADS * HEAD_DIM
IDX_HEADS = 16
IDX_DIM = 64
IDX_TOPK_MAX = 256
Q_BLOCK = 128
LANE = 128
POOL_WINDOWS = (2, 4, 8, 16)
POOL_GROUP_DIM = 256
POOL_WIDTH = 1024
POOL_HALO = 16
N_EXPERTS = 64
N_EXPERT_GROUPS = 8
EXPERTS_PER_GROUP = 8
TOPK_GROUPS = 4
EXPERT_TOPK = 8
EXPERT_HIDDEN = 512
ROUTED_SCALE = 2.5
NORM_EPS = 1e-6
N_MOD = 6

COL_GATE_A = 0
COL_GATE_P = 2048
COL_Q = 4096
COL_QI = 5120
COL_U = 6144
COL_K = 7168
COL_V = 7424
MAIN_WIDTH = 7680
KIWI_WIDTH = 128

KEY_CHUNK = 512
ATTN_PIECE = 256
SEARCH_CHECK_FROM = 24
SEARCH_GROUP = 2
EXPERT_TILE = 256
EXPERT_IN_SLOTS = 3
EXPERT_OUT_SLOTS = 2
VMEM_LIMIT = 56 * 1024 * 1024
INT_MIN = -2147483648
KEY_OF_LOWEST_FINITE = -2139095040
NEG_BIG = -1e30
LOG2E = 1.4426950408889634


def _alibi_slope(h):
    return 2.0 ** (-8.0 * (h + 1) / N_HEADS)


def _cparams(n_axes, vmem=VMEM_LIMIT):
    return pltpu.CompilerParams(dimension_semantics=("arbitrary",) * n_axes, vmem_limit_bytes=vmem)


def _sigmoid(x):
    return 1.0 / (1.0 + jnp.exp(-x))


def _silu(x):
    return x * _sigmoid(x)


def _rows_to_slabs(ref, val):
    rows, width = val.shape
    s = width // LANE
    for j in range(s):
        ref[pl.ds(j, rows, stride=s), :] = val[:, j * LANE:(j + 1) * LANE]


def _slabs_to_rows(ref, rows):
    s = ref.shape[0] // rows
    return jnp.concatenate([ref[pl.ds(j, rows, stride=s), :] for j in range(s)], axis=1)


def _ada_kernel(c_ref, w_ref, b_ref, o_ref):
    sc = _silu(c_ref[...]).astype(bf16)
    o_ref[...] = jnp.dot(sc, w_ref[...].astype(bf16), preferred_element_type=f32) + b_ref[...]


def _ada(c8, w_ada, b_ada):
    d, n = w_ada.shape
    tn = 1024
    return pl.pallas_call(
        _ada_kernel,
        grid=(n // tn,),
        in_specs=[pl.BlockSpec((8, d), lambda j: (0, 0)),
                  pl.BlockSpec((d, tn), lambda j: (0, j)),
                  pl.BlockSpec((1, tn), lambda j: (0, j))],
        out_specs=pl.BlockSpec((8, tn), lambda j: (0, j)),
        out_shape=jax.ShapeDtypeStruct((8, n), f32),
        compiler_params=_cparams(1),
        name="adaln",
    )(c8, w_ada, b_ada)


def _inproj_kernel(x_ref, mod_ref, g_ref, w_ref, wk_ref, o_ref, kiwi_ref, h_scr):
    @pl.when(pl.program_id(1) == 0)
    def _():
        x = x_ref[...]
        y = x * lax.rsqrt(jnp.mean(x * x, axis=-1, keepdims=True) + NORM_EPS) * g_ref[...]
        h = y * (1.0 + mod_ref[0, 1:2, :]) + mod_ref[0, 0:1, :]
        hb = h.astype(bf16)
        h_scr[...] = hb
        kiwi_ref[...] = jnp.dot(hb, wk_ref[...], preferred_element_type=f32)

    o_ref[...] = jnp.dot(h_scr[...], w_ref[...], preferred_element_type=f32).astype(o_ref.dtype)


def _inproj(x2, mod8, g_mix, w_main, w_kiwi, seq):
    n, d = x2.shape
    tm, tn = 1024, 1536
    tm = min(tm, seq)
    per_seq = seq // tm
    return pl.pallas_call(
        _inproj_kernel,
        grid=(n // tm, MAIN_WIDTH // tn),
        in_specs=[pl.BlockSpec((tm, d), lambda i, j: (i, 0)),
                  pl.BlockSpec((1, 8, d), lambda i, j: (i // per_seq, 0, 0)),
                  pl.BlockSpec((1, d), lambda i, j: (0, 0)),
                  pl.BlockSpec((d, tn), lambda i, j: (0, j)),
                  pl.BlockSpec((d, KIWI_WIDTH), lambda i, j: (0, 0))],
        out_specs=[pl.BlockSpec((tm, tn), lambda i, j: (i, j)),
                   pl.BlockSpec((tm, KIWI_WIDTH), lambda i, j: (i, 0))],
        out_shape=[jax.ShapeDtypeStruct((n, MAIN_WIDTH), bf16),
                   jax.ShapeDtypeStruct((n, KIWI_WIDTH), f32)],
        scratch_shapes=[pltpu.VMEM((tm, d), bf16)],
        compiler_params=_cparams(2),
        name="inproj",
    )(x2, mod8, g_mix, w_main, w_kiwi)


def _tree_sum(parts):
    while len(parts) > 1:
        parts = [parts[a] + parts[a + 1] for a in range(0, len(parts) - 1, 2)] + ([parts[-1]] if len(parts) % 2 else [])
    return parts[0]


def _attn_t_kernel(q_ref, qi_ref, k_ref, v_ref, kiwi_all_ref, kiwi_blk_ref, qn_ref, kn_ref, o_ref,
                   kn_scr, ki_scr, vt_scr, score_scr, qt_scr, qit_scr, bias_scr, m_scr, l_scr, acc_scr, *, seq, topk):
    i = pl.program_id(1)
    ck = min(KEY_CHUNK, seq)
    n_chunks = i // (ck // Q_BLOCK) + 1
    heads_per_dot = 4

    @pl.when(i == 0)
    def _prep_keys():
        for n in range(N_KV_HEADS):
            kf = k_ref[:, n * HEAD_DIM:(n + 1) * HEAD_DIM].astype(f32)
            r = lax.rsqrt(jnp.mean(kf * kf, axis=-1, keepdims=True) + NORM_EPS)
            kn_scr[:, n * HEAD_DIM:(n + 1) * HEAD_DIM] = (kf * r * kn_ref[...]).astype(bf16)
        ki_scr[...] = kiwi_all_ref[:, 0:IDX_DIM].astype(bf16)
        for c in range(seq // ck):
            vt_scr[c] = v_ref[c * ck:(c + 1) * ck, :].astype(f32).T.astype(bf16)
        rel = (lax.broadcasted_iota(i32, (ck, Q_BLOCK), 1) - lax.broadcasted_iota(i32, (ck, Q_BLOCK), 0)).astype(f32)
        for h in range(N_HEADS):
            bias_scr[h] = rel * (-_alibi_slope(h) * LOG2E)

    q_t = []
    for h in range(N_HEADS):
        qf = q_ref[:, h * HEAD_DIM:(h + 1) * HEAD_DIM].astype(f32)
        r = lax.rsqrt(jnp.mean(qf * qf, axis=-1, keepdims=True) + NORM_EPS)
        q_t.append((qf * r * qn_ref[...] * (HEAD_DIM ** -0.5 * LOG2E)).T)
    for n in range(N_KV_HEADS):
        qt_scr[n] = jnp.concatenate(q_t[n * Q_PER_KV:(n + 1) * Q_PER_KV], axis=1).astype(bf16)
    qi_t = qi_ref[...].astype(f32).T
    for a in range(IDX_HEADS // heads_per_dot):
        qit_scr[a] = jnp.concatenate(
            [qi_t[(a * heads_per_dot + b) * IDX_DIM:(a * heads_per_dot + b + 1) * IDX_DIM, :]
             for b in range(heads_per_dot)], axis=1).astype(bf16)
    wi_t = kiwi_blk_ref[...].T[IDX_DIM:IDX_DIM + IDX_HEADS, :] * (IDX_HEADS ** -0.5 * IDX_DIM ** -0.5)
    qpos = i * Q_BLOCK + lax.broadcasted_iota(i32, (1, Q_BLOCK), 1)

    def index_chunk(c, carry):
        start = pl.multiple_of(c * ck, ck)
        kc = ki_scr[pl.ds(start, ck), :]
        acc = jnp.zeros((ck, Q_BLOCK), f32)
        for a in range(IDX_HEADS // heads_per_dot):
            d = jnp.dot(kc, qit_scr[a], preferred_element_type=f32)
            for b in range(heads_per_dot):
                h = a * heads_per_dot + b
                acc = acc + jnp.maximum(d[:, b * Q_BLOCK:(b + 1) * Q_BLOCK], 0.0) * wi_t[h:h + 1, :]
        kpos = start + lax.broadcasted_iota(i32, (ck, 1), 0)
        score_scr[c] = jnp.where(kpos <= qpos, acc, -jnp.inf)
        return carry

    lax.fori_loop(0, n_chunks, index_chunk, 0)

    def key_to_float(key):
        bits = key ^ (lax.shift_right_arithmetic(key, 31) & jnp.int32(0x7FFFFFFF))
        return lax.bitcast_convert_type(bits, f32)

    def bit_step(b, state):
        t_u, kept = state
        bit = lax.shift_left(jnp.int32(1), 31 - b)
        cand_u = t_u | bit
        cand = key_to_float(cand_u ^ jnp.int32(INT_MIN))

        def count_chunk(c, cnt):
            ge = jnp.where(score_scr[c] >= cand, 1.0, 0.0)
            return cnt + _tree_sum([ge[s * 8:(s + 1) * 8, :] for s in range(ck // 8)])

        cnt = lax.fori_loop(0, n_chunks, count_chunk, jnp.zeros((8, Q_BLOCK), f32))
        total = jnp.sum(cnt, axis=0, keepdims=True)
        accept = total >= float(topk)
        return jnp.where(accept, cand_u, t_u), jnp.where(accept, total, kept)

    state = lax.fori_loop(0, SEARCH_CHECK_FROM, bit_step,
                          (jnp.zeros((1, Q_BLOCK), i32), jnp.full((1, Q_BLOCK), float(seq + 1), f32)))

    def settled(kept):
        ok = (kept == float(topk)) | (qpos < topk)
        return jnp.min(jnp.where(ok, 1.0, 0.0)) > 0.0

    def more_bits(loop):
        b, _, done = loop
        return (b < 32) & jnp.logical_not(done)

    def bit_group(loop):
        b, st, _ = loop
        st = lax.fori_loop(b, b + SEARCH_GROUP, bit_step, st)
        return b + SEARCH_GROUP, st, settled(st[1])

    _, (t_u, _), _ = lax.while_loop(more_bits, bit_group, (jnp.int32(SEARCH_CHECK_FROM), state, settled(state[1])))
    thr = key_to_float(jnp.maximum(t_u ^ jnp.int32(INT_MIN), jnp.int32(KEY_OF_LOWEST_FINITE)))

    def count_ties(c, carry):
        ge, gt = carry
        sc = score_scr[c]
        ge = ge + _tree_sum([jnp.where(sc[s * 8:(s + 1) * 8, :] >= thr, 1.0, 0.0) for s in range(ck // 8)])
        gt = gt + _tree_sum([jnp.where(sc[s * 8:(s + 1) * 8, :] > thr, 1.0, 0.0) for s in range(ck // 8)])
        return ge, gt

    zeros8 = jnp.zeros((8, Q_BLOCK), f32)
    ge8, gt8 = lax.fori_loop(0, n_chunks, count_ties, (zeros8, zeros8))
    excess = jnp.sum(ge8, axis=0, keepdims=True) > float(topk)
    need = float(topk) - jnp.sum(gt8, axis=0, keepdims=True)

    @pl.when(jnp.max(jnp.where(excess, 1.0, 0.0)) > 0.0)
    def _break_ties():
        pos_bits = (seq - 1).bit_length()

        def key_positions(c):
            return c * ck + lax.broadcasted_iota(i32, (ck, Q_BLOCK), 0)

        def pos_step(b, q):
            cand = q | lax.shift_left(jnp.int32(1), pos_bits - 1 - b)

            def count_before(c, cnt):
                hit = jnp.where((score_scr[c] == thr) & (key_positions(c) < cand), 1.0, 0.0)
                return cnt + _tree_sum([hit[s * 8:(s + 1) * 8, :] for s in range(ck // 8)])

            before = jnp.sum(lax.fori_loop(0, n_chunks, count_before, zeros8), axis=0, keepdims=True)
            return jnp.where(before < need, cand, q)

        q = lax.fori_loop(0, pos_bits, pos_step, jnp.zeros((1, Q_BLOCK), i32))
        q = jnp.where(excess, q, jnp.int32(seq))

        def demote(c, carry):
            sc = score_scr[c]
            score_scr[c] = jnp.where((sc == thr) & (key_positions(c) > q), -jnp.inf, sc)
            return carry

        lax.fori_loop(0, n_chunks, demote, 0)

    m_scr[...] = jnp.full(m_scr.shape, NEG_BIG, f32)
    l_scr[...] = jnp.zeros(l_scr.shape, f32)
    acc_scr[...] = jnp.zeros(acc_scr.shape, f32)
    lane_head = lax.broadcasted_iota(i32, (1, Q_PER_KV * Q_BLOCK), 1) // Q_BLOCK

    piece = min(ATTN_PIECE, ck)
    slope_rows = []
    for n in range(N_KV_HEADS):
        slope_row = jnp.zeros((1, Q_PER_KV * Q_BLOCK), f32)
        for g in range(Q_PER_KV):
            slope_row = jnp.where(lane_head == g, _alibi_slope(n * Q_PER_KV + g) * LOG2E, slope_row)
        slope_rows.append(slope_row)

    def attn_chunk(c, carry):
        start = pl.multiple_of(c * ck, ck)
        tile_dist = (i * Q_BLOCK - start).astype(f32)
        units = [(sub, n) for sub in range(ck // piece) for n in range(N_KV_HEADS)]

        def scores(sub, n):
            kc = kn_scr[pl.ds(pl.multiple_of(start + sub * piece, piece), piece), n * HEAD_DIM:(n + 1) * HEAD_DIM]
            return jnp.dot(kc, qt_scr[n], preferred_element_type=f32)

        ahead = 3
        z_of = {u: scores(*u) for u in units[:ahead]}
        for idx, (sub, n) in enumerate(units):
            r0, r1 = sub * piece, (sub + 1) * piece
            sel = score_scr[c, r0:r1, :] >= thr
            z_all = z_of.pop((sub, n))
            z = jnp.concatenate(
                [jnp.where(sel, z_all[:, g * Q_BLOCK:(g + 1) * Q_BLOCK] + bias_scr[n * Q_PER_KV + g, r0:r1, :],
                           NEG_BIG) for g in range(Q_PER_KV)], axis=1)
            off = slope_rows[n] * (-tile_dist)
            m_old = m_scr[n]
            m_new = jnp.maximum(m_old, jnp.max(z, axis=0, keepdims=True) + off)
            alpha = jnp.exp2(m_old - m_new)
            p = jnp.exp2(z + (off - m_new))
            l_scr[n] = alpha * l_scr[n] + jnp.sum(p, axis=0, keepdims=True)
            vt = vt_scr[c, n * HEAD_DIM:(n + 1) * HEAD_DIM, r0:r1]
            acc_scr[n] = alpha * acc_scr[n] + jnp.dot(vt, p.astype(bf16), preferred_element_type=f32)
            m_scr[n] = m_new
            if idx + ahead < len(units):
                z_of[units[idx + ahead]] = scores(*units[idx + ahead])
        return carry

    lax.fori_loop(0, n_chunks, attn_chunk, 0)
    for n in range(N_KV_HEADS):
        out_t = acc_scr[n] / l_scr[n]
        for g in range(Q_PER_KV):
            h = n * Q_PER_KV + g
            o_ref[:, h * HEAD_DIM:(h + 1) * HEAD_DIM] = out_t[:, g * Q_BLOCK:(g + 1) * Q_BLOCK].T.astype(o_ref.dtype)


def _attention(proj, kiwi, q_norm, k_norm, batch, seq):
    n = batch * seq
    nq = seq // Q_BLOCK
    ck = min(KEY_CHUNK, seq)
    topk = min(IDX_TOPK_MAX, seq // 4)
    kern = functools.partial(_attn_t_kernel, seq=seq, topk=topk)
    return pl.pallas_call(
        kern,
        grid=(batch, nq),
        in_specs=[pl.BlockSpec((Q_BLOCK, ATTN_WIDTH), lambda b, i: (b * nq + i, COL_Q // ATTN_WIDTH)),
                  pl.BlockSpec((Q_BLOCK, IDX_HEADS * IDX_DIM), lambda b, i: (b * nq + i, COL_QI // (IDX_HEADS * IDX_DIM))),
                  pl.BlockSpec((seq, KV_WIDTH), lambda b, i: (b, COL_K // KV_WIDTH)),
                  pl.BlockSpec((seq, KV_WIDTH), lambda b, i: (b, COL_V // KV_WIDTH)),
                  pl.BlockSpec((seq, KIWI_WIDTH), lambda b, i: (b, 0)),
                  pl.BlockSpec((Q_BLOCK, KIWI_WIDTH), lambda b, i: (b * nq + i, 0)),
                  pl.BlockSpec((1, HEAD_DIM), lambda b, i: (0, 0)),
                  pl.BlockSpec((1, HEAD_DIM), lambda b, i: (0, 0))],
        out_specs=pl.BlockSpec((Q_BLOCK, ATTN_WIDTH), lambda b, i: (b * nq + i, 0)),
        out_shape=jax.ShapeDtypeStruct((n, ATTN_WIDTH), bf16),
        scratch_shapes=[pltpu.VMEM((seq, KV_WIDTH), bf16),
                        pltpu.VMEM((seq, IDX_DIM), bf16),
                        pltpu.VMEM((seq // ck, KV_WIDTH, ck), bf16),
                        pltpu.VMEM((seq // ck, ck, Q_BLOCK), f32),
                        pltpu.VMEM((N_KV_HEADS, HEAD_DIM, Q_PER_KV * Q_BLOCK), bf16),
                        pltpu.VMEM((IDX_HEADS // 4, IDX_DIM, 4 * Q_BLOCK), bf16),
                        pltpu.VMEM((N_HEADS, ck, Q_BLOCK), f32),
                        pltpu.VMEM((N_KV_HEADS, 1, Q_PER_KV * Q_BLOCK), f32),
                        pltpu.VMEM((N_KV_HEADS, 1, Q_PER_KV * Q_BLOCK), f32),
                        pltpu.VMEM((N_KV_HEADS, HEAD_DIM, Q_PER_KV * Q_BLOCK), f32)],
        compiler_params=_cparams(2),
        name="sparse_attn",
    )(proj, proj, proj, proj, kiwi, kiwi, q_norm, k_norm)


def _pool_kernel(u_ref, halo_ref, lin_ref, ps_ref, o_ref, scr, *, tm, per_seq):
    i = pl.program_id(0)
    first = (i % per_seq) == 0
    scr[0:POOL_HALO, :] = jnp.where(first, 0.0, halo_ref[...].astype(f32))
    scr[POOL_HALO:POOL_HALO + tm, :] = u_ref[...].astype(f32)
    t_in_seq = (i % per_seq) * tm + lax.broadcasted_iota(i32, (tm, 1), 0)
    for g, w in enumerate(POOL_WINDOWS):
        c0, c1 = g * POOL_GROUP_DIM, (g + 1) * POOL_GROUP_DIM
        cur = scr[POOL_HALO:POOL_HALO + tm, c0:c1]
        s = cur
        for j in range(1, w):
            s = s + scr[POOL_HALO - j:POOL_HALO - j + tm, c0:c1]
        count = jnp.minimum(t_in_seq + 1, w).astype(f32)
        pooled = s / count - cur
        mixed = jnp.dot(pooled.astype(bf16), lin_ref[g], preferred_element_type=f32)
        o_ref[:, c0:c1] = (mixed * ps_ref[:, c0:c1]).astype(o_ref.dtype)


def _pool(proj, pool_lin_b, pool_scale, seq):
    n = proj.shape[0]
    tm = min(512, seq)
    per_seq = seq // tm
    hb = tm // POOL_HALO
    kern = functools.partial(_pool_kernel, tm=tm, per_seq=per_seq)
    return pl.pallas_call(
        kern,
        grid=(n // tm,),
        in_specs=[pl.BlockSpec((tm, POOL_WIDTH), lambda i: (i, COL_U // POOL_WIDTH)),
                  pl.BlockSpec((POOL_HALO, POOL_WIDTH), lambda i: (jnp.maximum(i * hb - 1, 0), COL_U // POOL_WIDTH)),
                  pl.BlockSpec((len(POOL_WINDOWS), POOL_GROUP_DIM, POOL_GROUP_DIM), lambda i: (0, 0, 0)),
                  pl.BlockSpec((1, POOL_WIDTH), lambda i: (0, 0))],
        out_specs=pl.BlockSpec((tm, POOL_WIDTH), lambda i: (i, 0)),
        out_shape=jax.ShapeDtypeStruct((n, POOL_WIDTH), bf16),
        scratch_shapes=[pltpu.VMEM((POOL_HALO + tm, POOL_WIDTH), f32)],
        compiler_params=_cparams(1),
        name="pool",
    )(proj, proj, pool_lin_b, pool_scale)


def _merge_kernel(a_ref, p_ref, wa_ref, wp_ref, ga_ref, gp_ref, o_ref):
    ya = jnp.dot(a_ref[...], wa_ref[...], preferred_element_type=f32)
    yp = jnp.dot(p_ref[...], wp_ref[...], preferred_element_type=f32)
    o = _sigmoid(ga_ref[...].astype(f32)) * ya + _sigmoid(gp_ref[...].astype(f32)) * yp
    o_ref[...] = o.astype(o_ref.dtype)


def _merge(attn, pool, w_au, w_pu, proj):
    n = attn.shape[0]
    d = w_au.shape[1]
    tm, tn = min(1024, n), 1024
    ga0, gp0 = COL_GATE_A // tn, COL_GATE_P // tn
    return pl.pallas_call(
        _merge_kernel,
        grid=(n // tm, d // tn),
        in_specs=[pl.BlockSpec((tm, ATTN_WIDTH), lambda i, j: (i, 0)),
                  pl.BlockSpec((tm, POOL_WIDTH), lambda i, j: (i, 0)),
                  pl.BlockSpec((ATTN_WIDTH, tn), lambda i, j: (0, j)),
                  pl.BlockSpec((POOL_WIDTH, tn), lambda i, j: (0, j)),
                  pl.BlockSpec((tm, tn), lambda i, j: (i, ga0 + j)),
                  pl.BlockSpec((tm, tn), lambda i, j: (i, gp0 + j))],
        out_specs=pl.BlockSpec((tm, tn), lambda i, j: (i, j)),
        out_shape=jax.ShapeDtypeStruct((n, d), bf16),
        compiler_params=_cparams(2),
        name="merge",
    )(attn, pool, w_au, w_pu, proj, proj)


def _outproj_kernel(m_ref, x_ref, mod_ref, g_ref, wo_ref, wr_ref, x1_ref, h2_ref, lg_ref):
    y = jnp.dot(m_ref[...], wo_ref[...], preferred_element_type=f32)
    x1 = x_ref[...] + mod_ref[0, 2:3, :] * y
    x1_ref[...] = x1
    hn = x1 * lax.rsqrt(jnp.mean(x1 * x1, axis=-1, keepdims=True) + NORM_EPS) * g_ref[...]
    h2 = hn * (1.0 + mod_ref[0, 4:5, :]) + mod_ref[0, 3:4, :]
    _rows_to_slabs(h2_ref, h2)
    h_hi = h2.astype(bf16)
    h_lo = (h2 - h_hi.astype(f32)).astype(bf16)
    a = jnp.dot(h_hi, wr_ref[...], preferred_element_type=f32)
    b = jnp.dot(h_lo, wr_ref[:, 0:N_EXPERTS], preferred_element_type=f32)
    lg_ref[...] = a[:, 0:N_EXPERTS] + (a[:, N_EXPERTS:2 * N_EXPERTS] + b)


def _outproj(merged, x2, mod8, g_ffn, w_out_b, w_router, seq):
    n, d = x2.shape
    tm = min(512, seq)
    per_seq = seq // tm
    return pl.pallas_call(
        _outproj_kernel,
        grid=(n // tm,),
        in_specs=[pl.BlockSpec((tm, d), lambda i: (i, 0)),
                  pl.BlockSpec((tm, d), lambda i: (i, 0)),
                  pl.BlockSpec((1, 8, d), lambda i: (i // per_seq, 0, 0)),
                  pl.BlockSpec((1, d), lambda i: (0, 0)),
                  pl.BlockSpec((d, d), lambda i: (0, 0)),
                  pl.BlockSpec((d, 2 * N_EXPERTS), lambda i: (0, 0))],
        out_specs=[pl.BlockSpec((tm, d), lambda i: (i, 0)),
                   pl.BlockSpec((tm * d // LANE, LANE), lambda i: (i, 0)),
                   pl.BlockSpec((tm, N_EXPERTS), lambda i: (i, 0))],
        out_shape=[jax.ShapeDtypeStruct((n, d), f32),
                   jax.ShapeDtypeStruct((n * d // LANE, LANE), f32),
                   jax.ShapeDtypeStruct((n, N_EXPERTS), f32)],
        compiler_params=_cparams(1),
        name="outproj",
    )(merged, x2, mod8, g_ffn, w_out_b, w_router)


def _route_select(lg_ref, bias_ref):
    s = _sigmoid(lg_ref[...])
    sel = s + bias_ref[...]
    rows = [sel[EXPERTS_PER_GROUP * g:EXPERTS_PER_GROUP * (g + 1), :] for g in range(N_EXPERT_GROUPS)]
    grp = []
    for r in rows:
        m1 = jnp.max(r, axis=0, keepdims=True)
        eq = r == m1
        n_eq = jnp.sum(jnp.where(eq, 1.0, 0.0), axis=0, keepdims=True)
        m2 = jnp.max(jnp.where(eq, -jnp.inf, r), axis=0, keepdims=True)
        grp.append(m1 + jnp.where(n_eq >= 2.0, m1, m2))
    masked = []
    for g in range(N_EXPERT_GROUPS):
        rank = jnp.zeros_like(grp[g])
        for g2 in range(N_EXPERT_GROUPS):
            if g2 == g:
                continue
            beats = (grp[g2] >= grp[g]) if g2 < g else (grp[g2] > grp[g])
            rank = rank + jnp.where(beats, 1.0, 0.0)
        masked.append(jnp.where(rank < float(TOPK_GROUPS), rows[g], -jnp.inf))
    masked = jnp.concatenate(masked, axis=0)
    eidx = lax.broadcasted_iota(i32, (N_EXPERTS, 1), 0)
    rank = jnp.zeros_like(masked)
    for e2 in range(N_EXPERTS):
        row = masked[e2:e2 + 1, :]
        beats = (row > masked) | ((row == masked) & (eidx > e2))
        rank = rank + jnp.where(beats, 1.0, 0.0)
    return s, rank, rank < float(EXPERT_TOPK)


def _route_count_kernel(lg_ref, bias_ref, cnt_ref):
    @pl.when(pl.program_id(0) == 0)
    def _():
        cnt_ref[...] = jnp.zeros(cnt_ref.shape, f32)

    _, _, selected = _route_select(lg_ref, bias_ref)
    cnt_ref[...] += jnp.sum(jnp.where(selected, 1.0, 0.0), axis=1, keepdims=True)


def _route_assign_kernel(lg_ref, bias_ref, pstart_ref, gate_ref, pos_ref, run_scr):
    @pl.when(pl.program_id(0) == 0)
    def _():
        run_scr[...] = jnp.zeros(run_scr.shape, f32)

    tn = lg_ref.shape[1]
    s, rank, selected = _route_select(lg_ref, bias_ref)
    sel_f = jnp.where(selected, 1.0, 0.0)
    earlier = lax.broadcasted_iota(i32, (tn, tn), 0) < lax.broadcasted_iota(i32, (tn, tn), 1)
    prefix = jnp.dot(sel_f.astype(bf16), jnp.where(earlier, 1.0, 0.0).astype(bf16), preferred_element_type=f32)
    pos = pstart_ref[...] + run_scr[...] + prefix
    run_scr[...] += jnp.sum(sel_f, axis=1, keepdims=True)
    gate = jnp.where(selected, s, 0.0)
    gate = gate / jnp.sum(gate, axis=0, keepdims=True) * ROUTED_SCALE
    for j in range(EXPERT_TOPK):
        slot = rank == float(j)
        gate_ref[j:j + 1, :] = jnp.sum(jnp.where(slot, gate, 0.0), axis=0, keepdims=True)
        pos_ref[j:j + 1, :] = jnp.sum(jnp.where(slot, pos, 0.0), axis=0, keepdims=True).astype(i32)


def _route_counts(logits_t, bias_col):
    e, n = logits_t.shape
    tn = min(512, n)
    return pl.pallas_call(
        _route_count_kernel,
        grid=(n // tn,),
        in_specs=[pl.BlockSpec((e, tn), lambda t: (0, t)),
                  pl.BlockSpec((e, 1), lambda t: (0, 0))],
        out_specs=pl.BlockSpec((e, 1), lambda t: (0, 0)),
        out_shape=jax.ShapeDtypeStruct((e, 1), f32),
        compiler_params=_cparams(1),
        name="route_count",
    )(logits_t, bias_col)


def _route_assign(logits_t, bias_col, pstart_col):
    e, n = logits_t.shape
    tn = min(512, n)
    return pl.pallas_call(
        _route_assign_kernel,
        grid=(n // tn,),
        in_specs=[pl.BlockSpec((e, tn), lambda t: (0, t)),
                  pl.BlockSpec((e, 1), lambda t: (0, 0)),
                  pl.BlockSpec((e, 1), lambda t: (0, 0))],
        out_specs=[pl.BlockSpec((EXPERT_TOPK, tn), lambda t: (0, t)),
                   pl.BlockSpec((EXPERT_TOPK, tn), lambda t: (0, t))],
        out_shape=[jax.ShapeDtypeStruct((EXPERT_TOPK, n), f32),
                   jax.ShapeDtypeStruct((EXPERT_TOPK, n), i32)],
        scratch_shapes=[pltpu.VMEM((e, 1), f32)],
        compiler_params=_cparams(1),
        name="route_assign",
    )(logits_t, bias_col, pstart_col)


def _dispatch_kernel(pos_ref, pad_tile_ref, h_ref, x1_ref, mod_ref, ws1_ref, ws3_ref, ws2_ref,
                     xs_ref, base_ref, zero_scr, sem, fill_sem, *, n_tokens, tn, s):
    base = pl.program_id(0) * tn
    tile_slabs = EXPERT_TILE * s

    @pl.when(pl.program_id(0) == 0)
    def _zero_padded_tiles():
        zero_scr[...] = jnp.zeros(zero_scr.shape, zero_scr.dtype)

        def fill(e):
            first = pl.multiple_of(pad_tile_ref[e] * s, tile_slabs)
            return pltpu.make_async_copy(zero_scr, xs_ref.at[pl.ds(first, tile_slabs), :], fill_sem)

        def start(e, carry):
            @pl.when(pad_tile_ref[e] >= 0)
            def _():
                fill(e).start()
            return carry

        def wait(e, carry):
            @pl.when(pad_tile_ref[e] >= 0)
            def _():
                fill(e).wait()
            return carry

        def tail(tile):
            first = pl.multiple_of(tile * tile_slabs, tile_slabs)
            return pltpu.make_async_copy(zero_scr, xs_ref.at[pl.ds(first, tile_slabs), :], fill_sem)

        def start_tail(tile, carry):
            tail(tile).start()
            return carry

        def wait_tail(tile, carry):
            tail(tile).wait()
            return carry

        n_tiles_total = xs_ref.shape[0] // tile_slabs
        lax.fori_loop(0, N_EXPERTS, start, 0)
        lax.fori_loop(pad_tile_ref[N_EXPERTS], n_tiles_total, start_tail, 0)
        lax.fori_loop(0, N_EXPERTS, wait, 0)
        lax.fori_loop(pad_tile_ref[N_EXPERTS], n_tiles_total, wait_tail, 0)

    def row_copy(t, p):
        return pltpu.make_async_copy(h_ref.at[pl.ds(pl.multiple_of(t * s, s), s), :],
                                     xs_ref.at[pl.ds(pl.multiple_of(p * s, s), s), :], sem)

    def issue(t, carry):
        for j in range(EXPERT_TOPK):
            row_copy(t, pos_ref[j * n_tokens + base + t]).start(priority=j % 2)
        return carry

    lax.fori_loop(0, tn, issue, 0)

    hb = _slabs_to_rows(h_ref, tn).astype(bf16)
    act = (_silu(jnp.dot(hb, ws1_ref[...], preferred_element_type=f32))
           * jnp.dot(hb, ws3_ref[...], preferred_element_type=f32)).astype(bf16)
    shared = jnp.dot(act, ws2_ref[...], preferred_element_type=f32)
    base_ref[...] = x1_ref[...] + mod_ref[0, 5:6, :] * shared

    for j in range(EXPERT_TOPK):
        pltpu.make_async_copy(h_ref, xs_ref.at[pl.ds(0, tn * s), :], sem).wait()


def _dispatch(pos_flat, pad_tile, h2_slabs, x1, mod8, ws1b, ws3b, ws2b, n_rows, seq):
    n, d = x1.shape
    s = d // LANE
    hdim = ws1b.shape[1]
    tn = min(256, seq)
    per_seq = seq // tn
    kern = functools.partial(_dispatch_kernel, n_tokens=n, tn=tn, s=s)
    return pl.pallas_call(
        kern,
        grid=(n // tn,),
        in_specs=[pl.BlockSpec(memory_space=pltpu.SMEM),
                  pl.BlockSpec(memory_space=pltpu.SMEM),
                  pl.BlockSpec((tn * s, LANE), lambda i: (i, 0)),
                  pl.BlockSpec((tn, d), lambda i: (i, 0)),
                  pl.BlockSpec((1, 8, d), lambda i: (i // per_seq, 0, 0)),
                  pl.BlockSpec((d, hdim), lambda i: (0, 0)),
                  pl.BlockSpec((d, hdim), lambda i: (0, 0)),
                  pl.BlockSpec((hdim, d), lambda i: (0, 0))],
        out_specs=[pl.BlockSpec(memory_space=pl.ANY),
                   pl.BlockSpec((tn, d), lambda i: (i, 0))],
        out_shape=[jax.ShapeDtypeStruct((n_rows * s, LANE), f32),
                   jax.ShapeDtypeStruct((n, d), f32)],
        scratch_shapes=[pltpu.VMEM((EXPERT_TILE * s, LANE), f32), pltpu.SemaphoreType.DMA(()),
                        pltpu.SemaphoreType.DMA(())],
        compiler_params=_cparams(1),
        name="dispatch",
    )(pos_flat, pad_tile, h2_slabs, x1, mod8, ws1b, ws3b, ws2b)


def _expert_kernel(ord_ref, ue_ref, nu_ref, xs_hbm, w1_hbm, w3_hbm, w2_hbm, ys_hbm,
                   xbuf, ybuf, w13f, w2f, w1b, w3b, w2b, sems, xsems, ysems):
    i = pl.program_id(0)
    k = ord_ref[i]
    n_tiles = nu_ref[0]
    tb = xbuf.shape[1]

    def tile_rows(tile):
        return pl.ds(pl.multiple_of(tile * tb, tb), tb)

    def x_copy(tile):
        slot = tile % EXPERT_IN_SLOTS
        return pltpu.make_async_copy(xs_hbm.at[tile_rows(tile), :], xbuf.at[slot], xsems.at[slot])

    def y_copy(tile):
        slot = tile % EXPERT_OUT_SLOTS
        return pltpu.make_async_copy(ybuf.at[slot], ys_hbm.at[tile_rows(tile), :], ysems.at[slot])

    def fetch(kk, slot):
        e = ue_ref[kk]
        return (pltpu.make_async_copy(w1_hbm.at[e], w13f.at[slot, 0], sems.at[slot, 0]),
                pltpu.make_async_copy(w3_hbm.at[e], w13f.at[slot, 1], sems.at[slot, 1]),
                pltpu.make_async_copy(w2_hbm.at[e], w2f.at[slot], sems.at[slot, 2]))

    @pl.when(i == 0)
    def _prologue():
        for cp in fetch(0, 0):
            cp.start()
        for tile in range(EXPERT_IN_SLOTS - 1):
            @pl.when(tile < n_tiles)
            def _():
                x_copy(tile).start()

    def compute_tile(cast_slot):
        xs_ref = xbuf.at[i % EXPERT_IN_SLOTS]
        ys_ref = ybuf.at[i % EXPERT_OUT_SLOTS]
        t = EXPERT_TILE
        s = tb // t
        kc = 2 * LANE
        h1 = jnp.zeros((t, w1b.shape[1]), f32)
        h3 = jnp.zeros((t, w1b.shape[1]), f32)
        for c in range(s // 2):
            rows = slice(c * kc, (c + 1) * kc)
            if cast_slot is not None:
                w1b[rows, :] = w13f[cast_slot, 0, rows, :].astype(bf16)
                w3b[rows, :] = w13f[cast_slot, 1, rows, :].astype(bf16)
            xc = jnp.concatenate([xs_ref[pl.ds(2 * c, t, stride=s), :],
                                  xs_ref[pl.ds(2 * c + 1, t, stride=s), :]], axis=1).astype(bf16)
            h1 = h1 + jnp.dot(xc, w1b[rows, :], preferred_element_type=f32)
            h3 = h3 + jnp.dot(xc, w3b[rows, :], preferred_element_type=f32)
        a = (_silu(h1) * h3).astype(bf16)
        for c in range(s // 2):
            cols = slice(c * kc, (c + 1) * kc)
            if cast_slot is not None:
                w2b[:, cols] = w2f[cast_slot, :, cols].astype(bf16)
            yc = jnp.dot(a, w2b[:, cols], preferred_element_type=f32)
            ys_ref[pl.ds(2 * c, t, stride=s), :] = yc[:, 0:LANE]
            ys_ref[pl.ds(2 * c + 1, t, stride=s), :] = yc[:, LANE:kc]

    def run_tile(cast_slot):
        @pl.when(i + EXPERT_IN_SLOTS - 1 < n_tiles)
        def _():
            x_copy(i + EXPERT_IN_SLOTS - 1).start()

        x_copy(i).wait()

        @pl.when(i >= EXPERT_OUT_SLOTS)
        def _():
            y_copy(i - EXPERT_OUT_SLOTS).wait()

        compute_tile(cast_slot)
        y_copy(i).start()

        @pl.when(i == n_tiles - 1)
        def _drain():
            for back in range(EXPERT_OUT_SLOTS - 1, -1, -1):
                @pl.when(i >= back)
                def _():
                    y_copy(i - back).wait()

    first_of_expert = (i == 0) | (k != ord_ref[jnp.maximum(i - 1, 0)])

    @pl.when((i < n_tiles) & first_of_expert)
    def _first_tile():
        slot = k % 2
        for cp in fetch(k, slot):
            cp.wait()

        @pl.when(k + 1 < nu_ref[1])
        def _prefetch():
            for cp in fetch(k + 1, 1 - slot):
                cp.start(priority=1)

        run_tile(slot)

    @pl.when((i < n_tiles) & jnp.logical_not(first_of_expert))
    def _later_tile():
        run_tile(None)

    @pl.when(i >= n_tiles)
    def _unused():
        ybuf[0] = jnp.zeros(ybuf.shape[1:], ybuf.dtype)
        y_zero = pltpu.make_async_copy(ybuf.at[0], ys_hbm.at[tile_rows(i), :], ysems.at[0])
        y_zero.start()
        y_zero.wait()


def _experts(tile_ord, used_experts, n_used, xs, w1, w3, w2):
    _, d, hdim = w1.shape
    tb = EXPERT_TILE * d // LANE
    p, w = xs.shape
    grid_spec = pltpu.PrefetchScalarGridSpec(
        num_scalar_prefetch=3,
        grid=(p // tb,),
        in_specs=[pl.BlockSpec(memory_space=pl.ANY)] * 4,
        out_specs=pl.BlockSpec(memory_space=pl.ANY),
        scratch_shapes=[pltpu.VMEM((EXPERT_IN_SLOTS, tb, w), f32), pltpu.VMEM((EXPERT_OUT_SLOTS, tb, w), f32),
                        pltpu.VMEM((2, 2, d, hdim), f32), pltpu.VMEM((2, hdim, d), f32),
                        pltpu.VMEM((d, hdim), bf16), pltpu.VMEM((d, hdim), bf16), pltpu.VMEM((hdim, d), bf16),
                        pltpu.SemaphoreType.DMA((2, 3)), pltpu.SemaphoreType.DMA((EXPERT_IN_SLOTS,)),
                        pltpu.SemaphoreType.DMA((EXPERT_OUT_SLOTS,))],
    )
    return pl.pallas_call(
        _expert_kernel,
        grid_spec=grid_spec,
        out_shape=jax.ShapeDtypeStruct((p, w), f32),
        compiler_params=_cparams(1),
        name="experts",
    )(tile_ord, used_experts, n_used, xs, w1, w3, w2)


def _combine_kernel(pos_ref, gate_ref, base_ref, mod_ref, ys_ref, o_ref, gbuf, r_scr, sems, *, n_tokens, tn):
    step = pl.program_id(0)
    n_steps = pl.num_programs(0)
    slot = step % 2

    def gather_tile(s):
        base = s * tn
        sl = s % 2

        def issue(t, carry):
            for j in range(EXPERT_TOPK):
                p = pos_ref[j * n_tokens + base + t]
                pltpu.make_async_copy(ys_ref.at[p], gbuf.at[sl, j, t], sems.at[sl]).start(priority=j % 2)
            return carry

        lax.fori_loop(0, tn, issue, 0)

    @pl.when(step == 0)
    def _():
        gather_tile(step)

    @pl.when(step + 1 < n_steps)
    def _():
        gather_tile(step + 1)

    for j in range(EXPERT_TOPK):
        pltpu.make_async_copy(ys_ref.at[pl.ds(0, tn)], gbuf.at[slot, j], sems.at[slot]).wait()

    routed = gate_ref[:, 0:1, :] * gbuf[slot, 0]
    for j in range(1, EXPERT_TOPK):
        routed = routed + gate_ref[:, j:j + 1, :] * gbuf[slot, j]
    r_scr[...] = routed.reshape(r_scr.shape)
    o_ref[...] = base_ref[...] + mod_ref[0, 5:6, :] * _slabs_to_rows(r_scr, tn)


def _combine(pos_flat, gate_rep, base, mod8, ys, seq):
    n, d = base.shape
    s = d // LANE
    tn = min(128, seq)
    per_seq = seq // tn
    kern = functools.partial(_combine_kernel, n_tokens=n, tn=tn)
    return pl.pallas_call(
        kern,
        grid=(n // tn,),
        in_specs=[pl.BlockSpec(memory_space=pltpu.SMEM),
                  pl.BlockSpec((tn, EXPERT_TOPK, LANE), lambda i: (i, 0, 0)),
                  pl.BlockSpec((tn, d), lambda i: (i, 0)),
                  pl.BlockSpec((1, 8, d), lambda i: (i // per_seq, 0, 0)),
                  pl.BlockSpec(memory_space=pl.ANY)],
        out_specs=pl.BlockSpec((tn, d), lambda i: (i, 0)),
        out_shape=jax.ShapeDtypeStruct((n, d), f32),
        scratch_shapes=[pltpu.VMEM((2, EXPERT_TOPK, tn, s, LANE), f32), pltpu.VMEM((tn * s, LANE), f32),
                        pltpu.SemaphoreType.DMA((2,))],
        compiler_params=_cparams(1),
        name="combine",
    )(pos_flat, gate_rep, base, mod8, ys)


def _layer(x, c, w_ada, b_ada, g_mix, w_in, q_norm, k_norm, w_attn_up, pool_lin, pool_scale, w_pool_up,
           w_out, g_ffn, w_router, router_bias, w1, w3, w2, ws1, ws3, ws2):
    batch, seq, d = x.shape
    n = batch * seq
    x2 = x.reshape(n, d)

    offs = [0, ATTN_WIDTH, ATTN_WIDTH + KV_WIDTH, ATTN_WIDTH + 2 * KV_WIDTH]
    w_q = w_in[:, offs[0]:offs[1]]
    w_k = w_in[:, offs[1]:offs[2]]
    w_v = w_in[:, offs[2]:offs[3]]
    o_qi = offs[3]
    w_qi = w_in[:, o_qi:o_qi + IDX_HEADS * IDX_DIM]
    o_ki = o_qi + IDX_HEADS * IDX_DIM
    w_ki = w_in[:, o_ki:o_ki + IDX_DIM]
    o_wi = o_ki + IDX_DIM
    w_wi = w_in[:, o_wi:o_wi + IDX_HEADS]
    o_u = o_wi + IDX_HEADS
    w_u = w_in[:, o_u:o_u + POOL_WIDTH]
    o_g = o_u + POOL_WIDTH
    w_g = w_in[:, o_g:o_g + 2 * d]
    w_main = jnp.concatenate([w_g, w_q, w_qi, w_u, w_k, w_v], axis=1).astype(bf16)
    w_kiwi = jnp.concatenate(
        [w_ki, w_wi, jnp.zeros((d, KIWI_WIDTH - IDX_DIM - IDX_HEADS), w_in.dtype)], axis=1).astype(bf16)

    c8 = jnp.zeros((8, d), f32).at[:batch].set(c)
    mod = _ada(c8, w_ada, b_ada.reshape(1, -1))[:batch]
    mod8 = jnp.zeros((batch, 8, d), f32).at[:, :N_MOD].set(mod.reshape(batch, N_MOD, d))

    proj, kiwi = _inproj(x2, mod8, g_mix.reshape(1, d), w_main, w_kiwi, seq)
    attn = _attention(proj, kiwi, q_norm.reshape(1, -1), k_norm.reshape(1, -1), batch, seq)
    pool = _pool(proj, pool_lin.astype(bf16), pool_scale.reshape(1, -1), seq)
    merged = _merge(attn, pool, w_attn_up.astype(bf16), w_pool_up.astype(bf16), proj)
    wr_hi = w_router.astype(bf16)
    wr_lo = (w_router - wr_hi.astype(f32)).astype(bf16)
    x1, h2, logits = _outproj(merged, x2, mod8, g_ffn.reshape(1, d), w_out.astype(bf16),
                              jnp.concatenate([wr_hi, wr_lo], axis=1), seq)

    logits_t = logits.T
    bias_col = router_bias.reshape(N_EXPERTS, 1)
    counts = _route_counts(logits_t, bias_col)[:, 0].astype(i32)
    t = EXPERT_TILE
    tiles_e = (counts + t - 1) // t
    tile_end = jnp.cumsum(tiles_e)
    pstart = ((tile_end - tiles_e) * t).astype(f32).reshape(N_EXPERTS, 1)
    n_tiles = n * EXPERT_TOPK // t + N_EXPERTS
    n_used = tile_end[-1]
    tile_ids = jnp.minimum(jnp.arange(n_tiles, dtype=i32), n_used - 1)
    block_e = jnp.sum((tile_end[None, :] <= tile_ids[:, None]).astype(i32), axis=1)
    block_e = jnp.minimum(block_e, N_EXPERTS - 1)
    pad_tile = jnp.where(tiles_e > 0, (tile_end - 1) * t, -1).astype(i32)
    pad_tile = jnp.concatenate([pad_tile, n_used.reshape(1).astype(i32)])
    gate8, pos8 = _route_assign(logits_t, bias_col, pstart)
    pos_flat = pos8.reshape(-1)

    cum_used = jnp.cumsum((tiles_e > 0).astype(i32))
    slots = jnp.arange(N_EXPERTS, dtype=i32)
    used_experts = jnp.minimum(jnp.sum((cum_used[None, :] <= slots[:, None]).astype(i32), axis=1), N_EXPERTS - 1)
    tile_ord = jnp.sum(jnp.where(block_e[:, None] == slots[None, :], cum_used[None, :] - 1, 0), axis=1).astype(i32)
    n_used2 = jnp.stack([n_used, cum_used[-1]]).astype(i32)

    slabs = d // LANE
    xs, base = _dispatch(pos_flat, pad_tile, h2, x1, mod8, ws1.astype(bf16), ws3.astype(bf16), ws2.astype(bf16),
                         n_tiles * t, seq)
    ys = _experts(tile_ord, used_experts, n_used2, xs, w1, w3, w2)
    gate_rep = jnp.broadcast_to(gate8.T[:, :, None], (n, EXPERT_TOPK, LANE))
    out = _combine(pos_flat, gate_rep, base, mod8, ys.reshape(-1, slabs, LANE), seq)
    return out.reshape(batch, seq, d)


def kernel(x, c, w_ada, b_ada, g_mix, w_in, q_norm, k_norm, w_attn_up, pool_lin, pool_scale, w_pool_up, w_out, g_ffn, w_router, router_bias, w1, w3, w2, ws1, ws3, ws2):
    for l in range(w_ada.shape[0]):
        x = _layer(x, c, w_ada[l], b_ada[l], g_mix[l], w_in[l], q_norm[l], k_norm[l], w_attn_up[l], pool_lin[l],
                   pool_scale[l], w_pool_up[l], w_out[l], g_ffn[l], w_router[l], router_bias[l], w1[l], w3[l],
                   w2[l], ws1[l], ws3[l], ws2[l])
    return x
```

```python
import functools

import jax
import jax.numpy as jnp
from jax import lax
from jax.experimental import pallas as pl
from jax.experimental.pallas import tpu as pltpu

f32 = jnp.float32
bf16 = jnp.bfloat16
i32 = jnp.int32

N_HEADS = 8
HEAD_DIM = 128
N_KV_HEADS = 2
Q_PER_KV = N_HEADS // N_KV_HEADS
ATTN_WIDTH = N_HEADS * HEAD_DIM
KV_WIDTH = N_KV_HEADS * HEAD_DIM
IDX_HEADS = 16
IDX_DIM = 64
IDX_TOPK_MAX = 256
Q_BLOCK = 128
LANE = 128
POOL_WINDOWS = (2, 4, 8, 16)
POOL_GROUP_DIM = 256
POOL_WIDTH = 1024
POOL_HALO = 16
N_EXPERTS = 64
N_EXPERT_GROUPS = 8
EXPERTS_PER_GROUP = 8
TOPK_GROUPS = 4
EXPERT_TOPK = 8
EXPERT_HIDDEN = 512
ROUTED_SCALE = 2.5
NORM_EPS = 1e-6
N_MOD = 6
SUBLANES = 8
MOD_ROWS = SUBLANES

COL_GATE_A = 0
COL_GATE_P = 2048
COL_Q = 4096
COL_QI = 5120
COL_U = 6144
COL_K = 7168
COL_V = 7424
MAIN_WIDTH = 7680
KIWI_WIDTH = 128

KEY_CHUNK = 512
ATTN_PIECE = 256
SEARCH_CHECK_FROM = 24
SEARCH_GROUP = 2
EXPERT_TILE = 256
EXPERT_IN_SLOTS = 4
EXPERT_OUT_SLOTS = 3
VMEM_LIMIT = 56 * 1024 * 1024
INT_MIN = -2147483648
KEY_OF_LOWEST_FINITE = -2139095040
NEG_BIG = -1e30
LOG2E = 1.4426950408889634


def _alibi_slope(h):
    return 2.0 ** (-8.0 * (h + 1) / N_HEADS)


def _cparams(n_axes, vmem=VMEM_LIMIT):
    return pltpu.CompilerParams(dimension_semantics=("arbitrary",) * n_axes, vmem_limit_bytes=vmem)


def _sigmoid(x):
    return 1.0 / (1.0 + jnp.exp(-x))


def _silu(x):
    return x * _sigmoid(x)


def _rows_to_slabs(ref, val):
    rows, width = val.shape
    s = width // LANE
    for j in range(s):
        ref[pl.ds(j, rows, stride=s), :] = val[:, j * LANE:(j + 1) * LANE]


def _slabs_to_rows(ref, rows):
    s = ref.shape[0] // rows
    return jnp.concatenate([ref[pl.ds(j, rows, stride=s), :] for j in range(s)], axis=1)


def _ada_kernel(c_ref, w_ref, b_ref, o_ref):
    sc = _silu(c_ref[...]).astype(bf16)
    o_ref[...] = jnp.dot(sc, w_ref[...].astype(bf16), preferred_element_type=f32) + b_ref[...]


def _ada(c8, w_ada, b_ada):
    d, n = w_ada.shape
    tn = 1024
    return pl.pallas_call(
        _ada_kernel,
        grid=(n // tn,),
        in_specs=[pl.BlockSpec((SUBLANES, d), lambda j: (0, 0)),
                  pl.BlockSpec((d, tn), lambda j: (0, j)),
                  pl.BlockSpec((1, tn), lambda j: (0, j))],
        out_specs=pl.BlockSpec((SUBLANES, tn), lambda j: (0, j)),
        out_shape=jax.ShapeDtypeStruct((SUBLANES, n), f32),
        compiler_params=_cparams(1),
        name="adaln",
    )(c8, w_ada, b_ada)


def _inproj_kernel(x_ref, mod_ref, g_ref, w_ref, wk_ref, o_ref, kiwi_ref, h_scr):
    @pl.when(pl.program_id(1) == 0)
    def _():
        x = x_ref[...]
        y = x * lax.rsqrt(jnp.mean(x * x, axis=-1, keepdims=True) + NORM_EPS) * g_ref[...]
        h = y * (1.0 + mod_ref[0, 1:2, :]) + mod_ref[0, 0:1, :]
        hb = h.astype(bf16)
        h_scr[...] = hb
        kiwi_ref[...] = jnp.dot(hb, wk_ref[...], preferred_element_type=f32)

    o_ref[...] = jnp.dot(h_scr[...], w_ref[...], preferred_element_type=f32).astype(o_ref.dtype)


def _inproj(x2, mod8, g_mix, w_main, w_kiwi, seq):
    n, d = x2.shape
    tm, tn = 1024, 1536
    tm = min(tm, seq)
    per_seq = seq // tm
    return pl.pallas_call(
        _inproj_kernel,
        grid=(n // tm, MAIN_WIDTH // tn),
        in_specs=[pl.BlockSpec((tm, d), lambda i, j: (i, 0)),
                  pl.BlockSpec((1, MOD_ROWS, d), lambda i, j: (i // per_seq, 0, 0)),
                  pl.BlockSpec((1, d), lambda i, j: (0, 0)),
                  pl.BlockSpec((d, tn), lambda i, j: (0, j)),
                  pl.BlockSpec((d, KIWI_WIDTH), lambda i, j: (0, 0))],
        out_specs=[pl.BlockSpec((tm, tn), lambda i, j: (i, j)),
                   pl.BlockSpec((tm, KIWI_WIDTH), lambda i, j: (i, 0))],
        out_shape=[jax.ShapeDtypeStruct((n, MAIN_WIDTH), bf16),
                   jax.ShapeDtypeStruct((n, KIWI_WIDTH), f32)],
        scratch_shapes=[pltpu.VMEM((tm, d), bf16)],
        compiler_params=_cparams(2),
        name="inproj",
    )(x2, mod8, g_mix, w_main, w_kiwi)


def _tree_sum(parts):
    while len(parts) > 1:
        parts = [parts[a] + parts[a + 1] for a in range(0, len(parts) - 1, 2)] + ([parts[-1]] if len(parts) % 2 else [])
    return parts[0]


def _attn_t_kernel(q_ref, qi_ref, k_ref, v_ref, kiwi_all_ref, kiwi_blk_ref, qn_ref, kn_ref, o_ref,
                   kn_scr, ki_scr, vt_scr, score_scr, qt_scr, qit_scr, bias_scr, m_scr, l_scr, acc_scr, *, seq, topk):
    i = pl.program_id(1)
    ck = min(KEY_CHUNK, seq)
    n_chunks = i // (ck // Q_BLOCK) + 1
    heads_per_dot = 4

    @pl.when(i == 0)
    def _prep_keys():
        for n in range(N_KV_HEADS):
            kf = k_ref[:, n * HEAD_DIM:(n + 1) * HEAD_DIM].astype(f32)
            r = lax.rsqrt(jnp.mean(kf * kf, axis=-1, keepdims=True) + NORM_EPS)
            kn_scr[:, n * HEAD_DIM:(n + 1) * HEAD_DIM] = (kf * r * kn_ref[...]).astype(bf16)
        ki_scr[...] = kiwi_all_ref[:, 0:IDX_DIM].astype(bf16)
        for c in range(seq // ck):
            vt_scr[c] = v_ref[c * ck:(c + 1) * ck, :].astype(f32).T.astype(bf16)
        rel = (lax.broadcasted_iota(i32, (ck, Q_BLOCK), 1) - lax.broadcasted_iota(i32, (ck, Q_BLOCK), 0)).astype(f32)
        for h in range(N_HEADS):
            bias_scr[h] = rel * (-_alibi_slope(h) * LOG2E)

    q_t = []
    for h in range(N_HEADS):
        qf = q_ref[:, h * HEAD_DIM:(h + 1) * HEAD_DIM].astype(f32)
        r = lax.rsqrt(jnp.mean(qf * qf, axis=-1, keepdims=True) + NORM_EPS)
        q_t.append((qf * r * qn_ref[...] * (HEAD_DIM ** -0.5 * LOG2E)).T)
    for n in range(N_KV_HEADS):
        qt_scr[n] = jnp.concatenate(q_t[n * Q_PER_KV:(n + 1) * Q_PER_KV], axis=1).astype(bf16)
    qi_t = qi_ref[...].astype(f32).T
    for a in range(IDX_HEADS // heads_per_dot):
        qit_scr[a] = jnp.concatenate(
            [qi_t[(a * heads_per_dot + b) * IDX_DIM:(a * heads_per_dot + b + 1) * IDX_DIM, :]
             for b in range(heads_per_dot)], axis=1).astype(bf16)
    wi_t = kiwi_blk_ref[...].T[IDX_DIM:IDX_DIM + IDX_HEADS, :] * (IDX_HEADS ** -0.5 * IDX_DIM ** -0.5)
    qpos = i * Q_BLOCK + lax.broadcasted_iota(i32, (1, Q_BLOCK), 1)

    def index_chunk(c, carry):
        start = pl.multiple_of(c * ck, ck)
        kc = ki_scr[pl.ds(start, ck), :]
        acc = jnp.zeros((ck, Q_BLOCK), f32)
        for a in range(IDX_HEADS // heads_per_dot):
            d = jnp.dot(kc, qit_scr[a], preferred_element_type=f32)
            for b in range(heads_per_dot):
                h = a * heads_per_dot + b
                acc = acc + jnp.maximum(d[:, b * Q_BLOCK:(b + 1) * Q_BLOCK], 0.0) * wi_t[h:h + 1, :]
        kpos = start + lax.broadcasted_iota(i32, (ck, 1), 0)
        score_scr[c] = jnp.where(kpos <= qpos, acc, -jnp.inf)
        return carry

    lax.fori_loop(0, n_chunks, index_chunk, 0)

    def key_to_float(key):
        bits = key ^ (lax.shift_right_arithmetic(key, 31) & jnp.int32(0x7FFFFFFF))
        return lax.bitcast_convert_type(bits, f32)

    def bit_step(b, state):
        t_u, kept = state
        bit = lax.shift_left(jnp.int32(1), 31 - b)
        cand_u = t_u | bit
        cand = key_to_float(cand_u ^ jnp.int32(INT_MIN))

        def count_chunk(c, cnt):
            ge = jnp.where(score_scr[c] >= cand, 1.0, 0.0)
            return cnt + _tree_sum([ge[s * 8:(s + 1) * 8, :] for s in range(ck // 8)])

        cnt = lax.fori_loop(0, n_chunks, count_chunk, jnp.zeros((8, Q_BLOCK), f32))
        total = jnp.sum(cnt, axis=0, keepdims=True)
        accept = total >= float(topk)
        return jnp.where(accept, cand_u, t_u), jnp.where(accept, total, kept)

    state = lax.fori_loop(0, SEARCH_CHECK_FROM, bit_step,
                          (jnp.zeros((1, Q_BLOCK), i32), jnp.full((1, Q_BLOCK), float(seq + 1), f32)))

    def settled(kept):
        ok = (kept == float(topk)) | (qpos < topk)
        return jnp.min(jnp.where(ok, 1.0, 0.0)) > 0.0

    def more_bits(loop):
        b, _, done = loop
        return (b < 32) & jnp.logical_not(done)

    def bit_group(loop):
        b, st, _ = loop
        st = lax.fori_loop(b, b + SEARCH_GROUP, bit_step, st)
        return b + SEARCH_GROUP, st, settled(st[1])

    _, (t_u, _), _ = lax.while_loop(more_bits, bit_group, (jnp.int32(SEARCH_CHECK_FROM), state, settled(state[1])))
    thr = key_to_float(jnp.maximum(t_u ^ jnp.int32(INT_MIN), jnp.int32(KEY_OF_LOWEST_FINITE)))

    def count_ties(c, carry):
        ge, gt = carry
        sc = score_scr[c]
        ge = ge + _tree_sum([jnp.where(sc[s * 8:(s + 1) * 8, :] >= thr, 1.0, 0.0) for s in range(ck // 8)])
        gt = gt + _tree_sum([jnp.where(sc[s * 8:(s + 1) * 8, :] > thr, 1.0, 0.0) for s in range(ck // 8)])
        return ge, gt

    zeros8 = jnp.zeros((8, Q_BLOCK), f32)
    ge8, gt8 = lax.fori_loop(0, n_chunks, count_ties, (zeros8, zeros8))
    excess = jnp.sum(ge8, axis=0, keepdims=True) > float(topk)
    need = float(topk) - jnp.sum(gt8, axis=0, keepdims=True)

    @pl.when(jnp.max(jnp.where(excess, 1.0, 0.0)) > 0.0)
    def _break_ties():
        pos_bits = (seq - 1).bit_length()

        def key_positions(c):
            return c * ck + lax.broadcasted_iota(i32, (ck, Q_BLOCK), 0)

        def pos_step(b, q):
            cand = q | lax.shift_left(jnp.int32(1), pos_bits - 1 - b)

            def count_before(c, cnt):
                hit = jnp.where((score_scr[c] == thr) & (key_positions(c) < cand), 1.0, 0.0)
                return cnt + _tree_sum([hit[s * 8:(s + 1) * 8, :] for s in range(ck // 8)])

            before = jnp.sum(lax.fori_loop(0, n_chunks, count_before, zeros8), axis=0, keepdims=True)
            return jnp.where(before < need, cand, q)

        q = lax.fori_loop(0, pos_bits, pos_step, jnp.zeros((1, Q_BLOCK), i32))
        q = jnp.where(excess, q, jnp.int32(seq))

        def demote(c, carry):
            sc = score_scr[c]
            score_scr[c] = jnp.where((sc == thr) & (key_positions(c) > q), -jnp.inf, sc)
            return carry

        lax.fori_loop(0, n_chunks, demote, 0)

    m_scr[...] = jnp.full(m_scr.shape, NEG_BIG, f32)
    l_scr[...] = jnp.zeros(l_scr.shape, f32)
    acc_scr[...] = jnp.zeros(acc_scr.shape, f32)
    lane_head = lax.broadcasted_iota(i32, (1, Q_PER_KV * Q_BLOCK), 1) // Q_BLOCK

    piece = min(ATTN_PIECE, ck)
    slope_rows = []
    for n in range(N_KV_HEADS):
        slope_row = jnp.zeros((1, Q_PER_KV * Q_BLOCK), f32)
        for g in range(Q_PER_KV):
            slope_row = jnp.where(lane_head == g, _alibi_slope(n * Q_PER_KV + g) * LOG2E, slope_row)
        slope_rows.append(slope_row)

    def attn_chunk(c, carry):
        start = pl.multiple_of(c * ck, ck)
        tile_dist = (i * Q_BLOCK - start).astype(f32)
        units = [(sub, n) for sub in range(ck // piece) for n in range(N_KV_HEADS)]

        def scores(sub, n):
            kc = kn_scr[pl.ds(pl.multiple_of(start + sub * piece, piece), piece), n * HEAD_DIM:(n + 1) * HEAD_DIM]
            return jnp.dot(kc, qt_scr[n], preferred_element_type=f32)

        ahead = 3
        z_of = {u: scores(*u) for u in units[:ahead]}
        for idx, (sub, n) in enumerate(units):
            r0, r1 = sub * piece, (sub + 1) * piece
            sel = score_scr[c, r0:r1, :] >= thr
            z_all = z_of.pop((sub, n))
            z = jnp.concatenate(
                [jnp.where(sel, z_all[:, g * Q_BLOCK:(g + 1) * Q_BLOCK] + bias_scr[n * Q_PER_KV + g, r0:r1, :],
                           NEG_BIG) for g in range(Q_PER_KV)], axis=1)
            off = slope_rows[n] * (-tile_dist)
            m_old = m_scr[n]
            m_new = jnp.maximum(m_old, jnp.max(z, axis=0, keepdims=True) + off)
            alpha = jnp.exp2(m_old - m_new)
            p = jnp.exp2(z + (off - m_new))
            l_scr[n] = alpha * l_scr[n] + jnp.sum(p, axis=0, keepdims=True)
            vt = vt_scr[c, n * HEAD_DIM:(n + 1) * HEAD_DIM, r0:r1]
            acc_scr[n] = alpha * acc_scr[n] + jnp.dot(vt, p.astype(bf16), preferred_element_type=f32)
            m_scr[n] = m_new
            if idx + ahead < len(units):
                z_of[units[idx + ahead]] = scores(*units[idx + ahead])
        return carry

    lax.fori_loop(0, n_chunks, attn_chunk, 0)
    for n in range(N_KV_HEADS):
        out_t = acc_scr[n] / l_scr[n]
        for g in range(Q_PER_KV):
            h = n * Q_PER_KV + g
            o_ref[:, h * HEAD_DIM:(h + 1) * HEAD_DIM] = out_t[:, g * Q_BLOCK:(g + 1) * Q_BLOCK].T.astype(o_ref.dtype)


def _attention(proj, kiwi, q_norm, k_norm, batch, seq):
    n = batch * seq
    nq = seq // Q_BLOCK
    ck = min(KEY_CHUNK, seq)
    topk = min(IDX_TOPK_MAX, seq // 4)
    kern = functools.partial(_attn_t_kernel, seq=seq, topk=topk)
    return pl.pallas_call(
        kern,
        grid=(batch, nq),
        in_specs=[pl.BlockSpec((Q_BLOCK, ATTN_WIDTH), lambda b, i: (b * nq + i, COL_Q // ATTN_WIDTH)),
                  pl.BlockSpec((Q_BLOCK, IDX_HEADS * IDX_DIM), lambda b, i: (b * nq + i, COL_QI // (IDX_HEADS * IDX_DIM))),
                  pl.BlockSpec((seq, KV_WIDTH), lambda b, i: (b, COL_K // KV_WIDTH)),
                  pl.BlockSpec((seq, KV_WIDTH), lambda b, i: (b, COL_V // KV_WIDTH)),
                  pl.BlockSpec((seq, KIWI_WIDTH), lambda b, i: (b, 0)),
                  pl.BlockSpec((Q_BLOCK, KIWI_WIDTH), lambda b, i: (b * nq + i, 0)),
                  pl.BlockSpec((1, HEAD_DIM), lambda b, i: (0, 0)),
                  pl.BlockSpec((1, HEAD_DIM), lambda b, i: (0, 0))],
        out_specs=pl.BlockSpec((Q_BLOCK, ATTN_WIDTH), lambda b, i: (b * nq + i, 0)),
        out_shape=jax.ShapeDtypeStruct((n, ATTN_WIDTH), bf16),
        scratch_shapes=[pltpu.VMEM((seq, KV_WIDTH), bf16),
                        pltpu.VMEM((seq, IDX_DIM), bf16),
                        pltpu.VMEM((seq // ck, KV_WIDTH, ck), bf16),
                        pltpu.VMEM((seq // ck, ck, Q_BLOCK), f32),
                        pltpu.VMEM((N_KV_HEADS, HEAD_DIM, Q_PER_KV * Q_BLOCK), bf16),
                        pltpu.VMEM((IDX_HEADS // 4, IDX_DIM, 4 * Q_BLOCK), bf16),
                        pltpu.VMEM((N_HEADS, ck, Q_BLOCK), f32),
                        pltpu.VMEM((N_KV_HEADS, 1, Q_PER_KV * Q_BLOCK), f32),
                        pltpu.VMEM((N_KV_HEADS, 1, Q_PER_KV * Q_BLOCK), f32),
                        pltpu.VMEM((N_KV_HEADS, HEAD_DIM, Q_PER_KV * Q_BLOCK), f32)],
        compiler_params=_cparams(2),
        name="sparse_attn",
    )(proj, proj, proj, proj, kiwi, kiwi, q_norm, k_norm)


def _pool_kernel(u_ref, halo_ref, lin_ref, ps_ref, o_ref, scr, *, tm, per_seq):
    i = pl.program_id(0)
    first = (i % per_seq) == 0
    scr[0:POOL_HALO, :] = jnp.where(first, 0.0, halo_ref[...].astype(f32))
    scr[POOL_HALO:POOL_HALO + tm, :] = u_ref[...].astype(f32)
    t_in_seq = (i % per_seq) * tm + lax.broadcasted_iota(i32, (tm, 1), 0)
    for g, w in enumerate(POOL_WINDOWS):
        c0, c1 = g * POOL_GROUP_DIM, (g + 1) * POOL_GROUP_DIM
        cur = scr[POOL_HALO:POOL_HALO + tm, c0:c1]
        s = cur
        for j in range(1, w):
            s = s + scr[POOL_HALO - j:POOL_HALO - j + tm, c0:c1]
        count = jnp.minimum(t_in_seq + 1, w).astype(f32)
        pooled = s / count - cur
        mixed = jnp.dot(pooled.astype(bf16), lin_ref[g], preferred_element_type=f32)
        o_ref[:, c0:c1] = (mixed * ps_ref[:, c0:c1]).astype(o_ref.dtype)


def _pool(proj, pool_lin_b, pool_scale, seq):
    n = proj.shape[0]
    tm = min(512, seq)
    per_seq = seq // tm
    hb = tm // POOL_HALO
    kern = functools.partial(_pool_kernel, tm=tm, per_seq=per_seq)
    return pl.pallas_call(
        kern,
        grid=(n // tm,),
        in_specs=[pl.BlockSpec((tm, POOL_WIDTH), lambda i: (i, COL_U // POOL_WIDTH)),
                  pl.BlockSpec((POOL_HALO, POOL_WIDTH), lambda i: (jnp.maximum(i * hb - 1, 0), COL_U // POOL_WIDTH)),
                  pl.BlockSpec((len(POOL_WINDOWS), POOL_GROUP_DIM, POOL_GROUP_DIM), lambda i: (0, 0, 0)),
                  pl.BlockSpec((1, POOL_WIDTH), lambda i: (0, 0))],
        out_specs=pl.BlockSpec((tm, POOL_WIDTH), lambda i: (i, 0)),
        out_shape=jax.ShapeDtypeStruct((n, POOL_WIDTH), bf16),
        scratch_shapes=[pltpu.VMEM((POOL_HALO + tm, POOL_WIDTH), f32)],
        compiler_params=_cparams(1),
        name="pool",
    )(proj, proj, pool_lin_b, pool_scale)


def _merge_kernel(a_ref, p_ref, wa_ref, wp_ref, ga_ref, gp_ref, o_ref):
    ya = jnp.dot(a_ref[...], wa_ref[...], preferred_element_type=f32)
    yp = jnp.dot(p_ref[...], wp_ref[...], preferred_element_type=f32)
    o = _sigmoid(ga_ref[...].astype(f32)) * ya + _sigmoid(gp_ref[...].astype(f32)) * yp
    o_ref[...] = o.astype(o_ref.dtype)


def _merge(attn, pool, w_au, w_pu, proj):
    n = attn.shape[0]
    d = w_au.shape[1]
    tm, tn = min(1024, n), 1024
    ga0, gp0 = COL_GATE_A // tn, COL_GATE_P // tn
    return pl.pallas_call(
        _merge_kernel,
        grid=(n // tm, d // tn),
        in_specs=[pl.BlockSpec((tm, ATTN_WIDTH), lambda i, j: (i, 0)),
                  pl.BlockSpec((tm, POOL_WIDTH), lambda i, j: (i, 0)),
                  pl.BlockSpec((ATTN_WIDTH, tn), lambda i, j: (0, j)),
                  pl.BlockSpec((POOL_WIDTH, tn), lambda i, j: (0, j)),
                  pl.BlockSpec((tm, tn), lambda i, j: (i, ga0 + j)),
                  pl.BlockSpec((tm, tn), lambda i, j: (i, gp0 + j))],
        out_specs=pl.BlockSpec((tm, tn), lambda i, j: (i, j)),
        out_shape=jax.ShapeDtypeStruct((n, d), bf16),
        compiler_params=_cparams(2),
        name="merge",
    )(attn, pool, w_au, w_pu, proj, proj)


def _outproj_kernel(m_ref, x_ref, mod_ref, g_ref, wo_ref, wr_ref, x1_ref, h2_ref, lg_ref):
    y = jnp.dot(m_ref[...], wo_ref[...], preferred_element_type=f32)
    x1 = x_ref[...] + mod_ref[0, 2:3, :] * y
    x1_ref[...] = x1
    hn = x1 * lax.rsqrt(jnp.mean(x1 * x1, axis=-1, keepdims=True) + NORM_EPS) * g_ref[...]
    h2 = hn * (1.0 + mod_ref[0, 4:5, :]) + mod_ref[0, 3:4, :]
    _rows_to_slabs(h2_ref, h2)
    h_hi = h2.astype(bf16)
    h_lo = (h2 - h_hi.astype(f32)).astype(bf16)
    a = jnp.dot(h_hi, wr_ref[...], preferred_element_type=f32)
    b = jnp.dot(h_lo, wr_ref[:, 0:N_EXPERTS], preferred_element_type=f32)
    lg_ref[...] = a[:, 0:N_EXPERTS] + (a[:, N_EXPERTS:2 * N_EXPERTS] + b)


def _outproj(merged, x2, mod8, g_ffn, w_out_b, w_router, seq):
    n, d = x2.shape
    tm = min(512, seq)
    per_seq = seq // tm
    return pl.pallas_call(
        _outproj_kernel,
        grid=(n // tm,),
        in_specs=[pl.BlockSpec((tm, d), lambda i: (i, 0)),
                  pl.BlockSpec((tm, d), lambda i: (i, 0)),
                  pl.BlockSpec((1, MOD_ROWS, d), lambda i: (i // per_seq, 0, 0)),
                  pl.BlockSpec((1, d), lambda i: (0, 0)),
                  pl.BlockSpec((d, d), lambda i: (0, 0)),
                  pl.BlockSpec((d, 2 * N_EXPERTS), lambda i: (0, 0))],
        out_specs=[pl.BlockSpec((tm, d), lambda i: (i, 0)),
                   pl.BlockSpec((tm * d // LANE, LANE), lambda i: (i, 0)),
                   pl.BlockSpec((tm, N_EXPERTS), lambda i: (i, 0))],
        out_shape=[jax.ShapeDtypeStruct((n, d), f32),
                   jax.ShapeDtypeStruct((n * d // LANE, LANE), f32),
                   jax.ShapeDtypeStruct((n, N_EXPERTS), f32)],
        compiler_params=_cparams(1),
        name="outproj",
    )(merged, x2, mod8, g_ffn, w_out_b, w_router)


def _route_select(lg_ref, bias_ref):
    s = _sigmoid(lg_ref[...])
    sel = s + bias_ref[...]
    rows = [sel[EXPERTS_PER_GROUP * g:EXPERTS_PER_GROUP * (g + 1), :] for g in range(N_EXPERT_GROUPS)]
    grp = []
    for r in rows:
        m1 = jnp.max(r, axis=0, keepdims=True)
        eq = r == m1
        n_eq = jnp.sum(jnp.where(eq, 1.0, 0.0), axis=0, keepdims=True)
        m2 = jnp.max(jnp.where(eq, -jnp.inf, r), axis=0, keepdims=True)
        grp.append(m1 + jnp.where(n_eq >= 2.0, m1, m2))
    masked = []
    for g in range(N_EXPERT_GROUPS):
        rank = jnp.zeros_like(grp[g])
        for g2 in range(N_EXPERT_GROUPS):
            if g2 == g:
                continue
            beats = (grp[g2] >= grp[g]) if g2 < g else (grp[g2] > grp[g])
            rank = rank + jnp.where(beats, 1.0, 0.0)
        masked.append(jnp.where(rank < float(TOPK_GROUPS), rows[g], -jnp.inf))
    masked = jnp.concatenate(masked, axis=0)
    eidx = lax.broadcasted_iota(i32, (N_EXPERTS, 1), 0)
    rank = jnp.zeros_like(masked)
    for e2 in range(N_EXPERTS):
        row = masked[e2:e2 + 1, :]
        beats = (row > masked) | ((row == masked) & (eidx > e2))
        rank = rank + jnp.where(beats, 1.0, 0.0)
    return s, rank, rank < float(EXPERT_TOPK)


def _route_count_kernel(lg_ref, bias_ref, cnt_ref):
    @pl.when(pl.program_id(0) == 0)
    def _():
        cnt_ref[...] = jnp.zeros(cnt_ref.shape, f32)

    _, _, selected = _route_select(lg_ref, bias_ref)
    cnt_ref[...] += jnp.sum(jnp.where(selected, 1.0, 0.0), axis=1, keepdims=True)


def _route_assign_kernel(lg_ref, bias_ref, pstart_ref, gate_ref, pos_ref, run_scr):
    @pl.when(pl.program_id(0) == 0)
    def _():
        run_scr[...] = jnp.zeros(run_scr.shape, f32)

    tn = lg_ref.shape[1]
    s, rank, selected = _route_select(lg_ref, bias_ref)
    sel_f = jnp.where(selected, 1.0, 0.0)
    earlier = lax.broadcasted_iota(i32, (tn, tn), 0) < lax.broadcasted_iota(i32, (tn, tn), 1)
    prefix = jnp.dot(sel_f.astype(bf16), jnp.where(earlier, 1.0, 0.0).astype(bf16), preferred_element_type=f32)
    pos = pstart_ref[...] + run_scr[...] + prefix
    run_scr[...] += jnp.sum(sel_f, axis=1, keepdims=True)
    gate = jnp.where(selected, s, 0.0)
    gate = gate / jnp.sum(gate, axis=0, keepdims=True) * ROUTED_SCALE
    for j in range(EXPERT_TOPK):
        slot = rank == float(j)
        gate_ref[j:j + 1, :] = jnp.sum(jnp.where(slot, gate, 0.0), axis=0, keepdims=True)
        pos_ref[j:j + 1, :] = jnp.sum(jnp.where(slot, pos, 0.0), axis=0, keepdims=True).astype(i32)


def _route_counts(logits_t, bias_col):
    e, n = logits_t.shape
    tn = min(512, n)
    return pl.pallas_call(
        _route_count_kernel,
        grid=(n // tn,),
        in_specs=[pl.BlockSpec((e, tn), lambda t: (0, t)),
                  pl.BlockSpec((e, 1), lambda t: (0, 0))],
        out_specs=pl.BlockSpec((e, 1), lambda t: (0, 0)),
        out_shape=jax.ShapeDtypeStruct((e, 1), f32),
        compiler_params=_cparams(1),
        name="route_count",
    )(logits_t, bias_col)


def _route_assign(logits_t, bias_col, pstart_col):
    e, n = logits_t.shape
    tn = min(512, n)
    return pl.pallas_call(
        _route_assign_kernel,
        grid=(n // tn,),
        in_specs=[pl.BlockSpec((e, tn), lambda t: (0, t)),
                  pl.BlockSpec((e, 1), lambda t: (0, 0)),
                  pl.BlockSpec((e, 1), lambda t: (0, 0))],
        out_specs=[pl.BlockSpec((EXPERT_TOPK, tn), lambda t: (0, t)),
                   pl.BlockSpec((EXPERT_TOPK, tn), lambda t: (0, t))],
        out_shape=[jax.ShapeDtypeStruct((EXPERT_TOPK, n), f32),
                   jax.ShapeDtypeStruct((EXPERT_TOPK, n), i32)],
        scratch_shapes=[pltpu.VMEM((e, 1), f32)],
        compiler_params=_cparams(1),
        name="route_assign",
    )(logits_t, bias_col, pstart_col)


def _dispatch_kernel(pos_ref, pad_tile_ref, h_ref, x1_ref, mod_ref, ws1_ref, ws3_ref, ws2_ref,
                     xs_ref, base_ref, zero_scr, sem, fill_sem, *, n_tokens, tn, s):
    base = pl.program_id(0) * tn
    tile_slabs = EXPERT_TILE * s

    @pl.when(pl.program_id(0) == 0)
    def _zero_padded_tiles():
        zero_scr[...] = jnp.zeros(zero_scr.shape, zero_scr.dtype)

        def fill(e):
            first = pl.multiple_of(pad_tile_ref[e] * s, tile_slabs)
            return pltpu.make_async_copy(zero_scr, xs_ref.at[pl.ds(first, tile_slabs), :], fill_sem)

        def start(e, carry):
            @pl.when(pad_tile_ref[e] >= 0)
            def _():
                fill(e).start()
            return carry

        def wait(e, carry):
            @pl.when(pad_tile_ref[e] >= 0)
            def _():
                fill(e).wait()
            return carry

        def tail(tile):
            first = pl.multiple_of(tile * tile_slabs, tile_slabs)
            return pltpu.make_async_copy(zero_scr, xs_ref.at[pl.ds(first, tile_slabs), :], fill_sem)

        def start_tail(tile, carry):
            tail(tile).start()
            return carry

        def wait_tail(tile, carry):
            tail(tile).wait()
            return carry

        n_tiles_total = xs_ref.shape[0] // tile_slabs
        lax.fori_loop(0, N_EXPERTS, start, 0)
        lax.fori_loop(pad_tile_ref[N_EXPERTS], n_tiles_total, start_tail, 0)
        lax.fori_loop(0, N_EXPERTS, wait, 0)
        lax.fori_loop(pad_tile_ref[N_EXPERTS], n_tiles_total, wait_tail, 0)

    def row_copy(t, p):
        return pltpu.make_async_copy(h_ref.at[pl.ds(pl.multiple_of(t * s, s), s), :],
                                     xs_ref.at[pl.ds(pl.multiple_of(p * s, s), s), :], sem)

    def issue(t, carry):
        for j in range(EXPERT_TOPK):
            row_copy(t, pos_ref[j * n_tokens + base + t]).start(priority=j % 2)
        return carry

    lax.fori_loop(0, tn, issue, 0)

    hb = _slabs_to_rows(h_ref, tn).astype(bf16)
    act = (_silu(jnp.dot(hb, ws1_ref[...], preferred_element_type=f32))
           * jnp.dot(hb, ws3_ref[...], preferred_element_type=f32)).astype(bf16)
    shared = jnp.dot(act, ws2_ref[...], preferred_element_type=f32)
    base_ref[...] = x1_ref[...] + mod_ref[0, 5:6, :] * shared

    for j in range(EXPERT_TOPK):
        pltpu.make_async_copy(h_ref, xs_ref.at[pl.ds(0, tn * s), :], sem).wait()


def _dispatch(pos_flat, pad_tile, h2_slabs, x1, mod8, ws1b, ws3b, ws2b, n_rows, seq):
    n, d = x1.shape
    s = d // LANE
    hdim = ws1b.shape[1]
    tn = min(512, seq)
    per_seq = seq // tn
    kern = functools.partial(_dispatch_kernel, n_tokens=n, tn=tn, s=s)
    return pl.pallas_call(
        kern,
        grid=(n // tn,),
        in_specs=[pl.BlockSpec(memory_space=pltpu.SMEM),
                  pl.BlockSpec(memory_space=pltpu.SMEM),
                  pl.BlockSpec((tn * s, LANE), lambda i: (i, 0)),
                  pl.BlockSpec((tn, d), lambda i: (i, 0)),
                  pl.BlockSpec((1, MOD_ROWS, d), lambda i: (i // per_seq, 0, 0)),
                  pl.BlockSpec((d, hdim), lambda i: (0, 0)),
                  pl.BlockSpec((d, hdim), lambda i: (0, 0)),
                  pl.BlockSpec((hdim, d), lambda i: (0, 0))],
        out_specs=[pl.BlockSpec(memory_space=pl.ANY),
                   pl.BlockSpec((tn, d), lambda i: (i, 0))],
        out_shape=[jax.ShapeDtypeStruct((n_rows * s, LANE), f32),
                   jax.ShapeDtypeStruct((n, d), f32)],
        scratch_shapes=[pltpu.VMEM((EXPERT_TILE * s, LANE), f32), pltpu.SemaphoreType.DMA(()),
                        pltpu.SemaphoreType.DMA(())],
        compiler_params=_cparams(1),
        name="dispatch",
    )(pos_flat, pad_tile, h2_slabs, x1, mod8, ws1b, ws3b, ws2b)


def _expert_kernel(ord_ref, ue_ref, nu_ref, xs_hbm, w1_hbm, w3_hbm, w2_hbm, ys_hbm,
                   xbuf, ybuf, w13f, w2f, w1b, w3b, w2b, sems, xsems, ysems):
    i = pl.program_id(0)
    k = ord_ref[i]
    n_tiles = nu_ref[0]
    tb = xbuf.shape[1]

    def tile_rows(tile):
        return pl.ds(pl.multiple_of(tile * tb, tb), tb)

    def x_copy(tile):
        slot = tile % EXPERT_IN_SLOTS
        return pltpu.make_async_copy(xs_hbm.at[tile_rows(tile), :], xbuf.at[slot], xsems.at[slot])

    def y_copy(tile):
        slot = tile % EXPERT_OUT_SLOTS
        return pltpu.make_async_copy(ybuf.at[slot], ys_hbm.at[tile_rows(tile), :], ysems.at[slot])

    def fetch(kk, slot):
        e = ue_ref[kk]
        return (pltpu.make_async_copy(w1_hbm.at[e], w13f.at[slot, 0], sems.at[slot, 0]),
                pltpu.make_async_copy(w3_hbm.at[e], w13f.at[slot, 1], sems.at[slot, 1]),
                pltpu.make_async_copy(w2_hbm.at[e], w2f.at[slot], sems.at[slot, 2]))

    @pl.when(i == 0)
    def _prologue():
        for cp in fetch(0, 0):
            cp.start()
        for tile in range(EXPERT_IN_SLOTS - 1):
            @pl.when(tile < n_tiles)
            def _():
                x_copy(tile).start()

    def compute_tile(cast_slot):
        xs_ref = xbuf.at[i % EXPERT_IN_SLOTS]
        ys_ref = ybuf.at[i % EXPERT_OUT_SLOTS]
        t = EXPERT_TILE
        s = tb // t
        kc = 2 * LANE
        h1 = jnp.zeros((t, w1b.shape[1]), f32)
        h3 = jnp.zeros((t, w1b.shape[1]), f32)
        for c in range(s // 2):
            rows = slice(c * kc, (c + 1) * kc)
            if cast_slot is not None:
                w1b[rows, :] = w13f[cast_slot, 0, rows, :].astype(bf16)
                w3b[rows, :] = w13f[cast_slot, 1, rows, :].astype(bf16)
            xc = jnp.concatenate([xs_ref[pl.ds(2 * c, t, stride=s), :],
                                  xs_ref[pl.ds(2 * c + 1, t, stride=s), :]], axis=1).astype(bf16)
            h1 = h1 + jnp.dot(xc, w1b[rows, :], preferred_element_type=f32)
            h3 = h3 + jnp.dot(xc, w3b[rows, :], preferred_element_type=f32)
        a = (_silu(h1) * h3).astype(bf16)
        for c in range(s // 2):
            cols = slice(c * kc, (c + 1) * kc)
            if cast_slot is not None:
                w2b[:, cols] = w2f[cast_slot, :, cols].astype(bf16)
            yc = jnp.dot(a, w2b[:, cols], preferred_element_type=f32)
            ys_ref[pl.ds(2 * c, t, stride=s), :] = yc[:, 0:LANE]
            ys_ref[pl.ds(2 * c + 1, t, stride=s), :] = yc[:, LANE:kc]

    def run_tile(cast_slot):
        @pl.when(i + EXPERT_IN_SLOTS - 1 < n_tiles)
        def _():
            x_copy(i + EXPERT_IN_SLOTS - 1).start()

        x_copy(i).wait()

        @pl.when(i >= EXPERT_OUT_SLOTS)
        def _():
            y_copy(i - EXPERT_OUT_SLOTS).wait()

        compute_tile(cast_slot)
        y_copy(i).start()

        @pl.when(i == n_tiles - 1)
        def _drain():
            for back in range(EXPERT_OUT_SLOTS - 1, -1, -1):
                @pl.when(i >= back)
                def _():
                    y_copy(i - back).wait()

    first_of_expert = (i == 0) | (k != ord_ref[jnp.maximum(i - 1, 0)])

    @pl.when((i < n_tiles) & first_of_expert)
    def _first_tile():
        slot = k % 2
        for cp in fetch(k, slot):
            cp.wait()

        @pl.when(k + 1 < nu_ref[1])
        def _prefetch():
            for cp in fetch(k + 1, 1 - slot):
                cp.start(priority=1)

        run_tile(slot)

    @pl.when((i < n_tiles) & jnp.logical_not(first_of_expert))
    def _later_tile():
        run_tile(None)

    @pl.when(i >= n_tiles)
    def _unused():
        ybuf[0] = jnp.zeros(ybuf.shape[1:], ybuf.dtype)
        y_zero = pltpu.make_async_copy(ybuf.at[0], ys_hbm.at[tile_rows(i), :], ysems.at[0])
        y_zero.start()
        y_zero.wait()


def _experts(tile_ord, used_experts, n_used, xs, w1, w3, w2):
    _, d, hdim = w1.shape
    tb = EXPERT_TILE * d // LANE
    p, w = xs.shape
    grid_spec = pltpu.PrefetchScalarGridSpec(
        num_scalar_prefetch=3,
        grid=(p // tb,),
        in_specs=[pl.BlockSpec(memory_space=pl.ANY)] * 4,
        out_specs=pl.BlockSpec(memory_space=pl.ANY),
        scratch_shapes=[pltpu.VMEM((EXPERT_IN_SLOTS, tb, w), f32), pltpu.VMEM((EXPERT_OUT_SLOTS, tb, w), f32),
                        pltpu.VMEM((2, 2, d, hdim), f32), pltpu.VMEM((2, hdim, d), f32),
                        pltpu.VMEM((d, hdim), bf16), pltpu.VMEM((d, hdim), bf16), pltpu.VMEM((hdim, d), bf16),
                        pltpu.SemaphoreType.DMA((2, 3)), pltpu.SemaphoreType.DMA((EXPERT_IN_SLOTS,)),
                        pltpu.SemaphoreType.DMA((EXPERT_OUT_SLOTS,))],
    )
    return pl.pallas_call(
        _expert_kernel,
        grid_spec=grid_spec,
        out_shape=jax.ShapeDtypeStruct((p, w), f32),
        compiler_params=_cparams(1),
        name="experts",
    )(tile_ord, used_experts, n_used, xs, w1, w3, w2)


def _combine_kernel(pos_ref, gate_ref, base_ref, mod_ref, ys_ref, o_ref, gbuf, r_scr, sems, *, n_tokens, tn):
    step = pl.program_id(0)
    n_steps = pl.num_programs(0)
    slot = step % 2

    def gather_tile(s):
        base = s * tn
        sl = s % 2

        def issue(t, carry):
            for j in range(EXPERT_TOPK):
                p = pos_ref[j * n_tokens + base + t]
                pltpu.make_async_copy(ys_ref.at[p], gbuf.at[sl, j, t], sems.at[sl]).start(priority=j % 2)
            return carry

        lax.fori_loop(0, tn, issue, 0)

    @pl.when(step == 0)
    def _():
        gather_tile(step)

    @pl.when(step + 1 < n_steps)
    def _():
        gather_tile(step + 1)

    for j in range(EXPERT_TOPK):
        pltpu.make_async_copy(ys_ref.at[pl.ds(0, tn)], gbuf.at[slot, j], sems.at[slot]).wait()

    routed = gate_ref[:, 0:1, :] * gbuf[slot, 0]
    for j in range(1, EXPERT_TOPK):
        routed = routed + gate_ref[:, j:j + 1, :] * gbuf[slot, j]
    r_scr[...] = routed.reshape(r_scr.shape)
    o_ref[...] = base_ref[...] + mod_ref[0, 5:6, :] * _slabs_to_rows(r_scr, tn)


def _combine(pos_flat, gate_rep, base, mod8, ys, seq):
    n, d = base.shape
    s = d // LANE
    tn = min(256, seq)
    per_seq = seq // tn
    kern = functools.partial(_combine_kernel, n_tokens=n, tn=tn)
    return pl.pallas_call(
        kern,
        grid=(n // tn,),
        in_specs=[pl.BlockSpec(memory_space=pltpu.SMEM),
                  pl.BlockSpec((tn, EXPERT_TOPK, LANE), lambda i: (i, 0, 0)),
                  pl.BlockSpec((tn, d), lambda i: (i, 0)),
                  pl.BlockSpec((1, MOD_ROWS, d), lambda i: (i // per_seq, 0, 0)),
                  pl.BlockSpec(memory_space=pl.ANY)],
        out_specs=pl.BlockSpec((tn, d), lambda i: (i, 0)),
        out_shape=jax.ShapeDtypeStruct((n, d), f32),
        scratch_shapes=[pltpu.VMEM((2, EXPERT_TOPK, tn, s, LANE), f32), pltpu.VMEM((tn * s, LANE), f32),
                        pltpu.SemaphoreType.DMA((2,))],
        compiler_params=_cparams(1),
        name="combine",
    )(pos_flat, gate_rep, base, mod8, ys)


def _layer(x, c, w_ada, b_ada, g_mix, w_in, q_norm, k_norm, w_attn_up, pool_lin, pool_scale, w_pool_up,
           w_out, g_ffn, w_router, router_bias, w1, w3, w2, ws1, ws3, ws2):
    batch, seq, d = x.shape
    n = batch * seq
    x2 = x.reshape(n, d)

    offs = [0, ATTN_WIDTH, ATTN_WIDTH + KV_WIDTH, ATTN_WIDTH + 2 * KV_WIDTH]
    w_q = w_in[:, offs[0]:offs[1]]
    w_k = w_in[:, offs[1]:offs[2]]
    w_v = w_in[:, offs[2]:offs[3]]
    o_qi = offs[3]
    w_qi = w_in[:, o_qi:o_qi + IDX_HEADS * IDX_DIM]
    o_ki = o_qi + IDX_HEADS * IDX_DIM
    w_ki = w_in[:, o_ki:o_ki + IDX_DIM]
    o_wi = o_ki + IDX_DIM
    w_wi = w_in[:, o_wi:o_wi + IDX_HEADS]
    o_u = o_wi + IDX_HEADS
    w_u = w_in[:, o_u:o_u + POOL_WIDTH]
    o_g = o_u + POOL_WIDTH
    w_g = w_in[:, o_g:o_g + 2 * d]
    w_main = jnp.concatenate([w_g, w_q, w_qi, w_u, w_k, w_v], axis=1).astype(bf16)
    w_kiwi = jnp.concatenate(
        [w_ki, w_wi, jnp.zeros((d, KIWI_WIDTH - IDX_DIM - IDX_HEADS), w_in.dtype)], axis=1).astype(bf16)

    assert batch <= SUBLANES and seq % min(KEY_CHUNK, seq) == 0 and d % (2 * LANE) == 0
    c8 = jnp.zeros((SUBLANES, d), f32).at[:batch].set(c)
    mod = _ada(c8, w_ada, b_ada.reshape(1, -1))[:batch]
    mod8 = jnp.zeros((batch, MOD_ROWS, d), f32).at[:, :N_MOD].set(mod.reshape(batch, N_MOD, d))

    proj, kiwi = _inproj(x2, mod8, g_mix.reshape(1, d), w_main, w_kiwi, seq)
    attn = _attention(proj, kiwi, q_norm.reshape(1, -1), k_norm.reshape(1, -1), batch, seq)
    pool = _pool(proj, pool_lin.astype(bf16), pool_scale.reshape(1, -1), seq)
    merged = _merge(attn, pool, w_attn_up.astype(bf16), w_pool_up.astype(bf16), proj)
    wr_hi = w_router.astype(bf16)
    wr_lo = (w_router - wr_hi.astype(f32)).astype(bf16)
    x1, h2, logits = _outproj(merged, x2, mod8, g_ffn.reshape(1, d), w_out.astype(bf16),
                              jnp.concatenate([wr_hi, wr_lo], axis=1), seq)

    logits_t = logits.T
    bias_col = router_bias.reshape(N_EXPERTS, 1)
    counts = _route_counts(logits_t, bias_col)[:, 0].astype(i32)
    t = EXPERT_TILE
    tiles_e = (counts + t - 1) // t
    tile_end = jnp.cumsum(tiles_e)
    pstart = ((tile_end - tiles_e) * t).astype(f32).reshape(N_EXPERTS, 1)
    n_tiles = n * EXPERT_TOPK // t + N_EXPERTS
    n_used = tile_end[-1]
    tile_ids = jnp.minimum(jnp.arange(n_tiles, dtype=i32), n_used - 1)
    block_e = jnp.sum((tile_end[None, :] <= tile_ids[:, None]).astype(i32), axis=1)
    block_e = jnp.minimum(block_e, N_EXPERTS - 1)
    pad_tile = jnp.where(tiles_e > 0, (tile_end - 1) * t, -1).astype(i32)
    pad_tile = jnp.concatenate([pad_tile, n_used.reshape(1).astype(i32)])
    gate8, pos8 = _route_assign(logits_t, bias_col, pstart)
    pos_flat = pos8.reshape(-1)

    cum_used = jnp.cumsum((tiles_e > 0).astype(i32))
    slots = jnp.arange(N_EXPERTS, dtype=i32)
    used_experts = jnp.minimum(jnp.sum((cum_used[None, :] <= slots[:, None]).astype(i32), axis=1), N_EXPERTS - 1)
    tile_ord = jnp.sum(jnp.where(block_e[:, None] == slots[None, :], cum_used[None, :] - 1, 0), axis=1).astype(i32)
    n_used2 = jnp.stack([n_used, cum_used[-1]]).astype(i32)

    slabs = d // LANE
    xs, base = _dispatch(pos_flat, pad_tile, h2, x1, mod8, ws1.astype(bf16), ws3.astype(bf16), ws2.astype(bf16),
                         n_tiles * t, seq)
    ys = _experts(tile_ord, used_experts, n_used2, xs, w1, w3, w2)
    gate_rep = jnp.broadcast_to(gate8.T[:, :, None], (n, EXPERT_TOPK, LANE))
    out = _combine(pos_flat, gate_rep, base, mod8, ys.reshape(-1, slabs, LANE), seq)
    return out.reshape(batch, seq, d)


def kernel(x, c, w_ada, b_ada, g_mix, w_in, q_norm, k_norm, w_attn_up, pool_lin, pool_scale, w_pool_up, w_out, g_ffn, w_router, router_bias, w1, w3, w2, ws1, ws3, ws2):
    for l in range(w_ada.shape[0]):
        x = _layer(x, c, w_ada[l], b_ada[l], g_mix[l], w_in[l], q_norm[l], k_norm[l], w_attn_up[l], pool_lin[l],
                   pool_scale[l], w_pool_up[l], w_out[l], g_ffn[l], w_router[l], router_bias[l], w1[l], w3[l],
                   w2[l], ws1[l], ws3[l], ws2[l])
    return x
```

```python
import functools

import jax
import jax.numpy as jnp
from jax import lax
from jax.experimental import pallas as pl
from jax.experimental.pallas import tpu as pltpu

f32 = jnp.float32
bf16 = jnp.bfloat16
i32 = jnp.int32

N_HEADS = 8
HEAD_DIM = 128
N_KV_HEADS = 2
Q_PER_KV = N_HEADS // N_KV_HEADS
ATTN_WIDTH = N_HEADS * HEAD_DIM
KV_WIDTH = N_KV_HEADS * HEAD_DIM
IDX_HEADS = 16
IDX_DIM = 64
IDX_TOPK_MAX = 256
Q_BLOCK = 128
LANE = 128
POOL_WINDOWS = (2, 4, 8, 16)
POOL_GROUP_DIM = 256
POOL_WIDTH = 1024
POOL_HALO = 16
N_EXPERTS = 64
N_EXPERT_GROUPS = 8
EXPERTS_PER_GROUP = 8
TOPK_GROUPS = 4
EXPERT_TOPK = 8
EXPERT_HIDDEN = 512
ROUTED_SCALE = 2.5
NORM_EPS = 1e-6
N_MOD = 6
SUBLANES = 8
MOD_ROWS = SUBLANES

COL_GATE_A = 0
COL_GATE_P = 2048
COL_Q = 4096
COL_QI = 5120
COL_U = 6144
COL_K = 7168
COL_V = 7424
MAIN_WIDTH = 7680
KIWI_WIDTH = 128

KEY_CHUNK = 512
ATTN_PIECE = 256
SEARCH_CHECK_FROM = 24
SEARCH_GROUP = 2
EXPERT_TILE = 256
EXPERT_IN_SLOTS = 3
EXPERT_OUT_SLOTS = 2
VMEM_LIMIT = 56 * 1024 * 1024
INT_MIN = -2147483648
KEY_OF_LOWEST_FINITE = -2139095040
NEG_BIG = -1e30
LOG2E = 1.4426950408889634


def _alibi_slope(h):
    return 2.0 ** (-8.0 * (h + 1) / N_HEADS)


def _cparams(n_axes, vmem=VMEM_LIMIT):
    return pltpu.CompilerParams(dimension_semantics=("arbitrary",) * n_axes, vmem_limit_bytes=vmem)


def _sigmoid(x):
    return 1.0 / (1.0 + jnp.exp(-x))


def _silu(x):
    return x * _sigmoid(x)


def _rows_to_slabs(ref, val):
    rows, width = val.shape
    s = width // LANE
    for j in range(s):
        ref[pl.ds(j, rows, stride=s), :] = val[:, j * LANE:(j + 1) * LANE]


def _slabs_to_rows(ref, rows):
    s = ref.shape[0] // rows
    return jnp.concatenate([ref[pl.ds(j, rows, stride=s), :] for j in range(s)], axis=1)


def _ada_kernel(c_ref, w_ref, b_ref, o_ref):
    sc = _silu(c_ref[...]).astype(bf16)
    o_ref[...] = jnp.dot(sc, w_ref[...].astype(bf16), preferred_element_type=f32) + b_ref[...]


def _ada(c8, w_ada, b_ada):
    d, n = w_ada.shape
    tn = 1024
    return pl.pallas_call(
        _ada_kernel,
        grid=(n // tn,),
        in_specs=[pl.BlockSpec((SUBLANES, d), lambda j: (0, 0)),
                  pl.BlockSpec((d, tn), lambda j: (0, j)),
                  pl.BlockSpec((1, tn), lambda j: (0, j))],
        out_specs=pl.BlockSpec((SUBLANES, tn), lambda j: (0, j)),
        out_shape=jax.ShapeDtypeStruct((SUBLANES, n), f32),
        compiler_params=_cparams(1),
        name="adaln",
    )(c8, w_ada, b_ada)


def _inproj_kernel(x_ref, mod_ref, g_ref, w_ref, wk_ref, o_ref, kiwi_ref, h_scr):
    @pl.when(pl.program_id(1) == 0)
    def _():
        x = x_ref[...]
        y = x * lax.rsqrt(jnp.mean(x * x, axis=-1, keepdims=True) + NORM_EPS) * g_ref[...]
        h = y * (1.0 + mod_ref[0, 1:2, :]) + mod_ref[0, 0:1, :]
        hb = h.astype(bf16)
        h_scr[...] = hb
        kiwi_ref[...] = jnp.dot(hb, wk_ref[...], preferred_element_type=f32)

    o_ref[...] = jnp.dot(h_scr[...], w_ref[...], preferred_element_type=f32).astype(o_ref.dtype)


def _inproj(x2, mod8, g_mix, w_main, w_kiwi, seq):
    n, d = x2.shape
    tm, tn = 1024, 1536
    tm = min(tm, seq)
    per_seq = seq // tm
    return pl.pallas_call(
        _inproj_kernel,
        grid=(n // tm, MAIN_WIDTH // tn),
        in_specs=[pl.BlockSpec((tm, d), lambda i, j: (i, 0)),
                  pl.BlockSpec((1, MOD_ROWS, d), lambda i, j: (i // per_seq, 0, 0)),
                  pl.BlockSpec((1, d), lambda i, j: (0, 0)),
                  pl.BlockSpec((d, tn), lambda i, j: (0, j)),
                  pl.BlockSpec((d, KIWI_WIDTH), lambda i, j: (0, 0))],
        out_specs=[pl.BlockSpec((tm, tn), lambda i, j: (i, j)),
                   pl.BlockSpec((tm, KIWI_WIDTH), lambda i, j: (i, 0))],
        out_shape=[jax.ShapeDtypeStruct((n, MAIN_WIDTH), bf16),
                   jax.ShapeDtypeStruct((n, KIWI_WIDTH), f32)],
        scratch_shapes=[pltpu.VMEM((tm, d), bf16)],
        compiler_params=_cparams(2),
        name="inproj",
    )(x2, mod8, g_mix, w_main, w_kiwi)


def _tree_sum(parts):
    while len(parts) > 1:
        parts = [parts[a] + parts[a + 1] for a in range(0, len(parts) - 1, 2)] + ([parts[-1]] if len(parts) % 2 else [])
    return parts[0]


def _attn_t_kernel(q_ref, qi_ref, k_ref, v_ref, kiwi_all_ref, kiwi_blk_ref, qn_ref, kn_ref, o_ref,
                   kn_scr, ki_scr, vt_scr, score_scr, qt_scr, qit_scr, bias_scr, m_scr, l_scr, acc_scr, *, seq, topk):
    i = pl.program_id(1)
    ck = min(KEY_CHUNK, seq)
    n_chunks = i // (ck // Q_BLOCK) + 1
    heads_per_dot = 4

    @pl.when(i == 0)
    def _prep_keys():
        for n in range(N_KV_HEADS):
            kf = k_ref[:, n * HEAD_DIM:(n + 1) * HEAD_DIM].astype(f32)
            r = lax.rsqrt(jnp.mean(kf * kf, axis=-1, keepdims=True) + NORM_EPS)
            kn_scr[:, n * HEAD_DIM:(n + 1) * HEAD_DIM] = (kf * r * kn_ref[...]).astype(bf16)
        ki_scr[...] = kiwi_all_ref[:, 0:IDX_DIM].astype(bf16)
        for c in range(seq // ck):
            vt_scr[c] = v_ref[c * ck:(c + 1) * ck, :].astype(f32).T.astype(bf16)
        rel = (lax.broadcasted_iota(i32, (ck, Q_BLOCK), 1) - lax.broadcasted_iota(i32, (ck, Q_BLOCK), 0)).astype(f32)
        for h in range(N_HEADS):
            bias_scr[h] = rel * (-_alibi_slope(h) * LOG2E)

    q_t = []
    for h in range(N_HEADS):
        qf = q_ref[:, h * HEAD_DIM:(h + 1) * HEAD_DIM].astype(f32)
        r = lax.rsqrt(jnp.mean(qf * qf, axis=-1, keepdims=True) + NORM_EPS)
        q_t.append((qf * r * qn_ref[...] * (HEAD_DIM ** -0.5 * LOG2E)).T)
    for n in range(N_KV_HEADS):
        qt_scr[n] = jnp.concatenate(q_t[n * Q_PER_KV:(n + 1) * Q_PER_KV], axis=1).astype(bf16)
    qi_t = qi_ref[...].astype(f32).T
    for a in range(IDX_HEADS // heads_per_dot):
        qit_scr[a] = jnp.concatenate(
            [qi_t[(a * heads_per_dot + b) * IDX_DIM:(a * heads_per_dot + b + 1) * IDX_DIM, :]
             for b in range(heads_per_dot)], axis=1).astype(bf16)
    wi_t = kiwi_blk_ref[...].T[IDX_DIM:IDX_DIM + IDX_HEADS, :] * (IDX_HEADS ** -0.5 * IDX_DIM ** -0.5)
    qpos = i * Q_BLOCK + lax.broadcasted_iota(i32, (1, Q_BLOCK), 1)

    def index_chunk(c, carry):
        start = pl.multiple_of(c * ck, ck)
        kc = ki_scr[pl.ds(start, ck), :]
        acc = jnp.zeros((ck, Q_BLOCK), f32)
        for a in range(IDX_HEADS // heads_per_dot):
            d = jnp.dot(kc, qit_scr[a], preferred_element_type=f32)
            for b in range(heads_per_dot):
                h = a * heads_per_dot + b
                acc = acc + jnp.maximum(d[:, b * Q_BLOCK:(b + 1) * Q_BLOCK], 0.0) * wi_t[h:h + 1, :]
        kpos = start + lax.broadcasted_iota(i32, (ck, 1), 0)
        score_scr[c] = jnp.where(kpos <= qpos, acc, -jnp.inf)
        return carry

    lax.fori_loop(0, n_chunks, index_chunk, 0)

    def key_to_float(key):
        bits = key ^ (lax.shift_right_arithmetic(key, 31) & jnp.int32(0x7FFFFFFF))
        return lax.bitcast_convert_type(bits, f32)

    def bit_step(b, state):
        t_u, kept = state
        bit = lax.shift_left(jnp.int32(1), 31 - b)
        cand_u = t_u | bit
        cand = key_to_float(cand_u ^ jnp.int32(INT_MIN))

        def count_chunk(c, cnt):
            ge = jnp.where(score_scr[c] >= cand, 1.0, 0.0)
            return cnt + _tree_sum([ge[s * 8:(s + 1) * 8, :] for s in range(ck // 8)])

        cnt = lax.fori_loop(0, n_chunks, count_chunk, jnp.zeros((8, Q_BLOCK), f32))
        total = jnp.sum(cnt, axis=0, keepdims=True)
        accept = total >= float(topk)
        return jnp.where(accept, cand_u, t_u), jnp.where(accept, total, kept)

    state = lax.fori_loop(0, SEARCH_CHECK_FROM, bit_step,
                          (jnp.zeros((1, Q_BLOCK), i32), jnp.full((1, Q_BLOCK), float(seq + 1), f32)))

    def settled(kept):
        ok = (kept == float(topk)) | (qpos < topk)
        return jnp.min(jnp.where(ok, 1.0, 0.0)) > 0.0

    def more_bits(loop):
        b, _, done = loop
        return (b < 32) & jnp.logical_not(done)

    def bit_group(loop):
        b, st, _ = loop
        st = lax.fori_loop(b, b + SEARCH_GROUP, bit_step, st)
        return b + SEARCH_GROUP, st, settled(st[1])

    _, (t_u, _), _ = lax.while_loop(more_bits, bit_group, (jnp.int32(SEARCH_CHECK_FROM), state, settled(state[1])))
    thr = key_to_float(jnp.maximum(t_u ^ jnp.int32(INT_MIN), jnp.int32(KEY_OF_LOWEST_FINITE)))

    def count_ties(c, carry):
        ge, gt = carry
        sc = score_scr[c]
        ge = ge + _tree_sum([jnp.where(sc[s * 8:(s + 1) * 8, :] >= thr, 1.0, 0.0) for s in range(ck // 8)])
        gt = gt + _tree_sum([jnp.where(sc[s * 8:(s + 1) * 8, :] > thr, 1.0, 0.0) for s in range(ck // 8)])
        return ge, gt

    zeros8 = jnp.zeros((8, Q_BLOCK), f32)
    ge8, gt8 = lax.fori_loop(0, n_chunks, count_ties, (zeros8, zeros8))
    excess = jnp.sum(ge8, axis=0, keepdims=True) > float(topk)
    need = float(topk) - jnp.sum(gt8, axis=0, keepdims=True)

    @pl.when(jnp.max(jnp.where(excess, 1.0, 0.0)) > 0.0)
    def _break_ties():
        pos_bits = (seq - 1).bit_length()

        def key_positions(c):
            return c * ck + lax.broadcasted_iota(i32, (ck, Q_BLOCK), 0)

        def pos_step(b, q):
            cand = q | lax.shift_left(jnp.int32(1), pos_bits - 1 - b)

            def count_before(c, cnt):
                hit = jnp.where((score_scr[c] == thr) & (key_positions(c) < cand), 1.0, 0.0)
                return cnt + _tree_sum([hit[s * 8:(s + 1) * 8, :] for s in range(ck // 8)])

            before = jnp.sum(lax.fori_loop(0, n_chunks, count_before, zeros8), axis=0, keepdims=True)
            return jnp.where(before < need, cand, q)

        q = lax.fori_loop(0, pos_bits, pos_step, jnp.zeros((1, Q_BLOCK), i32))
        q = jnp.where(excess, q, jnp.int32(seq))

        def demote(c, carry):
            sc = score_scr[c]
            score_scr[c] = jnp.where((sc == thr) & (key_positions(c) > q), -jnp.inf, sc)
            return carry

        lax.fori_loop(0, n_chunks, demote, 0)

    m_scr[...] = jnp.full(m_scr.shape, NEG_BIG, f32)
    l_scr[...] = jnp.zeros(l_scr.shape, f32)
    acc_scr[...] = jnp.zeros(acc_scr.shape, f32)
    lane_head = lax.broadcasted_iota(i32, (1, Q_PER_KV * Q_BLOCK), 1) // Q_BLOCK

    piece = min(ATTN_PIECE, ck)
    slope_rows = []
    for n in range(N_KV_HEADS):
        slope_row = jnp.zeros((1, Q_PER_KV * Q_BLOCK), f32)
        for g in range(Q_PER_KV):
            slope_row = jnp.where(lane_head == g, _alibi_slope(n * Q_PER_KV + g) * LOG2E, slope_row)
        slope_rows.append(slope_row)

    def attn_chunk(c, carry):
        start = pl.multiple_of(c * ck, ck)
        tile_dist = (i * Q_BLOCK - start).astype(f32)
        units = [(sub, n) for sub in range(ck // piece) for n in range(N_KV_HEADS)]

        def scores(sub, n):
            kc = kn_scr[pl.ds(pl.multiple_of(start + sub * piece, piece), piece), n * HEAD_DIM:(n + 1) * HEAD_DIM]
            return jnp.dot(kc, qt_scr[n], preferred_element_type=f32)

        ahead = 3
        z_of = {u: scores(*u) for u in units[:ahead]}
        for idx, (sub, n) in enumerate(units):
            r0, r1 = sub * piece, (sub + 1) * piece
            sel = score_scr[c, r0:r1, :] >= thr
            z_all = z_of.pop((sub, n))
            z = jnp.concatenate(
                [jnp.where(sel, z_all[:, g * Q_BLOCK:(g + 1) * Q_BLOCK] + bias_scr[n * Q_PER_KV + g, r0:r1, :],
                           NEG_BIG) for g in range(Q_PER_KV)], axis=1)
            off = slope_rows[n] * (-tile_dist)
            m_old = m_scr[n]
            m_new = jnp.maximum(m_old, jnp.max(z, axis=0, keepdims=True) + off)
            alpha = jnp.exp2(m_old - m_new)
            p = jnp.exp2(z + (off - m_new))
            l_scr[n] = alpha * l_scr[n] + jnp.sum(p, axis=0, keepdims=True)
            vt = vt_scr[c, n * HEAD_DIM:(n + 1) * HEAD_DIM, r0:r1]
            acc_scr[n] = alpha * acc_scr[n] + jnp.dot(vt, p.astype(bf16), preferred_element_type=f32)
            m_scr[n] = m_new
            if idx + ahead < len(units):
                z_of[units[idx + ahead]] = scores(*units[idx + ahead])
        return carry

    lax.fori_loop(0, n_chunks, attn_chunk, 0)
    for n in range(N_KV_HEADS):
        out_t = acc_scr[n] / l_scr[n]
        for g in range(Q_PER_KV):
            h = n * Q_PER_KV + g
            o_ref[:, h * HEAD_DIM:(h + 1) * HEAD_DIM] = out_t[:, g * Q_BLOCK:(g + 1) * Q_BLOCK].T.astype(o_ref.dtype)


def _attention(proj, kiwi, q_norm, k_norm, batch, seq):
    n = batch * seq
    nq = seq // Q_BLOCK
    ck = min(KEY_CHUNK, seq)
    topk = min(IDX_TOPK_MAX, seq // 4)
    kern = functools.partial(_attn_t_kernel, seq=seq, topk=topk)
    return pl.pallas_call(
        kern,
        grid=(batch, nq),
        in_specs=[pl.BlockSpec((Q_BLOCK, ATTN_WIDTH), lambda b, i: (b * nq + i, COL_Q // ATTN_WIDTH)),
                  pl.BlockSpec((Q_BLOCK, IDX_HEADS * IDX_DIM), lambda b, i: (b * nq + i, COL_QI // (IDX_HEADS * IDX_DIM))),
                  pl.BlockSpec((seq, KV_WIDTH), lambda b, i: (b, COL_K // KV_WIDTH)),
                  pl.BlockSpec((seq, KV_WIDTH), lambda b, i: (b, COL_V // KV_WIDTH)),
                  pl.BlockSpec((seq, KIWI_WIDTH), lambda b, i: (b, 0)),
                  pl.BlockSpec((Q_BLOCK, KIWI_WIDTH), lambda b, i: (b * nq + i, 0)),
                  pl.BlockSpec((1, HEAD_DIM), lambda b, i: (0, 0)),
                  pl.BlockSpec((1, HEAD_DIM), lambda b, i: (0, 0))],
        out_specs=pl.BlockSpec((Q_BLOCK, ATTN_WIDTH), lambda b, i: (b * nq + i, 0)),
        out_shape=jax.ShapeDtypeStruct((n, ATTN_WIDTH), bf16),
        scratch_shapes=[pltpu.VMEM((seq, KV_WIDTH), bf16),
                        pltpu.VMEM((seq, IDX_DIM), bf16),
                        pltpu.VMEM((seq // ck, KV_WIDTH, ck), bf16),
                        pltpu.VMEM((seq // ck, ck, Q_BLOCK), f32),
                        pltpu.VMEM((N_KV_HEADS, HEAD_DIM, Q_PER_KV * Q_BLOCK), bf16),
                        pltpu.VMEM((IDX_HEADS // 4, IDX_DIM, 4 * Q_BLOCK), bf16),
                        pltpu.VMEM((N_HEADS, ck, Q_BLOCK), f32),
                        pltpu.VMEM((N_KV_HEADS, 1, Q_PER_KV * Q_BLOCK), f32),
                        pltpu.VMEM((N_KV_HEADS, 1, Q_PER_KV * Q_BLOCK), f32),
                        pltpu.VMEM((N_KV_HEADS, HEAD_DIM, Q_PER_KV * Q_BLOCK), f32)],
        compiler_params=_cparams(2),
        name="sparse_attn",
    )(proj, proj, proj, proj, kiwi, kiwi, q_norm, k_norm)


def _pool_kernel(u_ref, halo_ref, lin_ref, ps_ref, o_ref, scr, *, tm, per_seq):
    i = pl.program_id(0)
    first = (i % per_seq) == 0
    scr[0:POOL_HALO, :] = jnp.where(first, 0.0, halo_ref[...].astype(f32))
    scr[POOL_HALO:POOL_HALO + tm, :] = u_ref[...].astype(f32)
    t_in_seq = (i % per_seq) * tm + lax.broadcasted_iota(i32, (tm, 1), 0)
    for g, w in enumerate(POOL_WINDOWS):
        c0, c1 = g * POOL_GROUP_DIM, (g + 1) * POOL_GROUP_DIM
        cur = scr[POOL_HALO:POOL_HALO + tm, c0:c1]
        s = cur
        for j in range(1, w):
            s = s + scr[POOL_HALO - j:POOL_HALO - j + tm, c0:c1]
        count = jnp.minimum(t_in_seq + 1, w).astype(f32)
        pooled = s / count - cur
        mixed = jnp.dot(pooled.astype(bf16), lin_ref[g], preferred_element_type=f32)
        o_ref[:, c0:c1] = (mixed * ps_ref[:, c0:c1]).astype(o_ref.dtype)


def _pool(proj, pool_lin_b, pool_scale, seq):
    n = proj.shape[0]
    tm = min(512, seq)
    per_seq = seq // tm
    hb = tm // POOL_HALO
    kern = functools.partial(_pool_kernel, tm=tm, per_seq=per_seq)
    return pl.pallas_call(
        kern,
        grid=(n // tm,),
        in_specs=[pl.BlockSpec((tm, POOL_WIDTH), lambda i: (i, COL_U // POOL_WIDTH)),
                  pl.BlockSpec((POOL_HALO, POOL_WIDTH), lambda i: (jnp.maximum(i * hb - 1, 0), COL_U // POOL_WIDTH)),
                  pl.BlockSpec((len(POOL_WINDOWS), POOL_GROUP_DIM, POOL_GROUP_DIM), lambda i: (0, 0, 0)),
                  pl.BlockSpec((1, POOL_WIDTH), lambda i: (0, 0))],
        out_specs=pl.BlockSpec((tm, POOL_WIDTH), lambda i: (i, 0)),
        out_shape=jax.ShapeDtypeStruct((n, POOL_WIDTH), bf16),
        scratch_shapes=[pltpu.VMEM((POOL_HALO + tm, POOL_WIDTH), f32)],
        compiler_params=_cparams(1),
        name="pool",
    )(proj, proj, pool_lin_b, pool_scale)


def _merge_kernel(a_ref, p_ref, wa_ref, wp_ref, ga_ref, gp_ref, o_ref):
    ya = jnp.dot(a_ref[...], wa_ref[...], preferred_element_type=f32)
    yp = jnp.dot(p_ref[...], wp_ref[...], preferred_element_type=f32)
    o = _sigmoid(ga_ref[...].astype(f32)) * ya + _sigmoid(gp_ref[...].astype(f32)) * yp
    o_ref[...] = o.astype(o_ref.dtype)


def _merge(attn, pool, w_au, w_pu, proj):
    n = attn.shape[0]
    d = w_au.shape[1]
    tm, tn = min(1024, n), 1024
    ga0, gp0 = COL_GATE_A // tn, COL_GATE_P // tn
    return pl.pallas_call(
        _merge_kernel,
        grid=(n // tm, d // tn),
        in_specs=[pl.BlockSpec((tm, ATTN_WIDTH), lambda i, j: (i, 0)),
                  pl.BlockSpec((tm, POOL_WIDTH), lambda i, j: (i, 0)),
                  pl.BlockSpec((ATTN_WIDTH, tn), lambda i, j: (0, j)),
                  pl.BlockSpec((POOL_WIDTH, tn), lambda i, j: (0, j)),
                  pl.BlockSpec((tm, tn), lambda i, j: (i, ga0 + j)),
                  pl.BlockSpec((tm, tn), lambda i, j: (i, gp0 + j))],
        out_specs=pl.BlockSpec((tm, tn), lambda i, j: (i, j)),
        out_shape=jax.ShapeDtypeStruct((n, d), bf16),
        compiler_params=_cparams(2),
        name="merge",
    )(attn, pool, w_au, w_pu, proj, proj)


def _outproj_kernel(m_ref, x_ref, mod_ref, g_ref, wo_ref, wr_ref, x1_ref, h2_ref, lg_ref):
    y = jnp.dot(m_ref[...], wo_ref[...], preferred_element_type=f32)
    x1 = x_ref[...] + mod_ref[0, 2:3, :] * y
    x1_ref[...] = x1
    hn = x1 * lax.rsqrt(jnp.mean(x1 * x1, axis=-1, keepdims=True) + NORM_EPS) * g_ref[...]
    h2 = hn * (1.0 + mod_ref[0, 4:5, :]) + mod_ref[0, 3:4, :]
    _rows_to_slabs(h2_ref, h2)
    h_hi = h2.astype(bf16)
    h_lo = (h2 - h_hi.astype(f32)).astype(bf16)
    a = jnp.dot(h_hi, wr_ref[...], preferred_element_type=f32)
    b = jnp.dot(h_lo, wr_ref[:, 0:N_EXPERTS], preferred_element_type=f32)
    lg_ref[...] = a[:, 0:N_EXPERTS] + (a[:, N_EXPERTS:2 * N_EXPERTS] + b)


def _outproj(merged, x2, mod8, g_ffn, w_out_b, w_router, seq):
    n, d = x2.shape
    tm = min(512, seq)
    per_seq = seq // tm
    return pl.pallas_call(
        _outproj_kernel,
        grid=(n // tm,),
        in_specs=[pl.BlockSpec((tm, d), lambda i: (i, 0)),
                  pl.BlockSpec((tm, d), lambda i: (i, 0)),
                  pl.BlockSpec((1, MOD_ROWS, d), lambda i: (i // per_seq, 0, 0)),
                  pl.BlockSpec((1, d), lambda i: (0, 0)),
                  pl.BlockSpec((d, d), lambda i: (0, 0)),
                  pl.BlockSpec((d, 2 * N_EXPERTS), lambda i: (0, 0))],
        out_specs=[pl.BlockSpec((tm, d), lambda i: (i, 0)),
                   pl.BlockSpec((tm * d // LANE, LANE), lambda i: (i, 0)),
                   pl.BlockSpec((tm, N_EXPERTS), lambda i: (i, 0))],
        out_shape=[jax.ShapeDtypeStruct((n, d), f32),
                   jax.ShapeDtypeStruct((n * d // LANE, LANE), f32),
                   jax.ShapeDtypeStruct((n, N_EXPERTS), f32)],
        compiler_params=_cparams(1),
        name="outproj",
    )(merged, x2, mod8, g_ffn, w_out_b, w_router)


def _route_select(lg_ref, bias_ref):
    s = _sigmoid(lg_ref[...])
    sel = s + bias_ref[...]
    rows = [sel[EXPERTS_PER_GROUP * g:EXPERTS_PER_GROUP * (g + 1), :] for g in range(N_EXPERT_GROUPS)]
    grp = []
    for r in rows:
        m1 = jnp.max(r, axis=0, keepdims=True)
        eq = r == m1
        n_eq = jnp.sum(jnp.where(eq, 1.0, 0.0), axis=0, keepdims=True)
        m2 = jnp.max(jnp.where(eq, -jnp.inf, r), axis=0, keepdims=True)
        grp.append(m1 + jnp.where(n_eq >= 2.0, m1, m2))
    masked = []
    for g in range(N_EXPERT_GROUPS):
        rank = jnp.zeros_like(grp[g])
        for g2 in range(N_EXPERT_GROUPS):
            if g2 == g:
                continue
            beats = (grp[g2] >= grp[g]) if g2 < g else (grp[g2] > grp[g])
            rank = rank + jnp.where(beats, 1.0, 0.0)
        masked.append(jnp.where(rank < float(TOPK_GROUPS), rows[g], -jnp.inf))
    masked = jnp.concatenate(masked, axis=0)
    eidx = lax.broadcasted_iota(i32, (N_EXPERTS, 1), 0)
    rank = jnp.zeros_like(masked)
    for e2 in range(N_EXPERTS):
        row = masked[e2:e2 + 1, :]
        beats = (row > masked) | ((row == masked) & (eidx > e2))
        rank = rank + jnp.where(beats, 1.0, 0.0)
    return s, rank, rank < float(EXPERT_TOPK)


def _route_count_kernel(lg_ref, bias_ref, cnt_ref):
    @pl.when(pl.program_id(0) == 0)
    def _():
        cnt_ref[...] = jnp.zeros(cnt_ref.shape, f32)

    _, _, selected = _route_select(lg_ref, bias_ref)
    cnt_ref[...] += jnp.sum(jnp.where(selected, 1.0, 0.0), axis=1, keepdims=True)


def _route_assign_kernel(lg_ref, bias_ref, pstart_ref, gate_ref, pos_ref, run_scr):
    @pl.when(pl.program_id(0) == 0)
    def _():
        run_scr[...] = jnp.zeros(run_scr.shape, f32)

    tn = lg_ref.shape[1]
    s, rank, selected = _route_select(lg_ref, bias_ref)
    sel_f = jnp.where(selected, 1.0, 0.0)
    earlier = lax.broadcasted_iota(i32, (tn, tn), 0) < lax.broadcasted_iota(i32, (tn, tn), 1)
    prefix = jnp.dot(sel_f.astype(bf16), jnp.where(earlier, 1.0, 0.0).astype(bf16), preferred_element_type=f32)
    pos = pstart_ref[...] + run_scr[...] + prefix
    run_scr[...] += jnp.sum(sel_f, axis=1, keepdims=True)
    gate = jnp.where(selected, s, 0.0)
    gate = gate / jnp.sum(gate, axis=0, keepdims=True) * ROUTED_SCALE
    for j in range(EXPERT_TOPK):
        slot = rank == float(j)
        gate_ref[j:j + 1, :] = jnp.sum(jnp.where(slot, gate, 0.0), axis=0, keepdims=True)
        pos_ref[j:j + 1, :] = jnp.sum(jnp.where(slot, pos, 0.0), axis=0, keepdims=True).astype(i32)


def _route_counts(logits_t, bias_col):
    e, n = logits_t.shape
    tn = min(512, n)
    return pl.pallas_call(
        _route_count_kernel,
        grid=(n // tn,),
        in_specs=[pl.BlockSpec((e, tn), lambda t: (0, t)),
                  pl.BlockSpec((e, 1), lambda t: (0, 0))],
        out_specs=pl.BlockSpec((e, 1), lambda t: (0, 0)),
        out_shape=jax.ShapeDtypeStruct((e, 1), f32),
        compiler_params=_cparams(1),
        name="route_count",
    )(logits_t, bias_col)


def _route_assign(logits_t, bias_col, pstart_col):
    e, n = logits_t.shape
    tn = min(512, n)
    return pl.pallas_call(
        _route_assign_kernel,
        grid=(n // tn,),
        in_specs=[pl.BlockSpec((e, tn), lambda t: (0, t)),
                  pl.BlockSpec((e, 1), lambda t: (0, 0)),
                  pl.BlockSpec((e, 1), lambda t: (0, 0))],
        out_specs=[pl.BlockSpec((EXPERT_TOPK, tn), lambda t: (0, t)),
                   pl.BlockSpec((EXPERT_TOPK, tn), lambda t: (0, t))],
        out_shape=[jax.ShapeDtypeStruct((EXPERT_TOPK, n), f32),
                   jax.ShapeDtypeStruct((EXPERT_TOPK, n), i32)],
        scratch_shapes=[pltpu.VMEM((e, 1), f32)],
        compiler_params=_cparams(1),
        name="route_assign",
    )(logits_t, bias_col, pstart_col)


def _dispatch_kernel(pos_ref, pad_tile_ref, h_ref, x1_ref, mod_ref, ws1_ref, ws3_ref, ws2_ref,
                     xs_ref, base_ref, zero_scr, sem, fill_sem, *, n_tokens, tn, s):
    base = pl.program_id(0) * tn
    tile_slabs = EXPERT_TILE * s

    @pl.when(pl.program_id(0) == 0)
    def _zero_padded_tiles():
        zero_scr[...] = jnp.zeros(zero_scr.shape, zero_scr.dtype)

        def fill(e):
            first = pl.multiple_of(pad_tile_ref[e] * s, tile_slabs)
            return pltpu.make_async_copy(zero_scr, xs_ref.at[pl.ds(first, tile_slabs), :], fill_sem)

        def start(e, carry):
            @pl.when(pad_tile_ref[e] >= 0)
            def _():
                fill(e).start()
            return carry

        def wait(e, carry):
            @pl.when(pad_tile_ref[e] >= 0)
            def _():
                fill(e).wait()
            return carry

        def tail(tile):
            first = pl.multiple_of(tile * tile_slabs, tile_slabs)
            return pltpu.make_async_copy(zero_scr, xs_ref.at[pl.ds(first, tile_slabs), :], fill_sem)

        def start_tail(tile, carry):
            tail(tile).start()
            return carry

        def wait_tail(tile, carry):
            tail(tile).wait()
            return carry

        n_tiles_total = xs_ref.shape[0] // tile_slabs
        lax.fori_loop(0, N_EXPERTS, start, 0)
        lax.fori_loop(pad_tile_ref[N_EXPERTS], n_tiles_total, start_tail, 0)
        lax.fori_loop(0, N_EXPERTS, wait, 0)
        lax.fori_loop(pad_tile_ref[N_EXPERTS], n_tiles_total, wait_tail, 0)

    def row_copy(t, p):
        return pltpu.make_async_copy(h_ref.at[pl.ds(pl.multiple_of(t * s, s), s), :],
                                     xs_ref.at[pl.ds(pl.multiple_of(p * s, s), s), :], sem)

    def issue(t, carry):
        for j in range(EXPERT_TOPK):
            row_copy(t, pos_ref[j * n_tokens + base + t]).start(priority=j % 2)
        return carry

    lax.fori_loop(0, tn, issue, 0)

    hb = _slabs_to_rows(h_ref, tn).astype(bf16)
    act = (_silu(jnp.dot(hb, ws1_ref[...], preferred_element_type=f32))
           * jnp.dot(hb, ws3_ref[...], preferred_element_type=f32)).astype(bf16)
    shared = jnp.dot(act, ws2_ref[...], preferred_element_type=f32)
    base_ref[...] = x1_ref[...] + mod_ref[0, 5:6, :] * shared

    for j in range(EXPERT_TOPK):
        pltpu.make_async_copy(h_ref, xs_ref.at[pl.ds(0, tn * s), :], sem).wait()


def _dispatch(pos_flat, pad_tile, h2_slabs, x1, mod8, ws1b, ws3b, ws2b, n_rows, seq):
    n, d = x1.shape
    s = d // LANE
    hdim = ws1b.shape[1]
    tn = min(256, seq)
    per_seq = seq // tn
    kern = functools.partial(_dispatch_kernel, n_tokens=n, tn=tn, s=s)
    return pl.pallas_call(
        kern,
        grid=(n // tn,),
        in_specs=[pl.BlockSpec(memory_space=pltpu.SMEM),
                  pl.BlockSpec(memory_space=pltpu.SMEM),
                  pl.BlockSpec((tn * s, LANE), lambda i: (i, 0)),
                  pl.BlockSpec((tn, d), lambda i: (i, 0)),
                  pl.BlockSpec((1, MOD_ROWS, d), lambda i: (i // per_seq, 0, 0)),
                  pl.BlockSpec((d, hdim), lambda i: (0, 0)),
                  pl.BlockSpec((d, hdim), lambda i: (0, 0)),
                  pl.BlockSpec((hdim, d), lambda i: (0, 0))],
        out_specs=[pl.BlockSpec(memory_space=pl.ANY),
                   pl.BlockSpec((tn, d), lambda i: (i, 0))],
        out_shape=[jax.ShapeDtypeStruct((n_rows * s, LANE), f32),
                   jax.ShapeDtypeStruct((n, d), f32)],
        scratch_shapes=[pltpu.VMEM((EXPERT_TILE * s, LANE), f32), pltpu.SemaphoreType.DMA(()),
                        pltpu.SemaphoreType.DMA(())],
        compiler_params=_cparams(1),
        name="dispatch",
    )(pos_flat, pad_tile, h2_slabs, x1, mod8, ws1b, ws3b, ws2b)


def _expert_kernel(ord_ref, ue_ref, nu_ref, xs_hbm, w1_hbm, w3_hbm, w2_hbm, ys_hbm,
                   xbuf, ybuf, w13f, w2f, w1b, w3b, w2b, sems, xsems, ysems):
    i = pl.program_id(0)
    k = ord_ref[i]
    n_tiles = nu_ref[0]
    tb = xbuf.shape[1]

    def tile_rows(tile):
        return pl.ds(pl.multiple_of(tile * tb, tb), tb)

    def x_copy(tile):
        slot = tile % EXPERT_IN_SLOTS
        return pltpu.make_async_copy(xs_hbm.at[tile_rows(tile), :], xbuf.at[slot], xsems.at[slot])

    def y_copy(tile):
        slot = tile % EXPERT_OUT_SLOTS
        return pltpu.make_async_copy(ybuf.at[slot], ys_hbm.at[tile_rows(tile), :], ysems.at[slot])

    def fetch(kk, slot):
        e = ue_ref[kk]
        return (pltpu.make_async_copy(w1_hbm.at[e], w13f.at[slot, 0], sems.at[slot, 0]),
                pltpu.make_async_copy(w3_hbm.at[e], w13f.at[slot, 1], sems.at[slot, 1]),
                pltpu.make_async_copy(w2_hbm.at[e], w2f.at[slot], sems.at[slot, 2]))

    @pl.when(i == 0)
    def _prologue():
        for cp in fetch(0, 0):
            cp.start()
        for tile in range(EXPERT_IN_SLOTS - 1):
            @pl.when(tile < n_tiles)
            def _():
                x_copy(tile).start()

    def compute_tile(cast_slot):
        xs_ref = xbuf.at[i % EXPERT_IN_SLOTS]
        ys_ref = ybuf.at[i % EXPERT_OUT_SLOTS]
        t = EXPERT_TILE
        s = tb // t
        kc = 2 * LANE
        h1 = jnp.zeros((t, w1b.shape[1]), f32)
        h3 = jnp.zeros((t, w1b.shape[1]), f32)
        for c in range(s // 2):
            rows = slice(c * kc, (c + 1) * kc)
            if cast_slot is not None:
                w1b[rows, :] = w13f[cast_slot, 0, rows, :].astype(bf16)
                w3b[rows, :] = w13f[cast_slot, 1, rows, :].astype(bf16)
            xc = jnp.concatenate([xs_ref[pl.ds(2 * c, t, stride=s), :],
                                  xs_ref[pl.ds(2 * c + 1, t, stride=s), :]], axis=1).astype(bf16)
            h1 = h1 + jnp.dot(xc, w1b[rows, :], preferred_element_type=f32)
            h3 = h3 + jnp.dot(xc, w3b[rows, :], preferred_element_type=f32)
        a = (_silu(h1) * h3).astype(bf16)
        for c in range(s // 2):
            cols = slice(c * kc, (c + 1) * kc)
            if cast_slot is not None:
                w2b[:, cols] = w2f[cast_slot, :, cols].astype(bf16)
            yc = jnp.dot(a, w2b[:, cols], preferred_element_type=f32)
            ys_ref[pl.ds(2 * c, t, stride=s), :] = yc[:, 0:LANE]
            ys_ref[pl.ds(2 * c + 1, t, stride=s), :] = yc[:, LANE:kc]

    def run_tile(cast_slot):
        @pl.when(i + EXPERT_IN_SLOTS - 1 < n_tiles)
        def _():
            x_copy(i + EXPERT_IN_SLOTS - 1).start()

        x_copy(i).wait()

        @pl.when(i >= EXPERT_OUT_SLOTS)
        def _():
            y_copy(i - EXPERT_OUT_SLOTS).wait()

        compute_tile(cast_slot)
        y_copy(i).start()

        @pl.when(i == n_tiles - 1)
        def _drain():
            for back in range(EXPERT_OUT_SLOTS - 1, -1, -1):
                @pl.when(i >= back)
                def _():
                    y_copy(i - back).wait()

    first_of_expert = (i == 0) | (k != ord_ref[jnp.maximum(i - 1, 0)])

    @pl.when((i < n_tiles) & first_of_expert)
    def _first_tile():
        slot = k % 2
        for cp in fetch(k, slot):
            cp.wait()

        @pl.when(k + 1 < nu_ref[1])
        def _prefetch():
            for cp in fetch(k + 1, 1 - slot):
                cp.start(priority=1)

        run_tile(slot)

    @pl.when((i < n_tiles) & jnp.logical_not(first_of_expert))
    def _later_tile():
        run_tile(None)

    @pl.when(i >= n_tiles)
    def _unused():
        def y_zero(tile):
            return pltpu.make_async_copy(ybuf.at[0], ys_hbm.at[tile_rows(tile), :], ysems.at[0])

        @pl.when(i == n_tiles)
        def _():
            ybuf[0] = jnp.zeros(ybuf.shape[1:], ybuf.dtype)

        y_zero(i).start()

        @pl.when(i == pl.num_programs(0) - 1)
        def _():
            def wait_one(tile, carry):
                y_zero(tile).wait()
                return carry

            lax.fori_loop(n_tiles, pl.num_programs(0), wait_one, 0)


def _experts(tile_ord, used_experts, n_used, xs, w1, w3, w2):
    _, d, hdim = w1.shape
    tb = EXPERT_TILE * d // LANE
    p, w = xs.shape
    grid_spec = pltpu.PrefetchScalarGridSpec(
        num_scalar_prefetch=3,
        grid=(p // tb,),
        in_specs=[pl.BlockSpec(memory_space=pl.ANY)] * 4,
        out_specs=pl.BlockSpec(memory_space=pl.ANY),
        scratch_shapes=[pltpu.VMEM((EXPERT_IN_SLOTS, tb, w), f32), pltpu.VMEM((EXPERT_OUT_SLOTS, tb, w), f32),
                        pltpu.VMEM((2, 2, d, hdim), f32), pltpu.VMEM((2, hdim, d), f32),
                        pltpu.VMEM((d, hdim), bf16), pltpu.VMEM((d, hdim), bf16), pltpu.VMEM((hdim, d), bf16),
                        pltpu.SemaphoreType.DMA((2, 3)), pltpu.SemaphoreType.DMA((EXPERT_IN_SLOTS,)),
                        pltpu.SemaphoreType.DMA((EXPERT_OUT_SLOTS,))],
    )
    return pl.pallas_call(
        _expert_kernel,
        grid_spec=grid_spec,
        out_shape=jax.ShapeDtypeStruct((p, w), f32),
        compiler_params=_cparams(1),
        name="experts",
    )(tile_ord, used_experts, n_used, xs, w1, w3, w2)


def _combine_kernel(pos_ref, gate_ref, base_ref, mod_ref, ys_ref, o_ref, gbuf, r_scr, sems, *, n_tokens, tn):
    step = pl.program_id(0)
    n_steps = pl.num_programs(0)
    slot = step % 2

    def gather_tile(s):
        base = s * tn
        sl = s % 2

        def issue(t, carry):
            for j in range(EXPERT_TOPK):
                p = pos_ref[j * n_tokens + base + t]
                pltpu.make_async_copy(ys_ref.at[p], gbuf.at[sl, j, t], sems.at[sl]).start(priority=j % 2)
            return carry

        lax.fori_loop(0, tn, issue, 0)

    @pl.when(step == 0)
    def _():
        gather_tile(step)

    @pl.when(step + 1 < n_steps)
    def _():
        gather_tile(step + 1)

    for j in range(EXPERT_TOPK):
        pltpu.make_async_copy(ys_ref.at[pl.ds(0, tn)], gbuf.at[slot, j], sems.at[slot]).wait()

    routed = gate_ref[:, 0:1, :] * gbuf[slot, 0]
    for j in range(1, EXPERT_TOPK):
        routed = routed + gate_ref[:, j:j + 1, :] * gbuf[slot, j]
    r_scr[...] = routed.reshape(r_scr.shape)
    o_ref[...] = base_ref[...] + mod_ref[0, 5:6, :] * _slabs_to_rows(r_scr, tn)


def _combine(pos_flat, gate_rep, base, mod8, ys, seq):
    n, d = base.shape
    s = d // LANE
    tn = min(128, seq)
    per_seq = seq // tn
    kern = functools.partial(_combine_kernel, n_tokens=n, tn=tn)
    return pl.pallas_call(
        kern,
        grid=(n // tn,),
        in_specs=[pl.BlockSpec(memory_space=pltpu.SMEM),
                  pl.BlockSpec((tn, EXPERT_TOPK, LANE), lambda i: (i, 0, 0)),
                  pl.BlockSpec((tn, d), lambda i: (i, 0)),
                  pl.BlockSpec((1, MOD_ROWS, d), lambda i: (i // per_seq, 0, 0)),
                  pl.BlockSpec(memory_space=pl.ANY)],
        out_specs=pl.BlockSpec((tn, d), lambda i: (i, 0)),
        out_shape=jax.ShapeDtypeStruct((n, d), f32),
        scratch_shapes=[pltpu.VMEM((2, EXPERT_TOPK, tn, s, LANE), f32), pltpu.VMEM((tn * s, LANE), f32),
                        pltpu.SemaphoreType.DMA((2,))],
        compiler_params=_cparams(1),
        name="combine",
    )(pos_flat, gate_rep, base, mod8, ys)


def _layer(x, c, w_ada, b_ada, g_mix, w_in, q_norm, k_norm, w_attn_up, pool_lin, pool_scale, w_pool_up,
           w_out, g_ffn, w_router, router_bias, w1, w3, w2, ws1, ws3, ws2):
    batch, seq, d = x.shape
    n = batch * seq
    x2 = x.reshape(n, d)

    offs = [0, ATTN_WIDTH, ATTN_WIDTH + KV_WIDTH, ATTN_WIDTH + 2 * KV_WIDTH]
    w_q = w_in[:, offs[0]:offs[1]]
    w_k = w_in[:, offs[1]:offs[2]]
    w_v = w_in[:, offs[2]:offs[3]]
    o_qi = offs[3]
    w_qi = w_in[:, o_qi:o_qi + IDX_HEADS * IDX_DIM]
    o_ki = o_qi + IDX_HEADS * IDX_DIM
    w_ki = w_in[:, o_ki:o_ki + IDX_DIM]
    o_wi = o_ki + IDX_DIM
    w_wi = w_in[:, o_wi:o_wi + IDX_HEADS]
    o_u = o_wi + IDX_HEADS
    w_u = w_in[:, o_u:o_u + POOL_WIDTH]
    o_g = o_u + POOL_WIDTH
    w_g = w_in[:, o_g:o_g + 2 * d]
    w_main = jnp.concatenate([w_g, w_q, w_qi, w_u, w_k, w_v], axis=1).astype(bf16)
    w_kiwi = jnp.concatenate(
        [w_ki, w_wi, jnp.zeros((d, KIWI_WIDTH - IDX_DIM - IDX_HEADS), w_in.dtype)], axis=1).astype(bf16)

    assert batch <= SUBLANES and seq % min(KEY_CHUNK, seq) == 0 and d % (2 * LANE) == 0
    c8 = jnp.zeros((SUBLANES, d), f32).at[:batch].set(c)
    mod = _ada(c8, w_ada, b_ada.reshape(1, -1))[:batch]
    mod8 = jnp.zeros((batch, MOD_ROWS, d), f32).at[:, :N_MOD].set(mod.reshape(batch, N_MOD, d))

    proj, kiwi = _inproj(x2, mod8, g_mix.reshape(1, d), w_main, w_kiwi, seq)
    attn = _attention(proj, kiwi, q_norm.reshape(1, -1), k_norm.reshape(1, -1), batch, seq)
    pool = _pool(proj, pool_lin.astype(bf16), pool_scale.reshape(1, -1), seq)
    merged = _merge(attn, pool, w_attn_up.astype(bf16), w_pool_up.astype(bf16), proj)
    wr_hi = w_router.astype(bf16)
    wr_lo = (w_router - wr_hi.astype(f32)).astype(bf16)
    x1, h2, logits = _outproj(merged, x2, mod8, g_ffn.reshape(1, d), w_out.astype(bf16),
                              jnp.concatenate([wr_hi, wr_lo], axis=1), seq)

    logits_t = logits.T
    bias_col = router_bias.reshape(N_EXPERTS, 1)
    counts = _route_counts(logits_t, bias_col)[:, 0].astype(i32)
    t = EXPERT_TILE
    tiles_e = (counts + t - 1) // t
    tile_end = jnp.cumsum(tiles_e)
    pstart = ((tile_end - tiles_e) * t).astype(f32).reshape(N_EXPERTS, 1)
    n_tiles = n * EXPERT_TOPK // t + N_EXPERTS
    n_used = tile_end[-1]
    tile_ids = jnp.minimum(jnp.arange(n_tiles, dtype=i32), n_used - 1)
    block_e = jnp.sum((tile_end[None, :] <= tile_ids[:, None]).astype(i32), axis=1)
    block_e = jnp.minimum(block_e, N_EXPERTS - 1)
    pad_tile = jnp.where(tiles_e > 0, (tile_end - 1) * t, -1).astype(i32)
    pad_tile = jnp.concatenate([pad_tile, n_used.reshape(1).astype(i32)])
    gate8, pos8 = _route_assign(logits_t, bias_col, pstart)
    pos_flat = pos8.reshape(-1)

    cum_used = jnp.cumsum((tiles_e > 0).astype(i32))
    slots = jnp.arange(N_EXPERTS, dtype=i32)
    used_experts = jnp.minimum(jnp.sum((cum_used[None, :] <= slots[:, None]).astype(i32), axis=1), N_EXPERTS - 1)
    tile_ord = jnp.sum(jnp.where(block_e[:, None] == slots[None, :], cum_used[None, :] - 1, 0), axis=1).astype(i32)
    n_used2 = jnp.stack([n_used, cum_used[-1]]).astype(i32)

    slabs = d // LANE
    xs, base = _dispatch(pos_flat, pad_tile, h2, x1, mod8, ws1.astype(bf16), ws3.astype(bf16), ws2.astype(bf16),
                         n_tiles * t, seq)
    ys = _experts(tile_ord, used_experts, n_used2, xs, w1, w3, w2)
    gate_rep = jnp.broadcast_to(gate8.T[:, :, None], (n, EXPERT_TOPK, LANE))
    out = _combine(pos_flat, gate_rep, base, mod8, ys.reshape(-1, slabs, LANE), seq)
    return out.reshape(batch, seq, d)


def kernel(x, c, w_ada, b_ada, g_mix, w_in, q_norm, k_norm, w_attn_up, pool_lin, pool_scale, w_pool_up, w_out, g_ffn, w_router, router_bias, w1, w3, w2, ws1, ws3, ws2):
    for l in range(w_ada.shape[0]):
        x = _layer(x, c, w_ada[l], b_ada[l], g_mix[l], w_in[l], q_norm[l], k_norm[l], w_attn_up[l], pool_lin[l],
                   pool_scale[l], w_pool_up[l], w_out[l], g_ffn[l], w_router[l], router_bias[l], w1[l], w3[l],
                   w2[l], ws1[l], ws3[l], ws2[l])
    return x
```

```python
import functools

import jax
import jax.numpy as jnp
from jax import lax
from jax.experimental import pallas as pl
from jax.experimental.pallas import tpu as pltpu

f32 = jnp.float32
bf16 = jnp.bfloat16
i32 = jnp.int32

N_HEADS = 8
HEAD_DIM = 128
N_KV_HEADS = 2
Q_PER_KV = N_HEADS // N_KV_HEADS
ATTN_WIDTH = N_HEADS * HEAD_DIM
KV_WIDTH = N_KV_HEADS * HEAD_DIM
IDX_HEADS = 16
IDX_DIM = 64
IDX_TOPK_MAX = 256
Q_BLOCK = 128
LANE = 128
POOL_WINDOWS = (2, 4, 8, 16)
POOL_GROUP_DIM = 256
POOL_WIDTH = 1024
POOL_HALO = 16
N_EXPERTS = 64
N_EXPERT_GROUPS = 8
EXPERTS_PER_GROUP = 8
TOPK_GROUPS = 4
EXPERT_TOPK = 8
EXPERT_HIDDEN = 512
ROUTED_SCALE = 2.5
NORM_EPS = 1e-6
N_MOD = 6
SUBLANES = 8
MOD_ROWS = SUBLANES

COL_GATE_A = 0
COL_GATE_P = 2048
COL_Q = 4096
COL_QI = 5120
COL_U = 6144
COL_K = 7168
COL_V = 7424
MAIN_WIDTH = 7680
KIWI_WIDTH = 128

KEY_CHUNK = 512
ATTN_PIECE = 256
SEARCH_CHECK_FROM = 24
SEARCH_GROUP = 2
EXPERT_TILE = 256
EXPERT_IN_SLOTS = 3
EXPERT_OUT_SLOTS = 2
VMEM_LIMIT = 56 * 1024 * 1024
INT_MIN = -2147483648
KEY_OF_LOWEST_FINITE = -2139095040
NEG_BIG = -1e30
LOG2E = 1.4426950408889634


def _alibi_slope(h):
    return 2.0 ** (-8.0 * (h + 1) / N_HEADS)


def _cparams(n_axes, vmem=VMEM_LIMIT):
    return pltpu.CompilerParams(dimension_semantics=("arbitrary",) * n_axes, vmem_limit_bytes=vmem)


def _sigmoid(x):
    return 1.0 / (1.0 + jnp.exp(-x))


def _silu(x):
    return x * _sigmoid(x)


def _rows_to_slabs(ref, val):
    rows, width = val.shape
    s = width // LANE
    for j in range(s):
        ref[pl.ds(j, rows, stride=s), :] = val[:, j * LANE:(j + 1) * LANE]


def _slabs_to_rows(ref, rows):
    s = ref.shape[0] // rows
    return jnp.concatenate([ref[pl.ds(j, rows, stride=s), :] for j in range(s)], axis=1)


def _ada_kernel(c_ref, w_ref, b_ref, o_ref):
    sc = _silu(c_ref[...]).astype(bf16)
    o_ref[...] = jnp.dot(sc, w_ref[...].astype(bf16), preferred_element_type=f32) + b_ref[...]


def _ada(c8, w_ada, b_ada):
    d, n = w_ada.shape
    tn = 1024
    return pl.pallas_call(
        _ada_kernel,
        grid=(n // tn,),
        in_specs=[pl.BlockSpec((SUBLANES, d), lambda j: (0, 0)),
                  pl.BlockSpec((d, tn), lambda j: (0, j)),
                  pl.BlockSpec((1, tn), lambda j: (0, j))],
        out_specs=pl.BlockSpec((SUBLANES, tn), lambda j: (0, j)),
        out_shape=jax.ShapeDtypeStruct((SUBLANES, n), f32),
        compiler_params=_cparams(1),
        name="adaln",
    )(c8, w_ada, b_ada)


def _inproj_kernel(x_ref, mod_ref, g_ref, w_ref, wk_ref, o_ref, kiwi_ref, h_scr):
    @pl.when(pl.program_id(1) == 0)
    def _():
        x = x_ref[...]
        y = x * lax.rsqrt(jnp.mean(x * x, axis=-1, keepdims=True) + NORM_EPS) * g_ref[...]
        h = y * (1.0 + mod_ref[0, 1:2, :]) + mod_ref[0, 0:1, :]
        hb = h.astype(bf16)
        h_scr[...] = hb
        kiwi_ref[...] = jnp.dot(hb, wk_ref[...], preferred_element_type=f32)

    o_ref[...] = jnp.dot(h_scr[...], w_ref[...], preferred_element_type=f32).astype(o_ref.dtype)


def _inproj(x2, mod8, g_mix, w_main, w_kiwi, seq):
    n, d = x2.shape
    tm, tn = 1024, 1536
    tm = min(tm, seq)
    per_seq = seq // tm
    return pl.pallas_call(
        _inproj_kernel,
        grid=(n // tm, MAIN_WIDTH // tn),
        in_specs=[pl.BlockSpec((tm, d), lambda i, j: (i, 0)),
                  pl.BlockSpec((1, MOD_ROWS, d), lambda i, j: (i // per_seq, 0, 0)),
                  pl.BlockSpec((1, d), lambda i, j: (0, 0)),
                  pl.BlockSpec((d, tn), lambda i, j: (0, j)),
                  pl.BlockSpec((d, KIWI_WIDTH), lambda i, j: (0, 0))],
        out_specs=[pl.BlockSpec((tm, tn), lambda i, j: (i, j)),
                   pl.BlockSpec((tm, KIWI_WIDTH), lambda i, j: (i, 0))],
        out_shape=[jax.ShapeDtypeStruct((n, MAIN_WIDTH), bf16),
                   jax.ShapeDtypeStruct((n, KIWI_WIDTH), f32)],
        scratch_shapes=[pltpu.VMEM((tm, d), bf16)],
        compiler_params=_cparams(2),
        name="inproj",
    )(x2, mod8, g_mix, w_main, w_kiwi)


def _tree_sum(parts):
    while len(parts) > 1:
        parts = [parts[a] + parts[a + 1] for a in range(0, len(parts) - 1, 2)] + ([parts[-1]] if len(parts) % 2 else [])
    return parts[0]


def _attn_t_kernel(q_ref, qi_ref, k_ref, v_ref, kiwi_all_ref, kiwi_blk_ref, qn_ref, kn_ref, o_ref,
                   kn_scr, ki_scr, vt_scr, score_scr, qt_scr, qit_scr, bias_scr, m_scr, l_scr, acc_scr, *, seq, topk):
    i = pl.program_id(1)
    ck = min(KEY_CHUNK, seq)
    n_chunks = i // (ck // Q_BLOCK) + 1
    heads_per_dot = 4

    @pl.when(i == 0)
    def _prep_keys():
        for n in range(N_KV_HEADS):
            kf = k_ref[:, n * HEAD_DIM:(n + 1) * HEAD_DIM].astype(f32)
            r = lax.rsqrt(jnp.mean(kf * kf, axis=-1, keepdims=True) + NORM_EPS)
            kn_scr[:, n * HEAD_DIM:(n + 1) * HEAD_DIM] = (kf * r * kn_ref[...]).astype(bf16)
        ki_scr[...] = kiwi_all_ref[:, 0:IDX_DIM].astype(bf16)
        for c in range(seq // ck):
            vt_scr[c] = v_ref[c * ck:(c + 1) * ck, :].astype(f32).T.astype(bf16)
        rel = (lax.broadcasted_iota(i32, (ck, Q_BLOCK), 1) - lax.broadcasted_iota(i32, (ck, Q_BLOCK), 0)).astype(f32)
        for h in range(N_HEADS):
            bias_scr[h] = rel * (-_alibi_slope(h) * LOG2E)

    q_t = []
    for h in range(N_HEADS):
        qf = q_ref[:, h * HEAD_DIM:(h + 1) * HEAD_DIM].astype(f32)
        r = lax.rsqrt(jnp.mean(qf * qf, axis=-1, keepdims=True) + NORM_EPS)
        q_t.append((qf * r * qn_ref[...] * (HEAD_DIM ** -0.5 * LOG2E)).T)
    for n in range(N_KV_HEADS):
        qt_scr[n] = jnp.concatenate(q_t[n * Q_PER_KV:(n + 1) * Q_PER_KV], axis=1).astype(bf16)
    qi_t = qi_ref[...].astype(f32).T
    for a in range(IDX_HEADS // heads_per_dot):
        qit_scr[a] = jnp.concatenate(
            [qi_t[(a * heads_per_dot + b) * IDX_DIM:(a * heads_per_dot + b + 1) * IDX_DIM, :]
             for b in range(heads_per_dot)], axis=1).astype(bf16)
    wi_t = kiwi_blk_ref[...].T[IDX_DIM:IDX_DIM + IDX_HEADS, :] * (IDX_HEADS ** -0.5 * IDX_DIM ** -0.5)
    qpos = i * Q_BLOCK + lax.broadcasted_iota(i32, (1, Q_BLOCK), 1)

    def index_chunk(c, carry):
        start = pl.multiple_of(c * ck, ck)
        kc = ki_scr[pl.ds(start, ck), :]
        acc = jnp.zeros((ck, Q_BLOCK), f32)
        for a in range(IDX_HEADS // heads_per_dot):
            d = jnp.dot(kc, qit_scr[a], preferred_element_type=f32)
            for b in range(heads_per_dot):
                h = a * heads_per_dot + b
                acc = acc + jnp.maximum(d[:, b * Q_BLOCK:(b + 1) * Q_BLOCK], 0.0) * wi_t[h:h + 1, :]
        kpos = start + lax.broadcasted_iota(i32, (ck, 1), 0)
        score_scr[c] = jnp.where(kpos <= qpos, acc, -jnp.inf)
        return carry

    lax.fori_loop(0, n_chunks, index_chunk, 0)

    def key_to_float(key):
        bits = key ^ (lax.shift_right_arithmetic(key, 31) & jnp.int32(0x7FFFFFFF))
        return lax.bitcast_convert_type(bits, f32)

    def bit_step(b, state):
        t_u, kept = state
        bit = lax.shift_left(jnp.int32(1), 31 - b)
        cand_u = t_u | bit
        cand = key_to_float(cand_u ^ jnp.int32(INT_MIN))

        def count_chunk(c, cnt):
            ge = jnp.where(score_scr[c] >= cand, 1.0, 0.0)
            return cnt + _tree_sum([ge[s * 8:(s + 1) * 8, :] for s in range(ck // 8)])

        cnt = lax.fori_loop(0, n_chunks, count_chunk, jnp.zeros((8, Q_BLOCK), f32))
        total = jnp.sum(cnt, axis=0, keepdims=True)
        accept = total >= float(topk)
        return jnp.where(accept, cand_u, t_u), jnp.where(accept, total, kept)

    state = lax.fori_loop(0, SEARCH_CHECK_FROM, bit_step,
                          (jnp.zeros((1, Q_BLOCK), i32), jnp.full((1, Q_BLOCK), float(seq + 1), f32)))

    def settled(kept):
        ok = (kept == float(topk)) | (qpos < topk)
        return jnp.min(jnp.where(ok, 1.0, 0.0)) > 0.0

    def more_bits(loop):
        b, _, done = loop
        return (b < 32) & jnp.logical_not(done)

    def bit_group(loop):
        b, st, _ = loop
        st = lax.fori_loop(b, b + SEARCH_GROUP, bit_step, st)
        return b + SEARCH_GROUP, st, settled(st[1])

    _, (t_u, _), _ = lax.while_loop(more_bits, bit_group, (jnp.int32(SEARCH_CHECK_FROM), state, settled(state[1])))
    thr = key_to_float(jnp.maximum(t_u ^ jnp.int32(INT_MIN), jnp.int32(KEY_OF_LOWEST_FINITE)))

    def count_ties(c, carry):
        ge, gt = carry
        sc = score_scr[c]
        ge = ge + _tree_sum([jnp.where(sc[s * 8:(s + 1) * 8, :] >= thr, 1.0, 0.0) for s in range(ck // 8)])
        gt = gt + _tree_sum([jnp.where(sc[s * 8:(s + 1) * 8, :] > thr, 1.0, 0.0) for s in range(ck // 8)])
        return ge, gt

    zeros8 = jnp.zeros((8, Q_BLOCK), f32)
    ge8, gt8 = lax.fori_loop(0, n_chunks, count_ties, (zeros8, zeros8))
    excess = jnp.sum(ge8, axis=0, keepdims=True) > float(topk)
    need = float(topk) - jnp.sum(gt8, axis=0, keepdims=True)

    @pl.when(jnp.max(jnp.where(excess, 1.0, 0.0)) > 0.0)
    def _break_ties():
        pos_bits = (seq - 1).bit_length()

        def key_positions(c):
            return c * ck + lax.broadcasted_iota(i32, (ck, Q_BLOCK), 0)

        def pos_step(b, q):
            cand = q | lax.shift_left(jnp.int32(1), pos_bits - 1 - b)

            def count_before(c, cnt):
                hit = jnp.where((score_scr[c] == thr) & (key_positions(c) < cand), 1.0, 0.0)
                return cnt + _tree_sum([hit[s * 8:(s + 1) * 8, :] for s in range(ck // 8)])

            before = jnp.sum(lax.fori_loop(0, n_chunks, count_before, zeros8), axis=0, keepdims=True)
            return jnp.where(before < need, cand, q)

        q = lax.fori_loop(0, pos_bits, pos_step, jnp.zeros((1, Q_BLOCK), i32))
        q = jnp.where(excess, q, jnp.int32(seq))

        def demote(c, carry):
            sc = score_scr[c]
            score_scr[c] = jnp.where((sc == thr) & (key_positions(c) > q), -jnp.inf, sc)
            return carry

        lax.fori_loop(0, n_chunks, demote, 0)

    m_scr[...] = jnp.full(m_scr.shape, NEG_BIG, f32)
    l_scr[...] = jnp.zeros(l_scr.shape, f32)
    acc_scr[...] = jnp.zeros(acc_scr.shape, f32)
    lane_head = lax.broadcasted_iota(i32, (1, Q_PER_KV * Q_BLOCK), 1) // Q_BLOCK

    piece = min(ATTN_PIECE, ck)
    slope_rows = []
    for n in range(N_KV_HEADS):
        slope_row = jnp.zeros((1, Q_PER_KV * Q_BLOCK), f32)
        for g in range(Q_PER_KV):
            slope_row = jnp.where(lane_head == g, _alibi_slope(n * Q_PER_KV + g) * LOG2E, slope_row)
        slope_rows.append(slope_row)

    def attn_chunk(c, carry):
        start = pl.multiple_of(c * ck, ck)
        tile_dist = (i * Q_BLOCK - start).astype(f32)
        units = [(sub, n) for sub in range(ck // piece) for n in range(N_KV_HEADS)]

        def scores(sub, n):
            kc = kn_scr[pl.ds(pl.multiple_of(start + sub * piece, piece), piece), n * HEAD_DIM:(n + 1) * HEAD_DIM]
            return jnp.dot(kc, qt_scr[n], preferred_element_type=f32)

        ahead = 3
        z_of = {u: scores(*u) for u in units[:ahead]}
        for idx, (sub, n) in enumerate(units):
            r0, r1 = sub * piece, (sub + 1) * piece
            sel = score_scr[c, r0:r1, :] >= thr
            z_all = z_of.pop((sub, n))
            z = jnp.concatenate(
                [jnp.where(sel, z_all[:, g * Q_BLOCK:(g + 1) * Q_BLOCK] + bias_scr[n * Q_PER_KV + g, r0:r1, :],
                           NEG_BIG) for g in range(Q_PER_KV)], axis=1)
            off = slope_rows[n] * (-tile_dist)
            m_old = m_scr[n]
            m_new = jnp.maximum(m_old, jnp.max(z, axis=0, keepdims=True) + off)
            alpha = jnp.exp2(m_old - m_new)
            p = jnp.exp2(z + (off - m_new))
            l_scr[n] = alpha * l_scr[n] + jnp.sum(p, axis=0, keepdims=True)
            vt = vt_scr[c, n * HEAD_DIM:(n + 1) * HEAD_DIM, r0:r1]
            acc_scr[n] = alpha * acc_scr[n] + jnp.dot(vt, p.astype(bf16), preferred_element_type=f32)
            m_scr[n] = m_new
            if idx + ahead < len(units):
                z_of[units[idx + ahead]] = scores(*units[idx + ahead])
        return carry

    lax.fori_loop(0, n_chunks, attn_chunk, 0)
    for n in range(N_KV_HEADS):
        out_t = acc_scr[n] / l_scr[n]
        for g in range(Q_PER_KV):
            h = n * Q_PER_KV + g
            o_ref[:, h * HEAD_DIM:(h + 1) * HEAD_DIM] = out_t[:, g * Q_BLOCK:(g + 1) * Q_BLOCK].T.astype(o_ref.dtype)


def _attention(proj, kiwi, q_norm, k_norm, batch, seq):
    n = batch * seq
    nq = seq // Q_BLOCK
    ck = min(KEY_CHUNK, seq)
    topk = min(IDX_TOPK_MAX, seq // 4)
    kern = functools.partial(_attn_t_kernel, seq=seq, topk=topk)
    return pl.pallas_call(
        kern,
        grid=(batch, nq),
        in_specs=[pl.BlockSpec((Q_BLOCK, ATTN_WIDTH), lambda b, i: (b * nq + i, COL_Q // ATTN_WIDTH)),
                  pl.BlockSpec((Q_BLOCK, IDX_HEADS * IDX_DIM), lambda b, i: (b * nq + i, COL_QI // (IDX_HEADS * IDX_DIM))),
                  pl.BlockSpec((seq, KV_WIDTH), lambda b, i: (b, COL_K // KV_WIDTH)),
                  pl.BlockSpec((seq, KV_WIDTH), lambda b, i: (b, COL_V // KV_WIDTH)),
                  pl.BlockSpec((seq, KIWI_WIDTH), lambda b, i: (b, 0)),
                  pl.BlockSpec((Q_BLOCK, KIWI_WIDTH), lambda b, i: (b * nq + i, 0)),
                  pl.BlockSpec((1, HEAD_DIM), lambda b, i: (0, 0)),
                  pl.BlockSpec((1, HEAD_DIM), lambda b, i: (0, 0))],
        out_specs=pl.BlockSpec((Q_BLOCK, ATTN_WIDTH), lambda b, i: (b * nq + i, 0)),
        out_shape=jax.ShapeDtypeStruct((n, ATTN_WIDTH), bf16),
        scratch_shapes=[pltpu.VMEM((seq, KV_WIDTH), bf16),
                        pltpu.VMEM((seq, IDX_DIM), bf16),
                        pltpu.VMEM((seq // ck, KV_WIDTH, ck), bf16),
                        pltpu.VMEM((seq // ck, ck, Q_BLOCK), f32),
                        pltpu.VMEM((N_KV_HEADS, HEAD_DIM, Q_PER_KV * Q_BLOCK), bf16),
                        pltpu.VMEM((IDX_HEADS // 4, IDX_DIM, 4 * Q_BLOCK), bf16),
                        pltpu.VMEM((N_HEADS, ck, Q_BLOCK), f32),
                        pltpu.VMEM((N_KV_HEADS, 1, Q_PER_KV * Q_BLOCK), f32),
                        pltpu.VMEM((N_KV_HEADS, 1, Q_PER_KV * Q_BLOCK), f32),
                        pltpu.VMEM((N_KV_HEADS, HEAD_DIM, Q_PER_KV * Q_BLOCK), f32)],
        compiler_params=_cparams(2),
        name="sparse_attn",
    )(proj, proj, proj, proj, kiwi, kiwi, q_norm, k_norm)


def _pool_kernel(u_ref, halo_ref, lin_ref, ps_ref, o_ref, scr, *, tm, per_seq):
    i = pl.program_id(0)
    first = (i % per_seq) == 0
    scr[0:POOL_HALO, :] = jnp.where(first, 0.0, halo_ref[...].astype(f32))
    scr[POOL_HALO:POOL_HALO + tm, :] = u_ref[...].astype(f32)
    t_in_seq = (i % per_seq) * tm + lax.broadcasted_iota(i32, (tm, 1), 0)
    for g, w in enumerate(POOL_WINDOWS):
        c0, c1 = g * POOL_GROUP_DIM, (g + 1) * POOL_GROUP_DIM
        cur = scr[POOL_HALO:POOL_HALO + tm, c0:c1]
        s = cur
        for j in range(1, w):
            s = s + scr[POOL_HALO - j:POOL_HALO - j + tm, c0:c1]
        count = jnp.minimum(t_in_seq + 1, w).astype(f32)
        pooled = s / count - cur
        mixed = jnp.dot(pooled.astype(bf16), lin_ref[g], preferred_element_type=f32)
        o_ref[:, c0:c1] = (mixed * ps_ref[:, c0:c1]).astype(o_ref.dtype)


def _pool(proj, pool_lin_b, pool_scale, seq):
    n = proj.shape[0]
    tm = min(512, seq)
    per_seq = seq // tm
    hb = tm // POOL_HALO
    kern = functools.partial(_pool_kernel, tm=tm, per_seq=per_seq)
    return pl.pallas_call(
        kern,
        grid=(n // tm,),
        in_specs=[pl.BlockSpec((tm, POOL_WIDTH), lambda i: (i, COL_U // POOL_WIDTH)),
                  pl.BlockSpec((POOL_HALO, POOL_WIDTH), lambda i: (jnp.maximum(i * hb - 1, 0), COL_U // POOL_WIDTH)),
                  pl.BlockSpec((len(POOL_WINDOWS), POOL_GROUP_DIM, POOL_GROUP_DIM), lambda i: (0, 0, 0)),
                  pl.BlockSpec((1, POOL_WIDTH), lambda i: (0, 0))],
        out_specs=pl.BlockSpec((tm, POOL_WIDTH), lambda i: (i, 0)),
        out_shape=jax.ShapeDtypeStruct((n, POOL_WIDTH), bf16),
        scratch_shapes=[pltpu.VMEM((POOL_HALO + tm, POOL_WIDTH), f32)],
        compiler_params=_cparams(1),
        name="pool",
    )(proj, proj, pool_lin_b, pool_scale)


def _merge_kernel(a_ref, p_ref, wa_ref, wp_ref, ga_ref, gp_ref, o_ref):
    ya = jnp.dot(a_ref[...], wa_ref[...], preferred_element_type=f32)
    yp = jnp.dot(p_ref[...], wp_ref[...], preferred_element_type=f32)
    o = _sigmoid(ga_ref[...].astype(f32)) * ya + _sigmoid(gp_ref[...].astype(f32)) * yp
    o_ref[...] = o.astype(o_ref.dtype)


def _merge(attn, pool, w_au, w_pu, proj):
    n = attn.shape[0]
    d = w_au.shape[1]
    tm, tn = min(1024, n), 1024
    ga0, gp0 = COL_GATE_A // tn, COL_GATE_P // tn
    return pl.pallas_call(
        _merge_kernel,
        grid=(n // tm, d // tn),
        in_specs=[pl.BlockSpec((tm, ATTN_WIDTH), lambda i, j: (i, 0)),
                  pl.BlockSpec((tm, POOL_WIDTH), lambda i, j: (i, 0)),
                  pl.BlockSpec((ATTN_WIDTH, tn), lambda i, j: (0, j)),
                  pl.BlockSpec((POOL_WIDTH, tn), lambda i, j: (0, j)),
                  pl.BlockSpec((tm, tn), lambda i, j: (i, ga0 + j)),
                  pl.BlockSpec((tm, tn), lambda i, j: (i, gp0 + j))],
        out_specs=pl.BlockSpec((tm, tn), lambda i, j: (i, j)),
        out_shape=jax.ShapeDtypeStruct((n, d), bf16),
        compiler_params=_cparams(2),
        name="merge",
    )(attn, pool, w_au, w_pu, proj, proj)


def _outproj_kernel(m_ref, x_ref, mod_ref, g_ref, wo_ref, wr_ref, x1_ref, h2_ref, lg_ref):
    y = jnp.dot(m_ref[...], wo_ref[...], preferred_element_type=f32)
    x1 = x_ref[...] + mod_ref[0, 2:3, :] * y
    x1_ref[...] = x1
    hn = x1 * lax.rsqrt(jnp.mean(x1 * x1, axis=-1, keepdims=True) + NORM_EPS) * g_ref[...]
    h2 = hn * (1.0 + mod_ref[0, 4:5, :]) + mod_ref[0, 3:4, :]
    _rows_to_slabs(h2_ref, h2)
    h_hi = h2.astype(bf16)
    h_lo = (h2 - h_hi.astype(f32)).astype(bf16)
    a = jnp.dot(h_hi, wr_ref[...], preferred_element_type=f32)
    b = jnp.dot(h_lo, wr_ref[:, 0:N_EXPERTS], preferred_element_type=f32)
    lg_ref[...] = a[:, 0:N_EXPERTS] + (a[:, N_EXPERTS:2 * N_EXPERTS] + b)


def _outproj(merged, x2, mod8, g_ffn, w_out_b, w_router, seq):
    n, d = x2.shape
    tm = min(512, seq)
    per_seq = seq // tm
    return pl.pallas_call(
        _outproj_kernel,
        grid=(n // tm,),
        in_specs=[pl.BlockSpec((tm, d), lambda i: (i, 0)),
                  pl.BlockSpec((tm, d), lambda i: (i, 0)),
                  pl.BlockSpec((1, MOD_ROWS, d), lambda i: (i // per_seq, 0, 0)),
                  pl.BlockSpec((1, d), lambda i: (0, 0)),
                  pl.BlockSpec((d, d), lambda i: (0, 0)),
                  pl.BlockSpec((d, 2 * N_EXPERTS), lambda i: (0, 0))],
        out_specs=[pl.BlockSpec((tm, d), lambda i: (i, 0)),
                   pl.BlockSpec((tm * d // LANE, LANE), lambda i: (i, 0)),
                   pl.BlockSpec((tm, N_EXPERTS), lambda i: (i, 0))],
        out_shape=[jax.ShapeDtypeStruct((n, d), f32),
                   jax.ShapeDtypeStruct((n * d // LANE, LANE), f32),
                   jax.ShapeDtypeStruct((n, N_EXPERTS), f32)],
        compiler_params=_cparams(1),
        name="outproj",
    )(merged, x2, mod8, g_ffn, w_out_b, w_router)


def _route_select(lg_ref, bias_ref):
    s = _sigmoid(lg_ref[...])
    sel = s + bias_ref[...]
    rows = [sel[EXPERTS_PER_GROUP * g:EXPERTS_PER_GROUP * (g + 1), :] for g in range(N_EXPERT_GROUPS)]
    grp = []
    for r in rows:
        m1 = jnp.max(r, axis=0, keepdims=True)
        eq = r == m1
        n_eq = jnp.sum(jnp.where(eq, 1.0, 0.0), axis=0, keepdims=True)
        m2 = jnp.max(jnp.where(eq, -jnp.inf, r), axis=0, keepdims=True)
        grp.append(m1 + jnp.where(n_eq >= 2.0, m1, m2))
    masked = []
    for g in range(N_EXPERT_GROUPS):
        rank = jnp.zeros_like(grp[g])
        for g2 in range(N_EXPERT_GROUPS):
            if g2 == g:
                continue
            beats = (grp[g2] >= grp[g]) if g2 < g else (grp[g2] > grp[g])
            rank = rank + jnp.where(beats, 1.0, 0.0)
        masked.append(jnp.where(rank < float(TOPK_GROUPS), rows[g], -jnp.inf))
    masked = jnp.concatenate(masked, axis=0)
    eidx = lax.broadcasted_iota(i32, (N_EXPERTS, 1), 0)
    rank = jnp.zeros_like(masked)
    for e2 in range(N_EXPERTS):
        row = masked[e2:e2 + 1, :]
        beats = (row > masked) | ((row == masked) & (eidx > e2))
        rank = rank + jnp.where(beats, 1.0, 0.0)
    return s, rank, rank < float(EXPERT_TOPK)


def _route_count_kernel(lg_ref, bias_ref, cnt_ref):
    @pl.when(pl.program_id(0) == 0)
    def _():
        cnt_ref[...] = jnp.zeros(cnt_ref.shape, f32)

    _, _, selected = _route_select(lg_ref, bias_ref)
    cnt_ref[...] += jnp.sum(jnp.where(selected, 1.0, 0.0), axis=1, keepdims=True)


def _route_assign_kernel(lg_ref, bias_ref, pstart_ref, gate_ref, pos_ref, run_scr):
    @pl.when(pl.program_id(0) == 0)
    def _():
        run_scr[...] = jnp.zeros(run_scr.shape, f32)

    tn = lg_ref.shape[1]
    s, rank, selected = _route_select(lg_ref, bias_ref)
    sel_f = jnp.where(selected, 1.0, 0.0)
    earlier = lax.broadcasted_iota(i32, (tn, tn), 0) < lax.broadcasted_iota(i32, (tn, tn), 1)
    prefix = jnp.dot(sel_f.astype(bf16), jnp.where(earlier, 1.0, 0.0).astype(bf16), preferred_element_type=f32)
    pos = pstart_ref[...] + run_scr[...] + prefix
    run_scr[...] += jnp.sum(sel_f, axis=1, keepdims=True)
    gate = jnp.where(selected, s, 0.0)
    gate = gate / jnp.sum(gate, axis=0, keepdims=True) * ROUTED_SCALE
    for j in range(EXPERT_TOPK):
        slot = rank == float(j)
        gate_ref[j:j + 1, :] = jnp.sum(jnp.where(slot, gate, 0.0), axis=0, keepdims=True)
        pos_ref[j:j + 1, :] = jnp.sum(jnp.where(slot, pos, 0.0), axis=0, keepdims=True).astype(i32)


def _route_counts(logits_t, bias_col):
    e, n = logits_t.shape
    tn = min(512, n)
    return pl.pallas_call(
        _route_count_kernel,
        grid=(n // tn,),
        in_specs=[pl.BlockSpec((e, tn), lambda t: (0, t)),
                  pl.BlockSpec((e, 1), lambda t: (0, 0))],
        out_specs=pl.BlockSpec((e, 1), lambda t: (0, 0)),
        out_shape=jax.ShapeDtypeStruct((e, 1), f32),
        compiler_params=_cparams(1),
        name="route_count",
    )(logits_t, bias_col)


def _route_assign(logits_t, bias_col, pstart_col):
    e, n = logits_t.shape
    tn = min(512, n)
    return pl.pallas_call(
        _route_assign_kernel,
        grid=(n // tn,),
        in_specs=[pl.BlockSpec((e, tn), lambda t: (0, t)),
                  pl.BlockSpec((e, 1), lambda t: (0, 0)),
                  pl.BlockSpec((e, 1), lambda t: (0, 0))],
        out_specs=[pl.BlockSpec((EXPERT_TOPK, tn), lambda t: (0, t)),
                   pl.BlockSpec((EXPERT_TOPK, tn), lambda t: (0, t))],
        out_shape=[jax.ShapeDtypeStruct((EXPERT_TOPK, n), f32),
                   jax.ShapeDtypeStruct((EXPERT_TOPK, n), i32)],
        scratch_shapes=[pltpu.VMEM((e, 1), f32)],
        compiler_params=_cparams(1),
        name="route_assign",
    )(logits_t, bias_col, pstart_col)


def _dispatch_kernel(pos_ref, pad_tile_ref, h_ref, x1_ref, mod_ref, ws1_ref, ws3_ref, ws2_ref,
                     xs_ref, base_ref, zero_scr, sem, fill_sem, tail_sem, *, n_tokens, tn, s):
    base = pl.program_id(0) * tn
    tile_slabs = EXPERT_TILE * s
    n_tiles_total = xs_ref.shape[0] // tile_slabs

    def tail_fill(tile):
        first = pl.multiple_of(tile * tile_slabs, tile_slabs)
        return pltpu.make_async_copy(zero_scr, xs_ref.at[pl.ds(first, tile_slabs), :], tail_sem)

    @pl.when(pl.program_id(0) == 0)
    def _zero_padded_tiles():
        zero_scr[...] = jnp.zeros(zero_scr.shape, zero_scr.dtype)

        def fill(e):
            first = pl.multiple_of(pad_tile_ref[e] * s, tile_slabs)
            return pltpu.make_async_copy(zero_scr, xs_ref.at[pl.ds(first, tile_slabs), :], fill_sem)

        def start(e, carry):
            @pl.when(pad_tile_ref[e] >= 0)
            def _():
                fill(e).start()
            return carry

        def wait(e, carry):
            @pl.when(pad_tile_ref[e] >= 0)
            def _():
                fill(e).wait()
            return carry

        def start_tail(tile, carry):
            tail_fill(tile).start()
            return carry

        lax.fori_loop(0, N_EXPERTS, start, 0)
        lax.fori_loop(pad_tile_ref[N_EXPERTS], n_tiles_total, start_tail, 0)
        lax.fori_loop(0, N_EXPERTS, wait, 0)

    @pl.when(pl.program_id(0) == pl.num_programs(0) - 1)
    def _finish_tail_fills():
        def wait_tail(tile, carry):
            tail_fill(tile).wait()
            return carry

        lax.fori_loop(pad_tile_ref[N_EXPERTS], n_tiles_total, wait_tail, 0)

    def row_copy(t, p):
        return pltpu.make_async_copy(h_ref.at[pl.ds(pl.multiple_of(t * s, s), s), :],
                                     xs_ref.at[pl.ds(pl.multiple_of(p * s, s), s), :], sem)

    def issue(t, carry):
        for j in range(EXPERT_TOPK):
            row_copy(t, pos_ref[j * n_tokens + base + t]).start(priority=j % 2)
        return carry

    lax.fori_loop(0, tn, issue, 0)

    hb = _slabs_to_rows(h_ref, tn).astype(bf16)
    act = (_silu(jnp.dot(hb, ws1_ref[...], preferred_element_type=f32))
           * jnp.dot(hb, ws3_ref[...], preferred_element_type=f32)).astype(bf16)
    shared = jnp.dot(act, ws2_ref[...], preferred_element_type=f32)
    base_ref[...] = x1_ref[...] + mod_ref[0, 5:6, :] * shared

    for j in range(EXPERT_TOPK):
        pltpu.make_async_copy(h_ref, xs_ref.at[pl.ds(0, tn * s), :], sem).wait()


def _dispatch(pos_flat, pad_tile, h2_slabs, x1, mod8, ws1b, ws3b, ws2b, n_rows, seq):
    n, d = x1.shape
    s = d // LANE
    hdim = ws1b.shape[1]
    tn = min(256, seq)
    per_seq = seq // tn
    kern = functools.partial(_dispatch_kernel, n_tokens=n, tn=tn, s=s)
    return pl.pallas_call(
        kern,
        grid=(n // tn,),
        in_specs=[pl.BlockSpec(memory_space=pltpu.SMEM),
                  pl.BlockSpec(memory_space=pltpu.SMEM),
                  pl.BlockSpec((tn * s, LANE), lambda i: (i, 0)),
                  pl.BlockSpec((tn, d), lambda i: (i, 0)),
                  pl.BlockSpec((1, MOD_ROWS, d), lambda i: (i // per_seq, 0, 0)),
                  pl.BlockSpec((d, hdim), lambda i: (0, 0)),
                  pl.BlockSpec((d, hdim), lambda i: (0, 0)),
                  pl.BlockSpec((hdim, d), lambda i: (0, 0))],
        out_specs=[pl.BlockSpec(memory_space=pl.ANY),
                   pl.BlockSpec((tn, d), lambda i: (i, 0))],
        out_shape=[jax.ShapeDtypeStruct((n_rows * s, LANE), f32),
                   jax.ShapeDtypeStruct((n, d), f32)],
        scratch_shapes=[pltpu.VMEM((EXPERT_TILE * s, LANE), f32), pltpu.SemaphoreType.DMA(()),
                        pltpu.SemaphoreType.DMA(()), pltpu.SemaphoreType.DMA(())],
        compiler_params=_cparams(1),
        name="dispatch",
    )(pos_flat, pad_tile, h2_slabs, x1, mod8, ws1b, ws3b, ws2b)


def _expert_kernel(ord_ref, ue_ref, nu_ref, xs_hbm, w1_hbm, w3_hbm, w2_hbm, ys_hbm,
                   xbuf, ybuf, w13f, w2f, w1b, w3b, w2b, sems, xsems, ysems):
    i = pl.program_id(0)
    k = ord_ref[i]
    n_tiles = nu_ref[0]
    tb = xbuf.shape[1]

    def tile_rows(tile):
        return pl.ds(pl.multiple_of(tile * tb, tb), tb)

    def x_copy(tile):
        slot = tile % EXPERT_IN_SLOTS
        return pltpu.make_async_copy(xs_hbm.at[tile_rows(tile), :], xbuf.at[slot], xsems.at[slot])

    def y_copy(tile):
        slot = tile % EXPERT_OUT_SLOTS
        return pltpu.make_async_copy(ybuf.at[slot], ys_hbm.at[tile_rows(tile), :], ysems.at[slot])

    def fetch(kk, slot):
        e = ue_ref[kk]
        return (pltpu.make_async_copy(w1_hbm.at[e], w13f.at[slot, 0], sems.at[slot, 0]),
                pltpu.make_async_copy(w3_hbm.at[e], w13f.at[slot, 1], sems.at[slot, 1]),
                pltpu.make_async_copy(w2_hbm.at[e], w2f.at[slot], sems.at[slot, 2]))

    @pl.when(i == 0)
    def _prologue():
        for cp in fetch(0, 0):
            cp.start()
        for tile in range(EXPERT_IN_SLOTS - 1):
            @pl.when(tile < n_tiles)
            def _():
                x_copy(tile).start()

    def compute_tile(cast_slot):
        xs_ref = xbuf.at[i % EXPERT_IN_SLOTS]
        ys_ref = ybuf.at[i % EXPERT_OUT_SLOTS]
        t = EXPERT_TILE
        s = tb // t
        kc = 2 * LANE
        h1 = jnp.zeros((t, w1b.shape[1]), f32)
        h3 = jnp.zeros((t, w1b.shape[1]), f32)
        for c in range(s // 2):
            rows = slice(c * kc, (c + 1) * kc)
            if cast_slot is not None:
                w1b[rows, :] = w13f[cast_slot, 0, rows, :].astype(bf16)
                w3b[rows, :] = w13f[cast_slot, 1, rows, :].astype(bf16)
            xc = jnp.concatenate([xs_ref[pl.ds(2 * c, t, stride=s), :],
                                  xs_ref[pl.ds(2 * c + 1, t, stride=s), :]], axis=1).astype(bf16)
            h1 = h1 + jnp.dot(xc, w1b[rows, :], preferred_element_type=f32)
            h3 = h3 + jnp.dot(xc, w3b[rows, :], preferred_element_type=f32)
        a = (_silu(h1) * h3).astype(bf16)
        for c in range(s // 2):
            cols = slice(c * kc, (c + 1) * kc)
            if cast_slot is not None:
                w2b[:, cols] = w2f[cast_slot, :, cols].astype(bf16)
            yc = jnp.dot(a, w2b[:, cols], preferred_element_type=f32)
            ys_ref[pl.ds(2 * c, t, stride=s), :] = yc[:, 0:LANE]
            ys_ref[pl.ds(2 * c + 1, t, stride=s), :] = yc[:, LANE:kc]

    def run_tile(cast_slot):
        @pl.when(i + EXPERT_IN_SLOTS - 1 < n_tiles)
        def _():
            x_copy(i + EXPERT_IN_SLOTS - 1).start()

        x_copy(i).wait()

        @pl.when(i >= EXPERT_OUT_SLOTS)
        def _():
            y_copy(i - EXPERT_OUT_SLOTS).wait()

        compute_tile(cast_slot)
        y_copy(i).start()

        @pl.when(i == n_tiles - 1)
        def _drain():
            for back in range(EXPERT_OUT_SLOTS - 1, -1, -1):
                @pl.when(i >= back)
                def _():
                    y_copy(i - back).wait()

    first_of_expert = (i == 0) | (k != ord_ref[jnp.maximum(i - 1, 0)])

    @pl.when((i < n_tiles) & first_of_expert)
    def _first_tile():
        slot = k % 2
        for cp in fetch(k, slot):
            cp.wait()

        @pl.when(k + 1 < nu_ref[1])
        def _prefetch():
            for cp in fetch(k + 1, 1 - slot):
                cp.start(priority=1)

        run_tile(slot)

    @pl.when((i < n_tiles) & jnp.logical_not(first_of_expert))
    def _later_tile():
        run_tile(None)

    @pl.when(i >= n_tiles)
    def _unused():
        def y_zero(tile):
            return pltpu.make_async_copy(ybuf.at[0], ys_hbm.at[tile_rows(tile), :], ysems.at[0])

        @pl.when(i == n_tiles)
        def _():
            ybuf[0] = jnp.zeros(ybuf.shape[1:], ybuf.dtype)

        y_zero(i).start()

        @pl.when(i == pl.num_programs(0) - 1)
        def _():
            def wait_one(tile, carry):
                y_zero(tile).wait()
                return carry

            lax.fori_loop(n_tiles, pl.num_programs(0), wait_one, 0)


def _experts(tile_ord, used_experts, n_used, xs, w1, w3, w2):
    _, d, hdim = w1.shape
    tb = EXPERT_TILE * d // LANE
    p, w = xs.shape
    grid_spec = pltpu.PrefetchScalarGridSpec(
        num_scalar_prefetch=3,
        grid=(p // tb,),
        in_specs=[pl.BlockSpec(memory_space=pl.ANY)] * 4,
        out_specs=pl.BlockSpec(memory_space=pl.ANY),
        scratch_shapes=[pltpu.VMEM((EXPERT_IN_SLOTS, tb, w), f32), pltpu.VMEM((EXPERT_OUT_SLOTS, tb, w), f32),
                        pltpu.VMEM((2, 2, d, hdim), f32), pltpu.VMEM((2, hdim, d), f32),
                        pltpu.VMEM((d, hdim), bf16), pltpu.VMEM((d, hdim), bf16), pltpu.VMEM((hdim, d), bf16),
                        pltpu.SemaphoreType.DMA((2, 3)), pltpu.SemaphoreType.DMA((EXPERT_IN_SLOTS,)),
                        pltpu.SemaphoreType.DMA((EXPERT_OUT_SLOTS,))],
    )
    return pl.pallas_call(
        _expert_kernel,
        grid_spec=grid_spec,
        out_shape=jax.ShapeDtypeStruct((p, w), f32),
        compiler_params=_cparams(1),
        name="experts",
    )(tile_ord, used_experts, n_used, xs, w1, w3, w2)


def _combine_kernel(pos_ref, gate_ref, base_ref, mod_ref, ys_ref, o_ref, gbuf, r_scr, sems, *, n_tokens, tn):
    step = pl.program_id(0)
    n_steps = pl.num_programs(0)
    slot = step % 2

    def gather_tile(s):
        base = s * tn
        sl = s % 2

        def issue(t, carry):
            for j in range(EXPERT_TOPK):
                p = pos_ref[j * n_tokens + base + t]
                pltpu.make_async_copy(ys_ref.at[p], gbuf.at[sl, j, t], sems.at[sl]).start(priority=j % 2)
            return carry

        lax.fori_loop(0, tn, issue, 0)

    @pl.when(step == 0)
    def _():
        gather_tile(step)

    @pl.when(step + 1 < n_steps)
    def _():
        gather_tile(step + 1)

    for j in range(EXPERT_TOPK):
        pltpu.make_async_copy(ys_ref.at[pl.ds(0, tn)], gbuf.at[slot, j], sems.at[slot]).wait()

    routed = gate_ref[:, 0:1, :] * gbuf[slot, 0]
    for j in range(1, EXPERT_TOPK):
        routed = routed + gate_ref[:, j:j + 1, :] * gbuf[slot, j]
    r_scr[...] = routed.reshape(r_scr.shape)
    o_ref[...] = base_ref[...] + mod_ref[0, 5:6, :] * _slabs_to_rows(r_scr, tn)


def _combine(pos_flat, gate_rep, base, mod8, ys, seq):
    n, d = base.shape
    s = d // LANE
    tn = min(128, seq)
    per_seq = seq // tn
    kern = functools.partial(_combine_kernel, n_tokens=n, tn=tn)
    return pl.pallas_call(
        kern,
        grid=(n // tn,),
        in_specs=[pl.BlockSpec(memory_space=pltpu.SMEM),
                  pl.BlockSpec((tn, EXPERT_TOPK, LANE), lambda i: (i, 0, 0)),
                  pl.BlockSpec((tn, d), lambda i: (i, 0)),
                  pl.BlockSpec((1, MOD_ROWS, d), lambda i: (i // per_seq, 0, 0)),
                  pl.BlockSpec(memory_space=pl.ANY)],
        out_specs=pl.BlockSpec((tn, d), lambda i: (i, 0)),
        out_shape=jax.ShapeDtypeStruct((n, d), f32),
        scratch_shapes=[pltpu.VMEM((2, EXPERT_TOPK, tn, s, LANE), f32), pltpu.VMEM((tn * s, LANE), f32),
                        pltpu.SemaphoreType.DMA((2,))],
        compiler_params=_cparams(1),
        name="combine",
    )(pos_flat, gate_rep, base, mod8, ys)


def _layer(x, c, w_ada, b_ada, g_mix, w_in, q_norm, k_norm, w_attn_up, pool_lin, pool_scale, w_pool_up,
           w_out, g_ffn, w_router, router_bias, w1, w3, w2, ws1, ws3, ws2):
    batch, seq, d = x.shape
    n = batch * seq
    x2 = x.reshape(n, d)

    offs = [0, ATTN_WIDTH, ATTN_WIDTH + KV_WIDTH, ATTN_WIDTH + 2 * KV_WIDTH]
    w_q = w_in[:, offs[0]:offs[1]]
    w_k = w_in[:, offs[1]:offs[2]]
    w_v = w_in[:, offs[2]:offs[3]]
    o_qi = offs[3]
    w_qi = w_in[:, o_qi:o_qi + IDX_HEADS * IDX_DIM]
    o_ki = o_qi + IDX_HEADS * IDX_DIM
    w_ki = w_in[:, o_ki:o_ki + IDX_DIM]
    o_wi = o_ki + IDX_DIM
    w_wi = w_in[:, o_wi:o_wi + IDX_HEADS]
    o_u = o_wi + IDX_HEADS
    w_u = w_in[:, o_u:o_u + POOL_WIDTH]
    o_g = o_u + POOL_WIDTH
    w_g = w_in[:, o_g:o_g + 2 * d]
    w_main = jnp.concatenate([w_g, w_q, w_qi, w_u, w_k, w_v], axis=1).astype(bf16)
    w_kiwi = jnp.concatenate(
        [w_ki, w_wi, jnp.zeros((d, KIWI_WIDTH - IDX_DIM - IDX_HEADS), w_in.dtype)], axis=1).astype(bf16)

    assert batch <= SUBLANES and seq % min(KEY_CHUNK, seq) == 0 and d % (2 * LANE) == 0
    c8 = jnp.zeros((SUBLANES, d), f32).at[:batch].set(c)
    mod = _ada(c8, w_ada, b_ada.reshape(1, -1))[:batch]
    mod8 = jnp.zeros((batch, MOD_ROWS, d), f32).at[:, :N_MOD].set(mod.reshape(batch, N_MOD, d))

    proj, kiwi = _inproj(x2, mod8, g_mix.reshape(1, d), w_main, w_kiwi, seq)
    attn = _attention(proj, kiwi, q_norm.reshape(1, -1), k_norm.reshape(1, -1), batch, seq)
    pool = _pool(proj, pool_lin.astype(bf16), pool_scale.reshape(1, -1), seq)
    merged = _merge(attn, pool, w_attn_up.astype(bf16), w_pool_up.astype(bf16), proj)
    wr_hi = w_router.astype(bf16)
    wr_lo = (w_router - wr_hi.astype(f32)).astype(bf16)
    x1, h2, logits = _outproj(merged, x2, mod8, g_ffn.reshape(1, d), w_out.astype(bf16),
                              jnp.concatenate([wr_hi, wr_lo], axis=1), seq)

    logits_t = logits.T
    bias_col = router_bias.reshape(N_EXPERTS, 1)
    counts = _route_counts(logits_t, bias_col)[:, 0].astype(i32)
    t = EXPERT_TILE
    tiles_e = (counts + t - 1) // t
    tile_end = jnp.cumsum(tiles_e)
    pstart = ((tile_end - tiles_e) * t).astype(f32).reshape(N_EXPERTS, 1)
    n_tiles = n * EXPERT_TOPK // t + N_EXPERTS
    n_used = tile_end[-1]
    tile_ids = jnp.minimum(jnp.arange(n_tiles, dtype=i32), n_used - 1)
    block_e = jnp.sum((tile_end[None, :] <= tile_ids[:, None]).astype(i32), axis=1)
    block_e = jnp.minimum(block_e, N_EXPERTS - 1)
    pad_tile = jnp.where(tiles_e > 0, (tile_end - 1) * t, -1).astype(i32)
    pad_tile = jnp.concatenate([pad_tile, n_used.reshape(1).astype(i32)])
    gate8, pos8 = _route_assign(logits_t, bias_col, pstart)
    pos_flat = pos8.reshape(-1)

    cum_used = jnp.cumsum((tiles_e > 0).astype(i32))
    slots = jnp.arange(N_EXPERTS, dtype=i32)
    used_experts = jnp.minimum(jnp.sum((cum_used[None, :] <= slots[:, None]).astype(i32), axis=1), N_EXPERTS - 1)
    tile_ord = jnp.sum(jnp.where(block_e[:, None] == slots[None, :], cum_used[None, :] - 1, 0), axis=1).astype(i32)
    n_used2 = jnp.stack([n_used, cum_used[-1]]).astype(i32)

    slabs = d // LANE
    xs, base = _dispatch(pos_flat, pad_tile, h2, x1, mod8, ws1.astype(bf16), ws3.astype(bf16), ws2.astype(bf16),
                         n_tiles * t, seq)
    ys = _experts(tile_ord, used_experts, n_used2, xs, w1, w3, w2)
    gate_rep = jnp.broadcast_to(gate8.T[:, :, None], (n, EXPERT_TOPK, LANE))
    out = _combine(pos_flat, gate_rep, base, mod8, ys.reshape(-1, slabs, LANE), seq)
    return out.reshape(batch, seq, d)


def kernel(x, c, w_ada, b_ada, g_mix, w_in, q_norm, k_norm, w_attn_up, pool_lin, pool_scale, w_pool_up, w_out, g_ffn, w_router, router_bias, w1, w3, w2, ws1, ws3, ws2):
    for l in range(w_ada.shape[0]):
        x = _layer(x, c, w_ada[l], b_ada[l], g_mix[l], w_in[l], q_norm[l], k_norm[l], w_attn_up[l], pool_lin[l],
                   pool_scale[l], w_pool_up[l], w_out[l], g_ffn[l], w_router[l], router_bias[l], w1[l], w3[l],
                   w2[l], ws1[l], ws3[l], ws2[l])
    return x
```

```python
import functools

import jax
import jax.numpy as jnp
from jax import lax
from jax.experimental import pallas as pl
from jax.experimental.pallas import tpu as pltpu

f32 = jnp.float32
bf16 = jnp.bfloat16
i32 = jnp.int32

N_HEADS = 8
HEAD_DIM = 128
N_KV_HEADS = 2
Q_PER_KV = N_HEADS // N_KV_HEADS
ATTN_WIDTH = N_HEADS * HEAD_DIM
KV_WIDTH = N_KV_HEADS * HEAD_DIM
IDX_HEADS = 16
IDX_DIM = 64
IDX_TOPK_MAX = 256
Q_BLOCK = 128
LANE = 128
POOL_WINDOWS = (2, 4, 8, 16)
POOL_GROUP_DIM = 256
POOL_WIDTH = 1024
POOL_HALO = 16
N_EXPERTS = 64
N_EXPERT_GROUPS = 8
EXPERTS_PER_GROUP = 8
TOPK_GROUPS = 4
EXPERT_TOPK = 8
EXPERT_HIDDEN = 512
ROUTED_SCALE = 2.5
NORM_EPS = 1e-6
N_MOD = 6
SUBLANES = 8
MOD_ROWS = SUBLANES

COL_GATE_A = 0
COL_GATE_P = 2048
COL_Q = 4096
COL_QI = 5120
COL_U = 6144
COL_K = 7168
COL_V = 7424
MAIN_WIDTH = 7680
KIWI_WIDTH = 128

KEY_CHUNK = 512
ATTN_PIECE = 256
SEARCH_CHECK_FROM = 24
SEARCH_GROUP = 2
EXPERT_TILE = 256
EXPERT_IN_SLOTS = 3
EXPERT_OUT_SLOTS = 2
VMEM_LIMIT = 56 * 1024 * 1024
INT_MIN = -2147483648
KEY_OF_LOWEST_FINITE = -2139095040
NEG_BIG = -1e30
LOG2E = 1.4426950408889634


def _alibi_slope(h):
    return 2.0 ** (-8.0 * (h + 1) / N_HEADS)


def _cparams(n_axes, vmem=VMEM_LIMIT):
    return pltpu.CompilerParams(dimension_semantics=("arbitrary",) * n_axes, vmem_limit_bytes=vmem)


def _sigmoid(x):
    return 1.0 / (1.0 + jnp.exp(-x))


def _silu(x):
    return x * _sigmoid(x)


def _rows_to_slabs(ref, val):
    rows, width = val.shape
    s = width // LANE
    for j in range(s):
        ref[pl.ds(j, rows, stride=s), :] = val[:, j * LANE:(j + 1) * LANE]


def _slabs_to_rows(ref, rows):
    s = ref.shape[0] // rows
    return jnp.concatenate([ref[pl.ds(j, rows, stride=s), :] for j in range(s)], axis=1)


def _ada_kernel(c_ref, w_ref, b_ref, o_ref):
    sc = _silu(c_ref[...]).astype(bf16)
    o_ref[...] = jnp.dot(sc, w_ref[...].astype(bf16), preferred_element_type=f32) + b_ref[...]


def _ada(c8, w_ada, b_ada):
    d, n = w_ada.shape
    tn = 1024
    return pl.pallas_call(
        _ada_kernel,
        grid=(n // tn,),
        in_specs=[pl.BlockSpec((SUBLANES, d), lambda j: (0, 0)),
                  pl.BlockSpec((d, tn), lambda j: (0, j)),
                  pl.BlockSpec((1, tn), lambda j: (0, j))],
        out_specs=pl.BlockSpec((SUBLANES, tn), lambda j: (0, j)),
        out_shape=jax.ShapeDtypeStruct((SUBLANES, n), f32),
        compiler_params=_cparams(1),
        name="adaln",
    )(c8, w_ada, b_ada)


def _inproj_kernel(x_ref, mod_ref, g_ref, w_ref, wk_ref, o_ref, kiwi_ref, h_scr):
    @pl.when(pl.program_id(1) == 0)
    def _():
        x = x_ref[...]
        y = x * lax.rsqrt(jnp.mean(x * x, axis=-1, keepdims=True) + NORM_EPS) * g_ref[...]
        h = y * (1.0 + mod_ref[0, 1:2, :]) + mod_ref[0, 0:1, :]
        hb = h.astype(bf16)
        h_scr[...] = hb
        kiwi_ref[...] = jnp.dot(hb, wk_ref[...], preferred_element_type=f32)

    o_ref[...] = jnp.dot(h_scr[...], w_ref[...], preferred_element_type=f32).astype(o_ref.dtype)


def _inproj(x2, mod8, g_mix, w_main, w_kiwi, seq):
    n, d = x2.shape
    tm, tn = 1024, 1536
    tm = min(tm, seq)
    per_seq = seq // tm
    return pl.pallas_call(
        _inproj_kernel,
        grid=(n // tm, MAIN_WIDTH // tn),
        in_specs=[pl.BlockSpec((tm, d), lambda i, j: (i, 0)),
                  pl.BlockSpec((1, MOD_ROWS, d), lambda i, j: (i // per_seq, 0, 0)),
                  pl.BlockSpec((1, d), lambda i, j: (0, 0)),
                  pl.BlockSpec((d, tn), lambda i, j: (0, j)),
                  pl.BlockSpec((d, KIWI_WIDTH), lambda i, j: (0, 0))],
        out_specs=[pl.BlockSpec((tm, tn), lambda i, j: (i, j)),
                   pl.BlockSpec((tm, KIWI_WIDTH), lambda i, j: (i, 0))],
        out_shape=[jax.ShapeDtypeStruct((n, MAIN_WIDTH), bf16),
                   jax.ShapeDtypeStruct((n, KIWI_WIDTH), f32)],
        scratch_shapes=[pltpu.VMEM((tm, d), bf16)],
        compiler_params=_cparams(2),
        name="inproj",
    )(x2, mod8, g_mix, w_main, w_kiwi)


def _tree_sum(parts):
    while len(parts) > 1:
        parts = [parts[a] + parts[a + 1] for a in range(0, len(parts) - 1, 2)] + ([parts[-1]] if len(parts) % 2 else [])
    return parts[0]


def _attn_t_kernel(q_ref, qi_ref, k_ref, v_ref, kiwi_all_ref, kiwi_blk_ref, qn_ref, kn_ref, o_ref,
                   kn_scr, ki_scr, vt_scr, score_scr, qt_scr, qit_scr, bias_scr, m_scr, l_scr, acc_scr, *, seq, topk):
    i = pl.program_id(1)
    ck = min(KEY_CHUNK, seq)
    n_chunks = i // (ck // Q_BLOCK) + 1
    heads_per_dot = 4

    @pl.when(i == 0)
    def _prep_keys():
        for n in range(N_KV_HEADS):
            kf = k_ref[:, n * HEAD_DIM:(n + 1) * HEAD_DIM].astype(f32)
            r = lax.rsqrt(jnp.mean(kf * kf, axis=-1, keepdims=True) + NORM_EPS)
            kn_scr[:, n * HEAD_DIM:(n + 1) * HEAD_DIM] = (kf * r * kn_ref[...]).astype(bf16)
        ki_scr[...] = kiwi_all_ref[:, 0:IDX_DIM].astype(bf16)
        for c in range(seq // ck):
            vt_scr[c] = v_ref[c * ck:(c + 1) * ck, :].astype(f32).T.astype(bf16)
        rel = (lax.broadcasted_iota(i32, (ck, Q_BLOCK), 1) - lax.broadcasted_iota(i32, (ck, Q_BLOCK), 0)).astype(f32)
        for h in range(N_HEADS):
            bias_scr[h] = rel * (-_alibi_slope(h) * LOG2E)

    q_t = []
    for h in range(N_HEADS):
        qf = q_ref[:, h * HEAD_DIM:(h + 1) * HEAD_DIM].astype(f32)
        r = lax.rsqrt(jnp.mean(qf * qf, axis=-1, keepdims=True) + NORM_EPS)
        q_t.append((qf * r * qn_ref[...] * (HEAD_DIM ** -0.5 * LOG2E)).T)
    for n in range(N_KV_HEADS):
        qt_scr[n] = jnp.concatenate(q_t[n * Q_PER_KV:(n + 1) * Q_PER_KV], axis=1).astype(bf16)
    qi_t = qi_ref[...].astype(f32).T
    for a in range(IDX_HEADS // heads_per_dot):
        qit_scr[a] = jnp.concatenate(
            [qi_t[(a * heads_per_dot + b) * IDX_DIM:(a * heads_per_dot + b + 1) * IDX_DIM, :]
             for b in range(heads_per_dot)], axis=1).astype(bf16)
    wi_t = kiwi_blk_ref[...].T[IDX_DIM:IDX_DIM + IDX_HEADS, :] * (IDX_HEADS ** -0.5 * IDX_DIM ** -0.5)
    qpos = i * Q_BLOCK + lax.broadcasted_iota(i32, (1, Q_BLOCK), 1)

    def index_chunk(c, carry):
        start = pl.multiple_of(c * ck, ck)
        kc = ki_scr[pl.ds(start, ck), :]
        acc = jnp.zeros((ck, Q_BLOCK), f32)
        for a in range(IDX_HEADS // heads_per_dot):
            d = jnp.dot(kc, qit_scr[a], preferred_element_type=f32)
            for b in range(heads_per_dot):
                h = a * heads_per_dot + b
                acc = acc + jnp.maximum(d[:, b * Q_BLOCK:(b + 1) * Q_BLOCK], 0.0) * wi_t[h:h + 1, :]
        kpos = start + lax.broadcasted_iota(i32, (ck, 1), 0)
        score_scr[c] = jnp.where(kpos <= qpos, acc, -jnp.inf)
        return carry

    lax.fori_loop(0, n_chunks, index_chunk, 0)

    def key_to_float(key):
        bits = key ^ (lax.shift_right_arithmetic(key, 31) & jnp.int32(0x7FFFFFFF))
        return lax.bitcast_convert_type(bits, f32)

    def bit_step(b, state):
        t_u, kept = state
        bit = lax.shift_left(jnp.int32(1), 31 - b)
        cand_u = t_u | bit
        cand = key_to_float(cand_u ^ jnp.int32(INT_MIN))

        def count_chunk(c, cnt):
            ge = jnp.where(score_scr[c] >= cand, 1.0, 0.0)
            return cnt + _tree_sum([ge[s * 8:(s + 1) * 8, :] for s in range(ck // 8)])

        cnt = lax.fori_loop(0, n_chunks, count_chunk, jnp.zeros((8, Q_BLOCK), f32))
        total = jnp.sum(cnt, axis=0, keepdims=True)
        accept = total >= float(topk)
        return jnp.where(accept, cand_u, t_u), jnp.where(accept, total, kept)

    state = lax.fori_loop(0, SEARCH_CHECK_FROM, bit_step,
                          (jnp.zeros((1, Q_BLOCK), i32), jnp.full((1, Q_BLOCK), float(seq + 1), f32)))

    def settled(kept):
        ok = (kept == float(topk)) | (qpos < topk)
        return jnp.min(jnp.where(ok, 1.0, 0.0)) > 0.0

    def more_bits(loop):
        b, _, done = loop
        return (b < 32) & jnp.logical_not(done)

    def bit_group(loop):
        b, st, _ = loop
        st = lax.fori_loop(b, b + SEARCH_GROUP, bit_step, st)
        return b + SEARCH_GROUP, st, settled(st[1])

    _, (t_u, _), _ = lax.while_loop(more_bits, bit_group, (jnp.int32(SEARCH_CHECK_FROM), state, settled(state[1])))
    thr = key_to_float(jnp.maximum(t_u ^ jnp.int32(INT_MIN), jnp.int32(KEY_OF_LOWEST_FINITE)))

    def count_ties(c, carry):
        ge, gt = carry
        sc = score_scr[c]
        ge = ge + _tree_sum([jnp.where(sc[s * 8:(s + 1) * 8, :] >= thr, 1.0, 0.0) for s in range(ck // 8)])
        gt = gt + _tree_sum([jnp.where(sc[s * 8:(s + 1) * 8, :] > thr, 1.0, 0.0) for s in range(ck // 8)])
        return ge, gt

    zeros8 = jnp.zeros((8, Q_BLOCK), f32)
    ge8, gt8 = lax.fori_loop(0, n_chunks, count_ties, (zeros8, zeros8))
    excess = jnp.sum(ge8, axis=0, keepdims=True) > float(topk)
    need = float(topk) - jnp.sum(gt8, axis=0, keepdims=True)

    @pl.when(jnp.max(jnp.where(excess, 1.0, 0.0)) > 0.0)
    def _break_ties():
        pos_bits = (seq - 1).bit_length()

        def key_positions(c):
            return c * ck + lax.broadcasted_iota(i32, (ck, Q_BLOCK), 0)

        def pos_step(b, q):
            cand = q | lax.shift_left(jnp.int32(1), pos_bits - 1 - b)

            def count_before(c, cnt):
                hit = jnp.where((score_scr[c] == thr) & (key_positions(c) < cand), 1.0, 0.0)
                return cnt + _tree_sum([hit[s * 8:(s + 1) * 8, :] for s in range(ck // 8)])

            before = jnp.sum(lax.fori_loop(0, n_chunks, count_before, zeros8), axis=0, keepdims=True)
            return jnp.where(before < need, cand, q)

        q = lax.fori_loop(0, pos_bits, pos_step, jnp.zeros((1, Q_BLOCK), i32))
        q = jnp.where(excess, q, jnp.int32(seq))

        def demote(c, carry):
            sc = score_scr[c]
            score_scr[c] = jnp.where((sc == thr) & (key_positions(c) > q), -jnp.inf, sc)
            return carry

        lax.fori_loop(0, n_chunks, demote, 0)

    m_scr[...] = jnp.full(m_scr.shape, NEG_BIG, f32)
    l_scr[...] = jnp.zeros(l_scr.shape, f32)
    acc_scr[...] = jnp.zeros(acc_scr.shape, f32)
    lane_head = lax.broadcasted_iota(i32, (1, Q_PER_KV * Q_BLOCK), 1) // Q_BLOCK

    piece = min(ATTN_PIECE, ck)
    slope_rows = []
    for n in range(N_KV_HEADS):
        slope_row = jnp.zeros((1, Q_PER_KV * Q_BLOCK), f32)
        for g in range(Q_PER_KV):
            slope_row = jnp.where(lane_head == g, _alibi_slope(n * Q_PER_KV + g) * LOG2E, slope_row)
        slope_rows.append(slope_row)

    def attn_chunk(c, carry):
        start = pl.multiple_of(c * ck, ck)
        tile_dist = (i * Q_BLOCK - start).astype(f32)
        units = [(sub, n) for sub in range(ck // piece) for n in range(N_KV_HEADS)]

        def scores(sub, n):
            kc = kn_scr[pl.ds(pl.multiple_of(start + sub * piece, piece), piece), n * HEAD_DIM:(n + 1) * HEAD_DIM]
            return jnp.dot(kc, qt_scr[n], preferred_element_type=f32)

        ahead = 3
        z_of = {u: scores(*u) for u in units[:ahead]}
        for idx, (sub, n) in enumerate(units):
            r0, r1 = sub * piece, (sub + 1) * piece
            sel = score_scr[c, r0:r1, :] >= thr
            z_all = z_of.pop((sub, n))
            z = jnp.concatenate(
                [jnp.where(sel, z_all[:, g * Q_BLOCK:(g + 1) * Q_BLOCK] + bias_scr[n * Q_PER_KV + g, r0:r1, :],
                           NEG_BIG) for g in range(Q_PER_KV)], axis=1)
            off = slope_rows[n] * (-tile_dist)
            m_old = m_scr[n]
            m_new = jnp.maximum(m_old, jnp.max(z, axis=0, keepdims=True) + off)
            alpha = jnp.exp2(m_old - m_new)
            p = jnp.exp2(z + (off - m_new))
            l_scr[n] = alpha * l_scr[n] + jnp.sum(p, axis=0, keepdims=True)
            vt = vt_scr[c, n * HEAD_DIM:(n + 1) * HEAD_DIM, r0:r1]
            acc_scr[n] = alpha * acc_scr[n] + jnp.dot(vt, p.astype(bf16), preferred_element_type=f32)
            m_scr[n] = m_new
            if idx + ahead < len(units):
                z_of[units[idx + ahead]] = scores(*units[idx + ahead])
        return carry

    lax.fori_loop(0, n_chunks, attn_chunk, 0)
    for n in range(N_KV_HEADS):
        out_t = acc_scr[n] / l_scr[n]
        for g in range(Q_PER_KV):
            h = n * Q_PER_KV + g
            o_ref[:, h * HEAD_DIM:(h + 1) * HEAD_DIM] = out_t[:, g * Q_BLOCK:(g + 1) * Q_BLOCK].T.astype(o_ref.dtype)


def _attention(proj, kiwi, q_norm, k_norm, batch, seq):
    n = batch * seq
    nq = seq // Q_BLOCK
    ck = min(KEY_CHUNK, seq)
    topk = min(IDX_TOPK_MAX, seq // 4)
    kern = functools.partial(_attn_t_kernel, seq=seq, topk=topk)
    return pl.pallas_call(
        kern,
        grid=(batch, nq),
        in_specs=[pl.BlockSpec((Q_BLOCK, ATTN_WIDTH), lambda b, i: (b * nq + i, COL_Q // ATTN_WIDTH)),
                  pl.BlockSpec((Q_BLOCK, IDX_HEADS * IDX_DIM), lambda b, i: (b * nq + i, COL_QI // (IDX_HEADS * IDX_DIM))),
                  pl.BlockSpec((seq, KV_WIDTH), lambda b, i: (b, COL_K // KV_WIDTH)),
                  pl.BlockSpec((seq, KV_WIDTH), lambda b, i: (b, COL_V // KV_WIDTH)),
                  pl.BlockSpec((seq, KIWI_WIDTH), lambda b, i: (b, 0)),
                  pl.BlockSpec((Q_BLOCK, KIWI_WIDTH), lambda b, i: (b * nq + i, 0)),
                  pl.BlockSpec((1, HEAD_DIM), lambda b, i: (0, 0)),
                  pl.BlockSpec((1, HEAD_DIM), lambda b, i: (0, 0))],
        out_specs=pl.BlockSpec((Q_BLOCK, ATTN_WIDTH), lambda b, i: (b * nq + i, 0)),
        out_shape=jax.ShapeDtypeStruct((n, ATTN_WIDTH), bf16),
        scratch_shapes=[pltpu.VMEM((seq, KV_WIDTH), bf16),
                        pltpu.VMEM((seq, IDX_DIM), bf16),
                        pltpu.VMEM((seq // ck, KV_WIDTH, ck), bf16),
                        pltpu.VMEM((seq // ck, ck, Q_BLOCK), f32),
                        pltpu.VMEM((N_KV_HEADS, HEAD_DIM, Q_PER_KV * Q_BLOCK), bf16),
                        pltpu.VMEM((IDX_HEADS // 4, IDX_DIM, 4 * Q_BLOCK), bf16),
                        pltpu.VMEM((N_HEADS, ck, Q_BLOCK), f32),
                        pltpu.VMEM((N_KV_HEADS, 1, Q_PER_KV * Q_BLOCK), f32),
                        pltpu.VMEM((N_KV_HEADS, 1, Q_PER_KV * Q_BLOCK), f32),
                        pltpu.VMEM((N_KV_HEADS, HEAD_DIM, Q_PER_KV * Q_BLOCK), f32)],
        compiler_params=_cparams(2),
        name="sparse_attn",
    )(proj, proj, proj, proj, kiwi, kiwi, q_norm, k_norm)


def _pool_kernel(u_ref, halo_ref, lin_ref, ps_ref, o_ref, scr, *, tm, per_seq):
    i = pl.program_id(0)
    first = (i % per_seq) == 0
    scr[0:POOL_HALO, :] = jnp.where(first, 0.0, halo_ref[...].astype(f32))
    scr[POOL_HALO:POOL_HALO + tm, :] = u_ref[...].astype(f32)
    t_in_seq = (i % per_seq) * tm + lax.broadcasted_iota(i32, (tm, 1), 0)
    for g, w in enumerate(POOL_WINDOWS):
        c0, c1 = g * POOL_GROUP_DIM, (g + 1) * POOL_GROUP_DIM
        cur = scr[POOL_HALO:POOL_HALO + tm, c0:c1]
        s = cur
        for j in range(1, w):
            s = s + scr[POOL_HALO - j:POOL_HALO - j + tm, c0:c1]
        count = jnp.minimum(t_in_seq + 1, w).astype(f32)
        pooled = s / count - cur
        mixed = jnp.dot(pooled.astype(bf16), lin_ref[g], preferred_element_type=f32)
        o_ref[:, c0:c1] = (mixed * ps_ref[:, c0:c1]).astype(o_ref.dtype)


def _pool(proj, pool_lin_b, pool_scale, seq):
    n = proj.shape[0]
    tm = min(512, seq)
    per_seq = seq // tm
    hb = tm // POOL_HALO
    kern = functools.partial(_pool_kernel, tm=tm, per_seq=per_seq)
    return pl.pallas_call(
        kern,
        grid=(n // tm,),
        in_specs=[pl.BlockSpec((tm, POOL_WIDTH), lambda i: (i, COL_U // POOL_WIDTH)),
                  pl.BlockSpec((POOL_HALO, POOL_WIDTH), lambda i: (jnp.maximum(i * hb - 1, 0), COL_U // POOL_WIDTH)),
                  pl.BlockSpec((len(POOL_WINDOWS), POOL_GROUP_DIM, POOL_GROUP_DIM), lambda i: (0, 0, 0)),
                  pl.BlockSpec((1, POOL_WIDTH), lambda i: (0, 0))],
        out_specs=pl.BlockSpec((tm, POOL_WIDTH), lambda i: (i, 0)),
        out_shape=jax.ShapeDtypeStruct((n, POOL_WIDTH), bf16),
        scratch_shapes=[pltpu.VMEM((POOL_HALO + tm, POOL_WIDTH), f32)],
        compiler_params=_cparams(1),
        name="pool",
    )(proj, proj, pool_lin_b, pool_scale)


def _merge_kernel(a_ref, p_ref, wa_ref, wp_ref, ga_ref, gp_ref, o_ref):
    ya = jnp.dot(a_ref[...], wa_ref[...], preferred_element_type=f32)
    yp = jnp.dot(p_ref[...], wp_ref[...], preferred_element_type=f32)
    o = _sigmoid(ga_ref[...].astype(f32)) * ya + _sigmoid(gp_ref[...].astype(f32)) * yp
    o_ref[...] = o.astype(o_ref.dtype)


def _merge(attn, pool, w_au, w_pu, proj):
    n = attn.shape[0]
    d = w_au.shape[1]
    tm, tn = min(1024, n), 1024
    ga0, gp0 = COL_GATE_A // tn, COL_GATE_P // tn
    return pl.pallas_call(
        _merge_kernel,
        grid=(n // tm, d // tn),
        in_specs=[pl.BlockSpec((tm, ATTN_WIDTH), lambda i, j: (i, 0)),
                  pl.BlockSpec((tm, POOL_WIDTH), lambda i, j: (i, 0)),
                  pl.BlockSpec((ATTN_WIDTH, tn), lambda i, j: (0, j)),
                  pl.BlockSpec((POOL_WIDTH, tn), lambda i, j: (0, j)),
                  pl.BlockSpec((tm, tn), lambda i, j: (i, ga0 + j)),
                  pl.BlockSpec((tm, tn), lambda i, j: (i, gp0 + j))],
        out_specs=pl.BlockSpec((tm, tn), lambda i, j: (i, j)),
        out_shape=jax.ShapeDtypeStruct((n, d), bf16),
        compiler_params=_cparams(2),
        name="merge",
    )(attn, pool, w_au, w_pu, proj, proj)


def _outproj_kernel(m_ref, x_ref, mod_ref, g_ref, wo_ref, wr_ref, x1_ref, h2_ref, lg_ref):
    y = jnp.dot(m_ref[...], wo_ref[...], preferred_element_type=f32)
    x1 = x_ref[...] + mod_ref[0, 2:3, :] * y
    x1_ref[...] = x1
    hn = x1 * lax.rsqrt(jnp.mean(x1 * x1, axis=-1, keepdims=True) + NORM_EPS) * g_ref[...]
    h2 = hn * (1.0 + mod_ref[0, 4:5, :]) + mod_ref[0, 3:4, :]
    _rows_to_slabs(h2_ref, h2)
    h_hi = h2.astype(bf16)
    h_lo = (h2 - h_hi.astype(f32)).astype(bf16)
    a = jnp.dot(h_hi, wr_ref[...], preferred_element_type=f32)
    b = jnp.dot(h_lo, wr_ref[:, 0:N_EXPERTS], preferred_element_type=f32)
    lg_ref[...] = a[:, 0:N_EXPERTS] + (a[:, N_EXPERTS:2 * N_EXPERTS] + b)


def _outproj(merged, x2, mod8, g_ffn, w_out_b, w_router, seq):
    n, d = x2.shape
    tm = min(512, seq)
    per_seq = seq // tm
    return pl.pallas_call(
        _outproj_kernel,
        grid=(n // tm,),
        in_specs=[pl.BlockSpec((tm, d), lambda i: (i, 0)),
                  pl.BlockSpec((tm, d), lambda i: (i, 0)),
                  pl.BlockSpec((1, MOD_ROWS, d), lambda i: (i // per_seq, 0, 0)),
                  pl.BlockSpec((1, d), lambda i: (0, 0)),
                  pl.BlockSpec((d, d), lambda i: (0, 0)),
                  pl.BlockSpec((d, 2 * N_EXPERTS), lambda i: (0, 0))],
        out_specs=[pl.BlockSpec((tm, d), lambda i: (i, 0)),
                   pl.BlockSpec((tm * d // LANE, LANE), lambda i: (i, 0)),
                   pl.BlockSpec((tm, N_EXPERTS), lambda i: (i, 0))],
        out_shape=[jax.ShapeDtypeStruct((n, d), f32),
                   jax.ShapeDtypeStruct((n * d // LANE, LANE), f32),
                   jax.ShapeDtypeStruct((n, N_EXPERTS), f32)],
        compiler_params=_cparams(1),
        name="outproj",
    )(merged, x2, mod8, g_ffn, w_out_b, w_router)


def _route_select(lg_ref, bias_ref):
    s = _sigmoid(lg_ref[...])
    sel = s + bias_ref[...]
    rows = [sel[EXPERTS_PER_GROUP * g:EXPERTS_PER_GROUP * (g + 1), :] for g in range(N_EXPERT_GROUPS)]
    grp = []
    for r in rows:
        m1 = jnp.max(r, axis=0, keepdims=True)
        eq = r == m1
        n_eq = jnp.sum(jnp.where(eq, 1.0, 0.0), axis=0, keepdims=True)
        m2 = jnp.max(jnp.where(eq, -jnp.inf, r), axis=0, keepdims=True)
        grp.append(m1 + jnp.where(n_eq >= 2.0, m1, m2))
    masked = []
    for g in range(N_EXPERT_GROUPS):
        rank = jnp.zeros_like(grp[g])
        for g2 in range(N_EXPERT_GROUPS):
            if g2 == g:
                continue
            beats = (grp[g2] >= grp[g]) if g2 < g else (grp[g2] > grp[g])
            rank = rank + jnp.where(beats, 1.0, 0.0)
        masked.append(jnp.where(rank < float(TOPK_GROUPS), rows[g], -jnp.inf))
    masked = jnp.concatenate(masked, axis=0)
    eidx = lax.broadcasted_iota(i32, (N_EXPERTS, 1), 0)
    rank = jnp.zeros_like(masked)
    for e2 in range(N_EXPERTS):
        row = masked[e2:e2 + 1, :]
        beats = (row > masked) | ((row == masked) & (eidx > e2))
        rank = rank + jnp.where(beats, 1.0, 0.0)
    return s, rank, rank < float(EXPERT_TOPK)


def _route_count_kernel(lg_ref, bias_ref, cnt_ref):
    @pl.when(pl.program_id(0) == 0)
    def _():
        cnt_ref[...] = jnp.zeros(cnt_ref.shape, f32)

    _, _, selected = _route_select(lg_ref, bias_ref)
    cnt_ref[...] += jnp.sum(jnp.where(selected, 1.0, 0.0), axis=1, keepdims=True)


def _route_assign_kernel(lg_ref, bias_ref, pstart_ref, gate_ref, pos_ref, run_scr):
    @pl.when(pl.program_id(0) == 0)
    def _():
        run_scr[...] = jnp.zeros(run_scr.shape, f32)

    tn = lg_ref.shape[1]
    s, rank, selected = _route_select(lg_ref, bias_ref)
    sel_f = jnp.where(selected, 1.0, 0.0)
    earlier = lax.broadcasted_iota(i32, (tn, tn), 0) < lax.broadcasted_iota(i32, (tn, tn), 1)
    prefix = jnp.dot(sel_f.astype(bf16), jnp.where(earlier, 1.0, 0.0).astype(bf16), preferred_element_type=f32)
    pos = pstart_ref[...] + run_scr[...] + prefix
    run_scr[...] += jnp.sum(sel_f, axis=1, keepdims=True)
    gate = jnp.where(selected, s, 0.0)
    gate = gate / jnp.sum(gate, axis=0, keepdims=True) * ROUTED_SCALE
    for j in range(EXPERT_TOPK):
        slot = rank == float(j)
        gate_ref[j:j + 1, :] = jnp.sum(jnp.where(slot, gate, 0.0), axis=0, keepdims=True)
        pos_ref[j:j + 1, :] = jnp.sum(jnp.where(slot, pos, 0.0), axis=0, keepdims=True).astype(i32)


def _route_counts(logits_t, bias_col):
    e, n = logits_t.shape
    tn = min(512, n)
    return pl.pallas_call(
        _route_count_kernel,
        grid=(n // tn,),
        in_specs=[pl.BlockSpec((e, tn), lambda t: (0, t)),
                  pl.BlockSpec((e, 1), lambda t: (0, 0))],
        out_specs=pl.BlockSpec((e, 1), lambda t: (0, 0)),
        out_shape=jax.ShapeDtypeStruct((e, 1), f32),
        compiler_params=_cparams(1),
        name="route_count",
    )(logits_t, bias_col)


def _route_assign(logits_t, bias_col, pstart_col):
    e, n = logits_t.shape
    tn = min(512, n)
    return pl.pallas_call(
        _route_assign_kernel,
        grid=(n // tn,),
        in_specs=[pl.BlockSpec((e, tn), lambda t: (0, t)),
                  pl.BlockSpec((e, 1), lambda t: (0, 0)),
                  pl.BlockSpec((e, 1), lambda t: (0, 0))],
        out_specs=[pl.BlockSpec((EXPERT_TOPK, tn), lambda t: (0, t)),
                   pl.BlockSpec((EXPERT_TOPK, tn), lambda t: (0, t))],
        out_shape=[jax.ShapeDtypeStruct((EXPERT_TOPK, n), f32),
                   jax.ShapeDtypeStruct((EXPERT_TOPK, n), i32)],
        scratch_shapes=[pltpu.VMEM((e, 1), f32)],
        compiler_params=_cparams(1),
        name="route_assign",
    )(logits_t, bias_col, pstart_col)


def _dispatch_kernel(pos_ref, pad_tile_ref, h_ref, x1_ref, mod_ref, ws1_ref, ws3_ref, ws2_ref,
                     xs_ref, base_ref, zero_scr, sem, fill_sem, tail_sem, *, n_tokens, tn, s):
    base = pl.program_id(0) * tn
    tile_slabs = EXPERT_TILE * s
    n_tiles_total = xs_ref.shape[0] // tile_slabs

    def tail_fill(tile):
        first = pl.multiple_of(tile * tile_slabs, tile_slabs)
        return pltpu.make_async_copy(zero_scr, xs_ref.at[pl.ds(first, tile_slabs), :], tail_sem)

    @pl.when(pl.program_id(0) == 0)
    def _zero_padded_tiles():
        zero_scr[...] = jnp.zeros(zero_scr.shape, zero_scr.dtype)

        def fill(e):
            first = pl.multiple_of(pad_tile_ref[e] * s, tile_slabs)
            return pltpu.make_async_copy(zero_scr, xs_ref.at[pl.ds(first, tile_slabs), :], fill_sem)

        def start(e, carry):
            @pl.when(pad_tile_ref[e] >= 0)
            def _():
                fill(e).start()
            return carry

        def wait(e, carry):
            @pl.when(pad_tile_ref[e] >= 0)
            def _():
                fill(e).wait()
            return carry

        def start_tail(tile, carry):
            tail_fill(tile).start()
            return carry

        lax.fori_loop(0, N_EXPERTS, start, 0)
        lax.fori_loop(pad_tile_ref[N_EXPERTS], n_tiles_total, start_tail, 0)
        lax.fori_loop(0, N_EXPERTS, wait, 0)

    @pl.when(pl.program_id(0) == pl.num_programs(0) - 1)
    def _finish_tail_fills():
        def wait_tail(tile, carry):
            tail_fill(tile).wait()
            return carry

        lax.fori_loop(pad_tile_ref[N_EXPERTS], n_tiles_total, wait_tail, 0)

    def row_copy(t, p):
        return pltpu.make_async_copy(h_ref.at[pl.ds(pl.multiple_of(t * s, s), s), :],
                                     xs_ref.at[pl.ds(pl.multiple_of(p * s, s), s), :], sem)

    def issue(t, carry):
        for j in range(EXPERT_TOPK):
            row_copy(t, pos_ref[j * n_tokens + base + t]).start(priority=j % 2)
        return carry

    lax.fori_loop(0, tn, issue, 0)

    hb = _slabs_to_rows(h_ref, tn).astype(bf16)
    act = (_silu(jnp.dot(hb, ws1_ref[...], preferred_element_type=f32))
           * jnp.dot(hb, ws3_ref[...], preferred_element_type=f32)).astype(bf16)
    shared = jnp.dot(act, ws2_ref[...], preferred_element_type=f32)
    base_ref[...] = x1_ref[...] + mod_ref[0, 5:6, :] * shared

    for j in range(EXPERT_TOPK):
        pltpu.make_async_copy(h_ref, xs_ref.at[pl.ds(0, tn * s), :], sem).wait()


def _dispatch(pos_flat, pad_tile, h2_slabs, x1, mod8, ws1b, ws3b, ws2b, n_rows, seq):
    n, d = x1.shape
    s = d // LANE
    hdim = ws1b.shape[1]
    tn = min(256, seq)
    per_seq = seq // tn
    kern = functools.partial(_dispatch_kernel, n_tokens=n, tn=tn, s=s)
    return pl.pallas_call(
        kern,
        grid=(n // tn,),
        in_specs=[pl.BlockSpec(memory_space=pltpu.SMEM),
                  pl.BlockSpec(memory_space=pltpu.SMEM),
                  pl.BlockSpec((tn * s, LANE), lambda i: (i, 0)),
                  pl.BlockSpec((tn, d), lambda i: (i, 0)),
                  pl.BlockSpec((1, MOD_ROWS, d), lambda i: (i // per_seq, 0, 0)),
                  pl.BlockSpec((d, hdim), lambda i: (0, 0)),
                  pl.BlockSpec((d, hdim), lambda i: (0, 0)),
                  pl.BlockSpec((hdim, d), lambda i: (0, 0))],
        out_specs=[pl.BlockSpec(memory_space=pl.ANY),
                   pl.BlockSpec((tn, d), lambda i: (i, 0))],
        out_shape=[jax.ShapeDtypeStruct((n_rows * s, LANE), f32),
                   jax.ShapeDtypeStruct((n, d), f32)],
        scratch_shapes=[pltpu.VMEM((EXPERT_TILE * s, LANE), f32), pltpu.SemaphoreType.DMA(()),
                        pltpu.SemaphoreType.DMA(()), pltpu.SemaphoreType.DMA(())],
        compiler_params=_cparams(1),
        name="dispatch",
    )(pos_flat, pad_tile, h2_slabs, x1, mod8, ws1b, ws3b, ws2b)


def _expert_kernel(ord_ref, ue_ref, nu_ref, xs_hbm, w1_hbm, w3_hbm, w2_hbm, ys_hbm,
                   xbuf, ybuf, w13f, w2f, w1b, w3b, w2b, sems, xsems, ysems):
    i = pl.program_id(0)
    k = ord_ref[i]
    n_tiles = nu_ref[0]
    tb = xbuf.shape[1]

    def tile_rows(tile):
        return pl.ds(pl.multiple_of(tile * tb, tb), tb)

    def x_copy(tile):
        slot = tile % EXPERT_IN_SLOTS
        return pltpu.make_async_copy(xs_hbm.at[tile_rows(tile), :], xbuf.at[slot], xsems.at[slot])

    def y_copy(tile):
        slot = tile % EXPERT_OUT_SLOTS
        return pltpu.make_async_copy(ybuf.at[slot], ys_hbm.at[tile_rows(tile), :], ysems.at[slot])

    def fetch(kk, slot):
        e = ue_ref[kk]
        return (pltpu.make_async_copy(w1_hbm.at[e], w13f.at[slot, 0], sems.at[slot, 0]),
                pltpu.make_async_copy(w3_hbm.at[e], w13f.at[slot, 1], sems.at[slot, 1]),
                pltpu.make_async_copy(w2_hbm.at[e], w2f.at[slot], sems.at[slot, 2]))

    @pl.when(i == 0)
    def _prologue():
        for cp in fetch(0, 0):
            cp.start()
        for tile in range(EXPERT_IN_SLOTS - 1):
            @pl.when(tile < n_tiles)
            def _():
                x_copy(tile).start()

    def compute_tile(cast_slot):
        xs_ref = xbuf.at[i % EXPERT_IN_SLOTS]
        ys_ref = ybuf.at[i % EXPERT_OUT_SLOTS]
        t = EXPERT_TILE
        s = tb // t
        kc = 2 * LANE
        h1 = jnp.zeros((t, w1b.shape[1]), f32)
        h3 = jnp.zeros((t, w1b.shape[1]), f32)
        for c in range(s // 2):
            rows = slice(c * kc, (c + 1) * kc)
            if cast_slot is not None:
                w1b[rows, :] = w13f[cast_slot, 0, rows, :].astype(bf16)
                w3b[rows, :] = w13f[cast_slot, 1, rows, :].astype(bf16)
            xc = jnp.concatenate([xs_ref[pl.ds(2 * c, t, stride=s), :],
                                  xs_ref[pl.ds(2 * c + 1, t, stride=s), :]], axis=1).astype(bf16)
            h1 = h1 + jnp.dot(xc, w1b[rows, :], preferred_element_type=f32)
            h3 = h3 + jnp.dot(xc, w3b[rows, :], preferred_element_type=f32)
        a = (_silu(h1) * h3).astype(bf16)
        for c in range(s // 2):
            cols = slice(c * kc, (c + 1) * kc)
            if cast_slot is not None:
                w2b[:, cols] = w2f[cast_slot, :, cols].astype(bf16)
            yc = jnp.dot(a, w2b[:, cols], preferred_element_type=f32)
            ys_ref[pl.ds(2 * c, t, stride=s), :] = yc[:, 0:LANE]
            ys_ref[pl.ds(2 * c + 1, t, stride=s), :] = yc[:, LANE:kc]

    def run_tile(cast_slot):
        @pl.when(i + EXPERT_IN_SLOTS - 1 < n_tiles)
        def _():
            x_copy(i + EXPERT_IN_SLOTS - 1).start()

        x_copy(i).wait()

        @pl.when(i >= EXPERT_OUT_SLOTS)
        def _():
            y_copy(i - EXPERT_OUT_SLOTS).wait()

        compute_tile(cast_slot)
        y_copy(i).start()

        @pl.when(i == n_tiles - 1)
        def _drain():
            for back in range(EXPERT_OUT_SLOTS - 1, -1, -1):
                @pl.when(i >= back)
                def _():
                    y_copy(i - back).wait()

    first_of_expert = (i == 0) | (k != ord_ref[jnp.maximum(i - 1, 0)])

    @pl.when((i < n_tiles) & first_of_expert)
    def _first_tile():
        slot = k % 2
        for cp in fetch(k, slot):
            cp.wait()

        @pl.when(k + 1 < nu_ref[1])
        def _prefetch():
            for cp in fetch(k + 1, 1 - slot):
                cp.start(priority=1)

        run_tile(slot)

    @pl.when((i < n_tiles) & jnp.logical_not(first_of_expert))
    def _later_tile():
        run_tile(None)

    @pl.when(i >= n_tiles)
    def _unused():
        def y_zero(tile):
            return pltpu.make_async_copy(ybuf.at[0], ys_hbm.at[tile_rows(tile), :], ysems.at[0])

        @pl.when(i == n_tiles)
        def _():
            ybuf[0] = jnp.zeros(ybuf.shape[1:], ybuf.dtype)

        y_zero(i).start()

        @pl.when(i == pl.num_programs(0) - 1)
        def _():
            def wait_one(tile, carry):
                y_zero(tile).wait()
                return carry

            lax.fori_loop(n_tiles, pl.num_programs(0), wait_one, 0)


def _experts(tile_ord, used_experts, n_used, xs, w1, w3, w2):
    _, d, hdim = w1.shape
    tb = EXPERT_TILE * d // LANE
    p, w = xs.shape
    grid_spec = pltpu.PrefetchScalarGridSpec(
        num_scalar_prefetch=3,
        grid=(p // tb,),
        in_specs=[pl.BlockSpec(memory_space=pl.ANY)] * 4,
        out_specs=pl.BlockSpec(memory_space=pl.ANY),
        scratch_shapes=[pltpu.VMEM((EXPERT_IN_SLOTS, tb, w), f32), pltpu.VMEM((EXPERT_OUT_SLOTS, tb, w), f32),
                        pltpu.VMEM((2, 2, d, hdim), f32), pltpu.VMEM((2, hdim, d), f32),
                        pltpu.VMEM((d, hdim), bf16), pltpu.VMEM((d, hdim), bf16), pltpu.VMEM((hdim, d), bf16),
                        pltpu.SemaphoreType.DMA((2, 3)), pltpu.SemaphoreType.DMA((EXPERT_IN_SLOTS,)),
                        pltpu.SemaphoreType.DMA((EXPERT_OUT_SLOTS,))],
    )
    return pl.pallas_call(
        _expert_kernel,
        grid_spec=grid_spec,
        out_shape=jax.ShapeDtypeStruct((p, w), f32),
        compiler_params=_cparams(1),
        name="experts",
    )(tile_ord, used_experts, n_used, xs, w1, w3, w2)


def _combine_kernel(pos_ref, gate_ref, base_ref, mod_ref, ys_ref, o_ref, gbuf, r_scr, sems, *, n_tokens, tn):
    step = pl.program_id(0)
    n_steps = pl.num_programs(0)
    slot = step % 2

    def gather_tile(s):
        base = s * tn
        sl = s % 2

        def issue(t, carry):
            for j in range(EXPERT_TOPK):
                p = pos_ref[j * n_tokens + base + t]
                pltpu.make_async_copy(ys_ref.at[p], gbuf.at[sl, j, t], sems.at[sl]).start(priority=j % 2)
            return carry

        lax.fori_loop(0, tn, issue, 0)

    @pl.when(step == 0)
    def _():
        gather_tile(step)

    @pl.when(step + 1 < n_steps)
    def _():
        gather_tile(step + 1)

    for j in range(EXPERT_TOPK):
        pltpu.make_async_copy(ys_ref.at[pl.ds(0, tn)], gbuf.at[slot, j], sems.at[slot]).wait()

    routed = gate_ref[:, 0:1, :] * gbuf[slot, 0]
    for j in range(1, EXPERT_TOPK):
        routed = routed + gate_ref[:, j:j + 1, :] * gbuf[slot, j]
    r_scr[...] = routed.reshape(r_scr.shape)
    o_ref[...] = base_ref[...] + mod_ref[0, 5:6, :] * _slabs_to_rows(r_scr, tn)


def _combine(pos_flat, gate_rep, base, mod8, ys, seq):
    n, d = base.shape
    s = d // LANE
    tn = min(128, seq)
    per_seq = seq // tn
    kern = functools.partial(_combine_kernel, n_tokens=n, tn=tn)
    return pl.pallas_call(
        kern,
        grid=(n // tn,),
        in_specs=[pl.BlockSpec(memory_space=pltpu.SMEM),
                  pl.BlockSpec((tn, EXPERT_TOPK, LANE), lambda i: (i, 0, 0)),
                  pl.BlockSpec((tn, d), lambda i: (i, 0)),
                  pl.BlockSpec((1, MOD_ROWS, d), lambda i: (i // per_seq, 0, 0)),
                  pl.BlockSpec(memory_space=pl.ANY)],
        out_specs=pl.BlockSpec((tn, d), lambda i: (i, 0)),
        out_shape=jax.ShapeDtypeStruct((n, d), f32),
        scratch_shapes=[pltpu.VMEM((2, EXPERT_TOPK, tn, s, LANE), f32), pltpu.VMEM((tn * s, LANE), f32),
                        pltpu.SemaphoreType.DMA((2,))],
        compiler_params=_cparams(1),
        name="combine",
    )(pos_flat, gate_rep, base, mod8, ys)


def _layer(x, c, w_ada, b_ada, g_mix, w_in, q_norm, k_norm, w_attn_up, pool_lin, pool_scale, w_pool_up,
           w_out, g_ffn, w_router, router_bias, w1, w3, w2, ws1, ws3, ws2):
    batch, seq, d = x.shape
    n = batch * seq
    x2 = x.reshape(n, d)

    offs = [0, ATTN_WIDTH, ATTN_WIDTH + KV_WIDTH, ATTN_WIDTH + 2 * KV_WIDTH]
    w_in = w_in.astype(bf16)
    w_q = w_in[:, offs[0]:offs[1]]
    w_k = w_in[:, offs[1]:offs[2]]
    w_v = w_in[:, offs[2]:offs[3]]
    o_qi = offs[3]
    w_qi = w_in[:, o_qi:o_qi + IDX_HEADS * IDX_DIM]
    o_ki = o_qi + IDX_HEADS * IDX_DIM
    w_ki = w_in[:, o_ki:o_ki + IDX_DIM]
    o_wi = o_ki + IDX_DIM
    w_wi = w_in[:, o_wi:o_wi + IDX_HEADS]
    o_u = o_wi + IDX_HEADS
    w_u = w_in[:, o_u:o_u + POOL_WIDTH]
    o_g = o_u + POOL_WIDTH
    w_g = w_in[:, o_g:o_g + 2 * d]
    w_main = jnp.concatenate([w_g, w_q, w_qi, w_u, w_k, w_v], axis=1).astype(bf16)
    w_kiwi = jnp.concatenate(
        [w_ki, w_wi, jnp.zeros((d, KIWI_WIDTH - IDX_DIM - IDX_HEADS), w_in.dtype)], axis=1).astype(bf16)

    assert batch <= SUBLANES and seq % min(KEY_CHUNK, seq) == 0 and d % (2 * LANE) == 0
    c8 = jnp.zeros((SUBLANES, d), f32).at[:batch].set(c)
    mod = _ada(c8, w_ada, b_ada.reshape(1, -1))[:batch]
    mod8 = jnp.zeros((batch, MOD_ROWS, d), f32).at[:, :N_MOD].set(mod.reshape(batch, N_MOD, d))

    proj, kiwi = _inproj(x2, mod8, g_mix.reshape(1, d), w_main, w_kiwi, seq)
    attn = _attention(proj, kiwi, q_norm.reshape(1, -1), k_norm.reshape(1, -1), batch, seq)
    pool = _pool(proj, pool_lin.astype(bf16), pool_scale.reshape(1, -1), seq)
    merged = _merge(attn, pool, w_attn_up.astype(bf16), w_pool_up.astype(bf16), proj)
    wr_hi = w_router.astype(bf16)
    wr_lo = (w_router - wr_hi.astype(f32)).astype(bf16)
    x1, h2, logits = _outproj(merged, x2, mod8, g_ffn.reshape(1, d), w_out.astype(bf16),
                              jnp.concatenate([wr_hi, wr_lo], axis=1), seq)

    logits_t = logits.T
    bias_col = router_bias.reshape(N_EXPERTS, 1)
    counts = _route_counts(logits_t, bias_col)[:, 0].astype(i32)
    t = EXPERT_TILE
    tiles_e = (counts + t - 1) // t
    tile_end = jnp.cumsum(tiles_e)
    pstart = ((tile_end - tiles_e) * t).astype(f32).reshape(N_EXPERTS, 1)
    n_tiles = n * EXPERT_TOPK // t + N_EXPERTS
    n_used = tile_end[-1]
    tile_ids = jnp.minimum(jnp.arange(n_tiles, dtype=i32), n_used - 1)
    block_e = jnp.sum((tile_end[None, :] <= tile_ids[:, None]).astype(i32), axis=1)
    block_e = jnp.minimum(block_e, N_EXPERTS - 1)
    pad_tile = jnp.where(tiles_e > 0, (tile_end - 1) * t, -1).astype(i32)
    pad_tile = jnp.concatenate([pad_tile, n_used.reshape(1).astype(i32)])
    gate8, pos8 = _route_assign(logits_t, bias_col, pstart)
    pos_flat = pos8.reshape(-1)

    cum_used = jnp.cumsum((tiles_e > 0).astype(i32))
    slots = jnp.arange(N_EXPERTS, dtype=i32)
    used_experts = jnp.minimum(jnp.sum((cum_used[None, :] <= slots[:, None]).astype(i32), axis=1), N_EXPERTS - 1)
    tile_ord = jnp.sum(jnp.where(block_e[:, None] == slots[None, :], cum_used[None, :] - 1, 0), axis=1).astype(i32)
    n_used2 = jnp.stack([n_used, cum_used[-1]]).astype(i32)

    slabs = d // LANE
    xs, base = _dispatch(pos_flat, pad_tile, h2, x1, mod8, ws1.astype(bf16), ws3.astype(bf16), ws2.astype(bf16),
                         n_tiles * t, seq)
    ys = _experts(tile_ord, used_experts, n_used2, xs, w1, w3, w2)
    gate_rep = jnp.broadcast_to(gate8.T[:, :, None], (n, EXPERT_TOPK, LANE))
    out = _combine(pos_flat, gate_rep, base, mod8, ys.reshape(-1, slabs, LANE), seq)
    return out.reshape(batch, seq, d)


def kernel(x, c, w_ada, b_ada, g_mix, w_in, q_norm, k_norm, w_attn_up, pool_lin, pool_scale, w_pool_up, w_out, g_ffn, w_router, router_bias, w1, w3, w2, ws1, ws3, ws2):
    for l in range(w_ada.shape[0]):
        x = _layer(x, c, w_ada[l], b_ada[l], g_mix[l], w_in[l], q_norm[l], k_norm[l], w_attn_up[l], pool_lin[l],
                   pool_scale[l], w_pool_up[l], w_out[l], g_ffn[l], w_router[l], router_bias[l], w1[l], w3[l],
                   w2[l], ws1[l], ws3[l], ws2[l])
    return x
```
